```python
import jax
import jax.numpy as jnp
from jax import lax
import numpy as np

D_MODEL = 4096
BATCH = 2
SEQ = 4096
DEPTH = 2

CHUNK = 64
QBLOCK = 128
ROPE_THETA = 10000.0
LN_EPS = 1e-5
ALPHA = (2 * DEPTH) ** 0.25
BETA = (8 * DEPTH) ** -0.25

MIX_WIDTH = D_MODEL
GROUP_WIDTH = MIX_WIDTH // 2

A_HEAD_DIM = 128
A_HEADS = GROUP_WIDTH // A_HEAD_DIM
IDX_HEADS = 16
IDX_DIM = 64
TOPK_MAX = 256
B_HEAD_DIM = 128
B_HEADS = GROUP_WIDTH // B_HEAD_DIM
AB_SIZES = (GROUP_WIDTH,) * 3 + (IDX_HEADS * IDX_DIM, IDX_DIM, IDX_HEADS) + (GROUP_WIDTH,) * 4

C_HEAD_DIM = 64
C_HEADS = GROUP_WIDTH // C_HEAD_DIM
C_DECAY_RANK = 128
C_ICLR_RANK = 128
C_GATE_RANK = 256
C_EPS = 64e-5
C_SIZES = (GROUP_WIDTH,) * 3 + (C_DECAY_RANK, C_ICLR_RANK, C_GATE_RANK)
C_COLS = sum(C_SIZES)
D_WIDTH = GROUP_WIDTH
D_BLOCKS = 16
D_BLOCK_DIM = D_WIDTH // D_BLOCKS
D_CONV = 4
LRU_C = 8.0
CD_SIZES = (C_COLS, D_WIDTH, D_WIDTH)

N_GROUPS = 4
EXPERTS_PER_GROUP = 8
N_EXPERTS = N_GROUPS * EXPERTS_PER_GROUP
TOP_K_IN_GROUP = 2
D_EXPERT = 512

N_EVEN = (DEPTH + 1) // 2
N_ODD = DEPTH // 2

kernel_name = 'hybrid_dsa_retention_rwkv7_rglru_hmoe'


def split_cols(p, sizes):
    offs = np.cumsum(sizes)[:-1].tolist()
    return jnp.split(p, offs, axis=-1)


def layer_norm(x, g, b):
    xf = x.astype(jnp.float32)
    mu = jnp.mean(xf, -1, keepdims=True)
    var = jnp.mean(jnp.square(xf - mu), -1, keepdims=True)
    return ((xf - mu) * lax.rsqrt(var + LN_EPS) * g + b).astype(x.dtype)


def head_norm(y, eps):
    yf = y.astype(jnp.float32)
    mu = jnp.mean(yf, -1, keepdims=True)
    var = jnp.mean(jnp.square(yf - mu), -1, keepdims=True)
    return (yf - mu) * lax.rsqrt(var + eps)


def rope(x, positions):
    d = x.shape[-1]
    half = d // 2
    freqs = ROPE_THETA ** (-jnp.arange(half, dtype=jnp.float32) / half)
    ang = positions.astype(jnp.float32)[..., None] * freqs
    cos = jnp.cos(ang)[:, :, None, :]
    sin = jnp.sin(ang)[:, :, None, :]
    xf = x.astype(jnp.float32)
    x1, x2 = xf[..., :half], xf[..., half:]
    return jnp.concatenate([x1 * cos - x2 * sin, x2 * cos + x1 * sin], -1).astype(x.dtype)


def dsa_sparse_attention(q, k, v, qi, ki, wi):
    Bsz, S, H, Dh = q.shape
    n_blk = S // QBLOCK
    topk = min(TOPK_MAX, S // 4)
    key_chunk = jnp.arange(S) // CHUNK
    idx_scale = IDX_DIM ** -0.5 * IDX_HEADS ** -0.5
    att_scale = Dh ** -0.5

    def to_blocks(t):
        return t.reshape(Bsz, n_blk, QBLOCK, *t.shape[2:]).swapaxes(0, 1)

    def block_fn(args):
        qb, qib, wib, blk = args
        q_chunk = (blk * QBLOCK + jnp.arange(QBLOCK)) // CHUNK
        rel = jax.nn.relu(jnp.einsum('bqhd,bsd->bqhs', qib, ki))
        score = jnp.einsum('bqhs,bqh->bqs', rel, wib).astype(jnp.float32) * idx_scale
        admissible = key_chunk[None, :] <= q_chunk[:, None]
        score = jnp.where(admissible[None], score, -jnp.inf)
        _, sel = lax.top_k(score, topk)
        valid = key_chunk[sel] <= q_chunk[None, :, None]
        k_sel = jax.vmap(lambda kb, ib: kb[ib])(k, sel)
        v_sel = jax.vmap(lambda vb, ib: vb[ib])(v, sel)
        logits = jnp.einsum('bqhd,bqkhd->bhqk', qb, k_sel).astype(jnp.float32) * att_scale
        logits = jnp.where(valid[:, None], logits, -jnp.inf)
        p = jax.nn.softmax(logits, axis=-1).astype(v.dtype)
        return jnp.einsum('bhqk,bqkhd->bqhd', p, v_sel)

    out = lax.map(block_fn, (to_blocks(q), to_blocks(qi), to_blocks(wi), jnp.arange(n_blk)))
    return out.swapaxes(0, 1).reshape(Bsz, S, H, Dh)


def retention_chunkwise(q, k, v):
    Bsz, S, H, dk = q.shape
    dv = v.shape[-1]
    nc = S // CHUNK
    log_g = jnp.log1p(-jnp.exp2(-5.0 - jnp.arange(H, dtype=jnp.float32)))
    pos = jnp.arange(CHUNK, dtype=jnp.float32)
    diff = pos[:, None] - pos[None, :]
    inner_decay = jnp.where(diff >= 0, jnp.exp(log_g[:, None, None] * jnp.maximum(diff, 0.0)), 0.0)
    q = q.reshape(Bsz, nc, CHUNK, H, dk)
    k = k.reshape(Bsz, nc, CHUNK, H, dk) * dk ** -0.5
    v = v.reshape(Bsz, nc, CHUNK, H, dv)
    scores = jnp.einsum('bnihd,bnjhd->bnhij', q, k) * inner_decay
    inner = jnp.einsum('bnhij,bnjhe->bnihe', scores, v)
    zeta = jnp.exp(log_g[None, :] * (CHUNK - 1.0 - pos)[:, None])
    kv = jnp.einsum('bnjhd,bnjhe->nbhde', k * zeta[:, :, None], v)
    chunk_decay = jnp.exp(log_g * CHUNK)[None, :, None, None]

    def step(state, kv_n):
        return chunk_decay * state + kv_n, state

    _, prev = lax.scan(step, jnp.zeros_like(kv[0]), kv)
    xi = jnp.exp(log_g[None, :] * (pos + 1.0)[:, None])
    cross = jnp.einsum('bnihd,nbhde->bnihe', q * xi[:, :, None], prev)
    return (inner + cross).reshape(Bsz, S, H, dv)


def mixer_sparse_retention(h, positions, w_in, w_out, gn_g):
    Bsz, S, _ = h.shape
    aq, ak, av, iq, ik, iw, bq, bk, bv, bg = split_cols(h @ w_in, AB_SIZES)
    aq = rope(aq.reshape(Bsz, S, A_HEADS, A_HEAD_DIM), positions)
    ak = rope(ak.reshape(Bsz, S, A_HEADS, A_HEAD_DIM), positions)
    av = av.reshape(Bsz, S, A_HEADS, A_HEAD_DIM)
    iq = rope(iq.reshape(Bsz, S, IDX_HEADS, IDX_DIM), positions)
    ik = rope(ik[:, :, None, :], positions)[:, :, 0]
    a_out = dsa_sparse_attention(aq, ak, av, iq, ik, iw)
    f32 = jnp.float32
    bq = rope(bq.reshape(Bsz, S, B_HEADS, B_HEAD_DIM), positions).astype(f32)
    bk = rope(bk.reshape(Bsz, S, B_HEADS, B_HEAD_DIM), positions).astype(f32)
    ret = retention_chunkwise(bq, bk, bv.reshape(Bsz, S, B_HEADS, B_HEAD_DIM).astype(f32))
    b_out = head_norm(ret, LN_EPS).reshape(Bsz, S, GROUP_WIDTH) * gn_g * jax.nn.silu(bg.astype(f32))
    mixed = jnp.concatenate([a_out.reshape(Bsz, S, GROUP_WIDTH), b_out.astype(h.dtype)], axis=-1)
    return mixed @ w_out


def rwkv7_time_mix(p, mu, w0, w_up, a0, a_up, g_up, k_k, k_a, r_k, ln_g, ln_b):
    Bsz, S, _ = p.shape
    f32 = jnp.float32
    prev = jnp.pad(p, ((0, 0), (1, 0), (0, 0)))[:, :-1]
    p = p + (prev - p) * mu
    r, k, v, dw, da, dg = split_cols(p, C_SIZES)
    w_log = -jax.nn.softplus(-(w0 + jnp.tanh(dw) @ w_up).astype(f32)) - 0.5
    decay = jnp.exp(-jnp.exp(w_log))
    a = jax.nn.sigmoid((a0 + da @ a_up).astype(f32))
    g = (jax.nn.sigmoid(dg) @ g_up).astype(f32)

    def heads(t):
        return t.astype(f32).reshape(Bsz, S, C_HEADS, C_HEAD_DIM)

    r, k, v, decay, a = heads(r), heads(k), heads(v), heads(decay), heads(a)
    kk = k * k_k.astype(f32).reshape(C_HEADS, C_HEAD_DIM)
    kk = kk * lax.rsqrt(jnp.maximum(jnp.sum(kk * kk, -1, keepdims=True), 1e-24))
    k = k * (1.0 + (a - 1.0) * k_a.astype(f32).reshape(C_HEADS, C_HEAD_DIM))

    def step(state, inp):
        r_t, w_t, k_t, v_t, kk_t, a_t = inp
        s_kk = jnp.einsum('bhvk,bhk->bhv', state, kk_t)
        state = (state * w_t[:, :, None, :] - s_kk[..., None] * (kk_t * a_t)[:, :, None, :]
                 + v_t[..., None] * k_t[:, :, None, :])
        return state, jnp.einsum('bhvk,bhk->bhv', state, r_t)

    xs = tuple(jnp.moveaxis(t, 1, 0) for t in (r, decay, k, v, kk, a))
    state0 = jnp.zeros((Bsz, C_HEADS, C_HEAD_DIM, C_HEAD_DIM), f32)
    _, y = lax.scan(step, state0, xs)
    y = jnp.moveaxis(y, 0, 1)
    y = head_norm(y, C_EPS).reshape(Bsz, S, GROUP_WIDTH) * ln_g + ln_b
    bonus = jnp.sum(r * k * r_k.astype(f32), -1, keepdims=True) * v
    out = (y + bonus.reshape(Bsz, S, GROUP_WIDTH)) * g
    return out.astype(p.dtype)


def _linear_combine(left, right):
    a_l, b_l = left
    a_r, b_r = right
    return a_l * a_r, a_r * b_l + b_r


def rg_lru_branch(gate_in, x_in, conv_w, conv_b, w_a, b_a, w_x, b_x, lam):
    Bsz, S, W = x_in.shape
    f32 = jnp.float32
    xc = lax.conv_general_dilated(x_in, conv_w[:, None, :], window_strides=(1,),
                                  padding=((D_CONV - 1, 0),),
                                  dimension_numbers=('NWC', 'WIO', 'NWC'),
                                  feature_group_count=W) + conv_b
    xb = xc.reshape(Bsz, S, D_BLOCKS, D_BLOCK_DIM)
    r = jax.nn.sigmoid((jnp.einsum('bsnc,ncd->bsnd', xb, w_a).reshape(Bsz, S, W) + b_a).astype(f32))
    i = jax.nn.sigmoid((jnp.einsum('bsnc,ncd->bsnd', xb, w_x).reshape(Bsz, S, W) + b_x).astype(f32))
    log_a = -LRU_C * r * jax.nn.softplus(-lam.astype(f32))
    a = jnp.exp(log_a)
    b = jnp.sqrt(-jnp.expm1(2.0 * log_a)) * (i * xc.astype(f32))
    _, hseq = lax.associative_scan(_linear_combine, (a, b), axis=1)
    return (hseq * jax.nn.gelu(gate_in.astype(f32))).astype(x_in.dtype)


def mixer_rwkv_lru(h, w_in, w_out, mu, w0, w_up, a0, a_up, g_up, k_k, k_a, r_k, ln_g, ln_b,
                   conv_w, conv_b, w_a, b_a, w_x, b_x, lam):
    pc, pg, px = split_cols(h @ w_in, CD_SIZES)
    c_out = rwkv7_time_mix(pc, mu, w0, w_up, a0, a_up, g_up, k_k, k_a, r_k, ln_g, ln_b)
    d_out = rg_lru_branch(pg, px, conv_w, conv_b, w_a, b_a, w_x, b_x, lam)
    return jnp.concatenate([c_out, d_out], axis=-1) @ w_out


def hier_moe(h, w_grp, b_grp, w_exp, b_exp, w_gate, w_up, w_down):
    Bsz, S, D = h.shape
    t = h.reshape(-1, D)
    N = t.shape[0]
    grp_prob = jax.nn.softmax((t @ w_grp + b_grp).astype(jnp.float32), axis=-1)
    grp_p, grp_idx = lax.top_k(grp_prob, 1)
    exp_logits = (t @ w_exp + b_exp).astype(jnp.float32).reshape(N, N_GROUPS, EXPERTS_PER_GROUP)
    in_grp = jnp.take_along_axis(exp_logits, grp_idx[:, :, None], axis=1)[:, 0]
    top_p, top_local = lax.top_k(jax.nn.softmax(in_grp, axis=-1), TOP_K_IN_GROUP)
    weights = grp_p * top_p / jnp.sum(top_p, -1, keepdims=True)
    flat_e = (grp_idx * EXPERTS_PER_GROUP + top_local).reshape(-1)
    order = jnp.argsort(flat_e)
    tok = order // TOP_K_IN_GROUP
    xs = t[tok]
    sizes = jnp.bincount(flat_e, length=N_EXPERTS).astype(jnp.int32)
    hid = jax.nn.silu(lax.ragged_dot(xs, w_gate, sizes)) * lax.ragged_dot(xs, w_up, sizes)
    ys = lax.ragged_dot(hid, w_down, sizes)
    ys = ys * weights.reshape(-1)[order][:, None].astype(ys.dtype)
    return jnp.zeros_like(t).at[tok].add(ys).reshape(Bsz, S, D)


def setup_inputs(seed: int = 0) -> dict:
    key = jax.random.key(seed)
    ks = iter(jax.random.split(key, 64))

    def nrm(shape, scale):
        return jax.random.normal(next(ks), shape, jnp.float32) * scale

    def uni(shape, lo, hi):
        return jax.random.uniform(next(ks), shape, jnp.float32, lo, hi)

    D = D_MODEL
    W = GROUP_WIDTH
    n_ab = sum(AB_SIZES)
    n_cd = sum(CD_SIZES)
    offsets = jax.random.randint(next(ks), (BATCH, 1), 0, 64) * CHUNK
    positions = (offsets + jnp.arange(SEQ, dtype=jnp.int32)[None, :]).astype(jnp.int32)
    s_lam = uni((N_ODD, D_WIDTH), 0.9, 0.999) ** (1.0 / LRU_C)
    return {
        'x': nrm((BATCH, SEQ, D), 1.0),
        'c': nrm((BATCH, D), 1.0),
        'positions': positions,
        'ada_w': nrm((D, 6 * D), 0.2 * D ** -0.5),
        'ada_b': nrm((6 * D,), 0.02),
        'ada_table': nrm((DEPTH, 6, D), 0.02),
        'ln_g': 1.0 + nrm((DEPTH, 2, D), 0.02),
        'ln_b': nrm((DEPTH, 2, D), 0.02),
        'ab_w_in': nrm((N_EVEN, D, n_ab), D ** -0.5),
        'ab_w_out': nrm((N_EVEN, MIX_WIDTH, D), BETA * MIX_WIDTH ** -0.5),
        'ret_gn_g': 1.0 + nrm((N_EVEN, W), 0.02),
        'cd_w_in': nrm((N_ODD, D, n_cd), D ** -0.5),
        'cd_w_out': nrm((N_ODD, MIX_WIDTH, D), BETA * MIX_WIDTH ** -0.5),
        'rwkv_mu': uni((N_ODD, C_COLS), 0.0, 1.0),
        'rwkv_w0': uni((N_ODD, W), -4.0, 1.0),
        'rwkv_w_up': nrm((N_ODD, C_DECAY_RANK, W), 0.1 * C_DECAY_RANK ** -0.5),
        'rwkv_a0': nrm((N_ODD, W), 0.5),
        'rwkv_a_up': nrm((N_ODD, C_ICLR_RANK, W), 0.1 * C_ICLR_RANK ** -0.5),
        'rwkv_g_up': nrm((N_ODD, C_GATE_RANK, W), C_GATE_RANK ** -0.5),
        'rwkv_k_k': 0.85 + nrm((N_ODD, W), 0.02),
        'rwkv_k_a': 1.0 + nrm((N_ODD, W), 0.02),
        'rwkv_r_k': nrm((N_ODD, C_HEADS, C_HEAD_DIM), 0.1),
        'rwkv_ln_g': 1.0 + nrm((N_ODD, W), 0.02),
        'rwkv_ln_b': nrm((N_ODD, W), 0.02),
        'lru_conv_w': nrm((N_ODD, D_CONV, D_WIDTH), D_CONV ** -0.5),
        'lru_conv_b': nrm((N_ODD, D_WIDTH), 0.02),
        'lru_w_a': nrm((N_ODD, D_BLOCKS, D_BLOCK_DIM, D_BLOCK_DIM), D_BLOCK_DIM ** -0.5),
        'lru_b_a': nrm((N_ODD, D_WIDTH), 0.02),
        'lru_w_x': nrm((N_ODD, D_BLOCKS, D_BLOCK_DIM, D_BLOCK_DIM), D_BLOCK_DIM ** -0.5),
        'lru_b_x': nrm((N_ODD, D_WIDTH), 0.02),
        'lru_lambda': jnp.log(s_lam) - jnp.log1p(-s_lam),
        'moe_w_grp': nrm((DEPTH, D, N_GROUPS), D ** -0.5),
        'moe_b_grp': nrm((DEPTH, N_GROUPS), 0.01),
        'moe_w_exp': nrm((DEPTH, D, N_EXPERTS), D ** -0.5),
        'moe_b_exp': nrm((DEPTH, N_EXPERTS), 0.01),
        'moe_w_gate': nrm((DEPTH, N_EXPERTS, D, D_EXPERT), D ** -0.5),
        'moe_w_up': nrm((DEPTH, N_EXPERTS, D, D_EXPERT), D ** -0.5),
        'moe_w_down': nrm((DEPTH, N_EXPERTS, D_EXPERT, D), BETA * D_EXPERT ** -0.5),
    }


def reference(x, c, positions, ada_w, ada_b, ada_table, ln_g, ln_b, ab_w_in, ab_w_out, ret_gn_g,
              cd_w_in, cd_w_out, rwkv_mu, rwkv_w0, rwkv_w_up, rwkv_a0, rwkv_a_up, rwkv_g_up,
              rwkv_k_k, rwkv_k_a, rwkv_r_k, rwkv_ln_g, rwkv_ln_b, lru_conv_w, lru_conv_b,
              lru_w_a, lru_b_a, lru_w_x, lru_b_x, lru_lambda, moe_w_grp, moe_b_grp, moe_w_exp,
              moe_b_exp, moe_w_gate, moe_w_up, moe_w_down):
    ada = (jax.nn.silu(c) @ ada_w + ada_b).reshape(c.shape[0], 6, D_MODEL)
    for layer in range(DEPTH):
        mod = (ada + ada_table[layer])[:, :, None, :]
        shift_m, scale_m, gate_m = mod[:, 0], mod[:, 1], mod[:, 2]
        shift_f, scale_f, gate_f = mod[:, 3], mod[:, 4], mod[:, 5]
        j = layer // 2
        h = x * (1.0 + scale_m) + shift_m
        if layer % 2 == 0:
            y = mixer_sparse_retention(h, positions, ab_w_in[j], ab_w_out[j], ret_gn_g[j])
        else:
            y = mixer_rwkv_lru(h, cd_w_in[j], cd_w_out[j], rwkv_mu[j], rwkv_w0[j], rwkv_w_up[j],
                               rwkv_a0[j], rwkv_a_up[j], rwkv_g_up[j], rwkv_k_k[j], rwkv_k_a[j],
                               rwkv_r_k[j], rwkv_ln_g[j], rwkv_ln_b[j], lru_conv_w[j], lru_conv_b[j],
                               lru_w_a[j], lru_b_a[j], lru_w_x[j], lru_b_x[j], lru_lambda[j])
        x = layer_norm(ALPHA * x + (1.0 + gate_m) * y, ln_g[layer, 0], ln_b[layer, 0])
        h = x * (1.0 + scale_f) + shift_f
        y = hier_moe(h, moe_w_grp[layer], moe_b_grp[layer], moe_w_exp[layer], moe_b_exp[layer],
                     moe_w_gate[layer], moe_w_up[layer], moe_w_down[layer])
        x = layer_norm(ALPHA * x + (1.0 + gate_f) * y, ln_g[layer, 1], ln_b[layer, 1])
    return x
```

```python
import functools

import jax
import jax.numpy as jnp
from jax import lax
from jax.experimental import pallas as pl
from jax.experimental.pallas import tpu as pltpu

f32 = jnp.float32
bf16 = jnp.bfloat16
i32 = jnp.int32

DEPTH = 2
CHUNK = 64
ROPE_THETA = 10000.0
LN_EPS = 1e-5
ALPHA = (2 * DEPTH) ** 0.25
HEAD_DIM = 128
IDX_HEADS = 16
IDX_DIM = 64
TOPK_MAX = 256
C_HEAD_DIM = 64
C_DECAY_RANK = 128
C_ICLR_RANK = 128
C_GATE_RANK = 256
C_EPS = 64e-5
D_BLOCKS = 16
D_CONV = 4
LRU_C = 8.0
N_GROUPS = 4
EXPERTS_PER_GROUP = 8
N_EXPERTS = N_GROUPS * EXPERTS_PER_GROUP
D_EXPERT = 512

LANES = 128
SUBLANES = 8
VMEM_BYTES_V7X = 64 * 1024 * 1024
VMEM_HEADROOM = 8 * 1024 * 1024


def _cparams(semantics, vmem_bytes):
    limit = min(int(vmem_bytes) + VMEM_HEADROOM, VMEM_BYTES_V7X - VMEM_HEADROOM)
    return pltpu.CompilerParams(dimension_semantics=semantics, vmem_limit_bytes=limit)


def _split_bf16(a):
    hi = a.astype(bf16)
    lo = (a - hi.astype(f32)).astype(bf16)
    return hi, lo


def _dot3(a, b, dims=(((1,), (0,)), ((), ()))):
    ah, al = _split_bf16(a)
    bh, bl = _split_bf16(b)
    dg = functools.partial(lax.dot_general, dimension_numbers=dims, preferred_element_type=f32)
    return dg(ah, bh) + dg(ah, bl) + dg(al, bh)


def _sigmoid(x):
    return 1.0 / (1.0 + jnp.exp(-x))


def _modcast_kernel(x_ref, sc_ref, sh_ref, hi_ref, lo_ref):
    h = x_ref[...] * (1.0 + sc_ref[0]) + sh_ref[0]
    hi, lo = _split_bf16(h)
    hi_ref[...] = hi
    lo_ref[...] = lo


def _modcast(x2, sc, sh, seq):
    m, k = x2.shape
    tm = 256
    row = pl.BlockSpec((tm, k), lambda i: (i, 0))
    per_batch = pl.BlockSpec((1, 1, k), lambda i: (i * tm // seq, 0, 0))
    return pl.pallas_call(
        _modcast_kernel,
        grid=(m // tm,),
        in_specs=[row, per_batch, per_batch],
        out_specs=[row, row],
        out_shape=[jax.ShapeDtypeStruct((m, k), bf16)] * 2,
        compiler_params=_cparams(("parallel",), 2 * tm * k * (4 + 2 + 2)),
        name="modcast",
    )(x2, sc, sh)


def _mm_kernel(*refs, nparts, epilogue):
    acc = None
    for p in range(nparts):
        d = jnp.dot(refs[p][...], refs[nparts + p][...].astype(bf16), preferred_element_type=f32)
        acc = d if acc is None else acc + d
    epilogue(acc, *refs[2 * nparts:])


def _mm3_kernel(xh_ref, xl_ref, w_ref, o_ref):
    wh, wl = _split_bf16(w_ref[...])
    xh = xh_ref[...]
    acc = jnp.dot(xh, wh, preferred_element_type=f32)
    acc += jnp.dot(xh, wl, preferred_element_type=f32)
    acc += jnp.dot(xl_ref[...], wh, preferred_element_type=f32)
    o_ref[...] = acc


def _store_epilogue(acc, o_ref):
    o_ref[...] = acc.astype(o_ref.dtype)


def _rope_epilogue(acc, c_ref, s_ref, o_ref, *, scale):
    c = c_ref[...]
    s = s_ref[...]
    for k in range(acc.shape[1] // LANES):
        blk = acc[:, k * LANES:(k + 1) * LANES]
        rot = blk * c + pltpu.roll(blk, LANES // 2, 1) * s
        if scale != 1.0:
            rot = rot * scale
        o_ref[:, k * LANES:(k + 1) * LANES] = rot.astype(o_ref.dtype)


def _mm(xs, w, *, col_off, n, tm, tn, out_dtype, epilogue=_store_epilogue, extra=(), extra_specs=()):
    m = xs[0].shape[0]
    kp = xs[0].shape[1]
    nparts = len(xs)
    assert all(x.shape == (m, kp) for x in xs) and w.shape[0] == nparts * kp
    assert m % tm == 0 and n % tn == 0 and col_off % tn == 0
    jo = col_off // tn
    x_specs = [pl.BlockSpec((tm, kp), lambda i, j: (i, 0)) for _ in xs]
    w_specs = [pl.BlockSpec((kp, tn), lambda i, j, p=p: (p, jo + j)) for p in range(nparts)]
    vmem = 2 * nparts * (tm * kp * 2 + kp * tn * 4) + nparts * kp * tn * 2 + 4 * tm * tn * 4
    return pl.pallas_call(
        functools.partial(_mm_kernel, nparts=nparts, epilogue=epilogue),
        grid=(m // tm, n // tn),
        in_specs=x_specs + w_specs + list(extra_specs),
        out_specs=pl.BlockSpec((tm, tn), lambda i, j: (i, j)),
        out_shape=jax.ShapeDtypeStruct((m, n), out_dtype),
        compiler_params=_cparams(("parallel", "arbitrary"), vmem),
        name="mm",
    )(*xs, *([w] * nparts), *extra)


def _mm3(xh, xl, w, *, col_off, n, tm, tn):
    m, k = xh.shape
    assert m % tm == 0 and n % tn == 0 and col_off % tn == 0
    jo = col_off // tn
    xspec = pl.BlockSpec((tm, k), lambda i, j: (i, 0))
    vmem = 2 * (2 * tm * k * 2 + k * tn * 4) + 2 * k * tn * 2 + 4 * tm * tn * 4
    return pl.pallas_call(
        _mm3_kernel,
        grid=(m // tm, n // tn),
        in_specs=[xspec, xspec, pl.BlockSpec((k, tn), lambda i, j: (0, jo + j))],
        out_specs=pl.BlockSpec((tm, tn), lambda i, j: (i, j)),
        out_shape=jax.ShapeDtypeStruct((m, n), f32),
        compiler_params=_cparams(("parallel", "arbitrary"), vmem),
        name="mm3",
    )(xh, xl, w)


def _res_ln(x, y, gate, g, b):
    z = ALPHA * x + (1.0 + gate) * y
    mu = jnp.mean(z, axis=-1, keepdims=True)
    zc = z - mu
    var = jnp.mean(zc * zc, axis=-1, keepdims=True)
    return zc * lax.rsqrt(var + LN_EPS) * g + b


def _ln_router_kernel(x_ref, y_ref, gate_ref, g_ref, b_ref, sc_ref, sh_ref, wr_ref, br_ref,
                      xo_ref, h_ref, rw_ref, rid_ref):
    xn = _res_ln(x_ref[...], y_ref[...], gate_ref[0], g_ref[...], b_ref[...])
    xo_ref[...] = xn
    h = xn * (1.0 + sc_ref[0]) + sh_ref[0]
    h_ref[...] = h
    logits = _dot3(h, wr_ref[...]) + br_ref[...]
    lane = lax.broadcasted_iota(i32, logits.shape, 1)
    neg = jnp.float32(-jnp.inf)
    is_grp = lane < N_GROUPS
    gl = jnp.where(is_grp, logits, neg)
    gmax = jnp.max(gl, axis=1, keepdims=True)
    gidx = jnp.min(jnp.where(gl == gmax, lane, LANES), axis=1, keepdims=True)
    gsum = jnp.sum(jnp.where(is_grp, jnp.exp(gl - gmax), 0.0), axis=1, keepdims=True)
    grp_p = 1.0 / gsum
    lo = N_GROUPS + EXPERTS_PER_GROUP * gidx
    in_grp = jnp.logical_and(lane >= lo, lane < lo + EXPERTS_PER_GROUP)
    el = jnp.where(in_grp, logits, neg)
    m1 = jnp.max(el, axis=1, keepdims=True)
    i1 = jnp.min(jnp.where(el == m1, lane, LANES), axis=1, keepdims=True)
    el2 = jnp.where(lane == i1, neg, el)
    m2 = jnp.max(el2, axis=1, keepdims=True)
    i2 = jnp.min(jnp.where(el2 == m2, lane, LANES), axis=1, keepdims=True)
    e2 = jnp.exp(m2 - m1)
    w1 = grp_p / (1.0 + e2)
    w2 = grp_p * e2 / (1.0 + e2)
    rw_ref[...] = jnp.where(lane == 0, w1, jnp.where(lane == 1, w2, 0.0))
    rid_ref[...] = jnp.where(lane == 0, i1 - N_GROUPS, jnp.where(lane == 1, i2 - N_GROUPS, 0))


def _ln_router(x2, y2, gate, g, b, sc, sh, wr, br, seq):
    m, d = x2.shape
    tm = 256
    row = pl.BlockSpec((tm, d), lambda i: (i, 0))
    per_batch = pl.BlockSpec((1, 1, d), lambda i: (i * tm // seq, 0, 0))
    vec = pl.BlockSpec((1, d), lambda i: (0, 0))
    small = pl.BlockSpec((tm, LANES), lambda i: (i, 0))
    return pl.pallas_call(
        _ln_router_kernel,
        grid=(m // tm,),
        in_specs=[row, row, per_batch, vec, vec, per_batch, per_batch,
                  pl.BlockSpec((d, LANES), lambda i: (0, 0)), pl.BlockSpec((1, LANES), lambda i: (0, 0))],
        out_specs=[row, row, small, small],
        out_shape=[jax.ShapeDtypeStruct((m, d), f32), jax.ShapeDtypeStruct((m, d), f32),
                   jax.ShapeDtypeStruct((m, LANES), f32), jax.ShapeDtypeStruct((m, LANES), i32)],
        compiler_params=_cparams(("parallel",), 12 * tm * d * 4 + 2 * d * LANES * 4),
        name="ln_router",
    )(x2, y2, gate, g, b, sc, sh, wr, br)


MOE_TM = 256
MOE_HALF = D_EXPERT // 2


def _row_copy(src_hbm, row, dst_vmem, r, sem):
    return pltpu.make_async_copy(src_hbm.at[pl.ds(row, 1), :], dst_vmem.at[pl.ds(r, 1), :], sem)


def _gather_rows(idx_ref, base, src_hbm, dst_vmem, sem, nrows, stride=1):
    def issue(r, c):
        _row_copy(src_hbm, idx_ref[base + r * stride], dst_vmem, r, sem).start()
        return c

    def wait(r, c):
        _row_copy(src_hbm, 0, dst_vmem, r, sem).wait()
        return c

    lax.fori_loop(0, nrows, issue, 0)
    lax.fori_loop(0, nrows, wait, 0)


def _moe_ffn_kernel(te_ref, tv_ref, tok_ref, h_hbm, wg_ref, wu_ref, wd_ref, o_ref, xbuf, xb16, sem):
    i = pl.program_id(0)
    hf = pl.program_id(1)
    valid = tv_ref[i] > 0

    @pl.when(jnp.logical_and(valid, hf == 0))
    def _gather():
        _gather_rows(tok_ref, i * MOE_TM, h_hbm, xbuf, sem, MOE_TM)
        xb16[...] = xbuf[...].astype(bf16)

    @pl.when(valid)
    def _compute():
        xb = xb16[...]
        gate = jnp.dot(xb, wg_ref[0].astype(bf16), preferred_element_type=f32)
        up = jnp.dot(xb, wu_ref[0].astype(bf16), preferred_element_type=f32)
        hid = gate * _sigmoid(gate) * up
        y = jnp.dot(hid.astype(bf16), wd_ref[0].astype(bf16), preferred_element_type=f32)

        @pl.when(hf == 0)
        def _first():
            o_ref[...] = y

        @pl.when(hf != 0)
        def _rest():
            o_ref[...] += y

    @pl.when(jnp.logical_and(jnp.logical_not(valid), hf == 0))
    def _empty():
        o_ref[...] = jnp.zeros_like(o_ref)


def _moe_ffn(h2, w_gate, w_up, w_down, tile_e, tile_valid, row_tok, n_tiles):
    n, d = h2.shape
    tm = MOE_TM
    nh = D_EXPERT // MOE_HALF
    grid_spec = pltpu.PrefetchScalarGridSpec(
        num_scalar_prefetch=3,
        grid=(n_tiles, nh),
        in_specs=[
            pl.BlockSpec(memory_space=pl.ANY),
            pl.BlockSpec((1, d, MOE_HALF), lambda i, hf, te, tv, tok: (te[i], 0, hf)),
            pl.BlockSpec((1, d, MOE_HALF), lambda i, hf, te, tv, tok: (te[i], 0, hf)),
            pl.BlockSpec((1, MOE_HALF, d), lambda i, hf, te, tv, tok: (te[i], hf, 0)),
        ],
        out_specs=pl.BlockSpec((tm, d), lambda i, hf, te, tv, tok: (i, 0)),
        scratch_shapes=[pltpu.VMEM((tm, d), f32), pltpu.VMEM((tm, d), bf16), pltpu.SemaphoreType.DMA],
    )
    vmem = 2 * 3 * d * MOE_HALF * 4 + 3 * d * MOE_HALF * 2 + 2 * tm * d * 4 + tm * d * 6 + 4 * tm * d * 4
    return pl.pallas_call(
        _moe_ffn_kernel,
        grid_spec=grid_spec,
        out_shape=jax.ShapeDtypeStruct((n_tiles * tm, d), f32),
        compiler_params=_cparams(("arbitrary", "arbitrary"), vmem),
        name="moe_ffn",
    )(tile_e, tile_valid, row_tok, h2, w_gate, w_up, w_down)


def _moe_combine_kernel(pos_ref, ys_hbm, rw_ref, x_ref, gate_ref, g_ref, b_ref, o_ref, buf0, buf1, sem):
    i = pl.program_id(0)
    tm = x_ref.shape[0]
    _gather_rows(pos_ref, 2 * i * tm, ys_hbm, buf0, sem, tm, stride=2)
    _gather_rows(pos_ref, 2 * i * tm + 1, ys_hbm, buf1, sem, tm, stride=2)
    rw = rw_ref[...]
    y = rw[:, 0:1] * buf0[...] + rw[:, 1:2] * buf1[...]
    o_ref[...] = _res_ln(x_ref[...], y, gate_ref[0], g_ref[...], b_ref[...])


def _moe_combine(pos, ys, rw, x2, gate, g, b, seq):
    n, d = x2.shape
    tm = 256
    row = lambda i, pos: (i, 0)
    grid_spec = pltpu.PrefetchScalarGridSpec(
        num_scalar_prefetch=1,
        grid=(n // tm,),
        in_specs=[
            pl.BlockSpec(memory_space=pl.ANY),
            pl.BlockSpec((tm, LANES), row),
            pl.BlockSpec((tm, d), row),
            pl.BlockSpec((1, 1, d), lambda i, pos: (i * tm // seq, 0, 0)),
            pl.BlockSpec((1, d), lambda i, pos: (0, 0)),
            pl.BlockSpec((1, d), lambda i, pos: (0, 0)),
        ],
        out_specs=pl.BlockSpec((tm, d), row),
        scratch_shapes=[pltpu.VMEM((tm, d), f32), pltpu.VMEM((tm, d), f32), pltpu.SemaphoreType.DMA],
    )
    return pl.pallas_call(
        _moe_combine_kernel,
        grid_spec=grid_spec,
        out_shape=jax.ShapeDtypeStruct((n, d), f32),
        compiler_params=_cparams(("arbitrary",), 10 * tm * d * 4),
        name="moe_combine",
    )(pos, ys, rw, x2, gate, g, b)


def _moe_plan(eid, n_tiles):
    tm = MOE_TM
    flat_e = eid.reshape(-1)
    onehot = (flat_e[:, None] == jnp.arange(N_EXPERTS, dtype=i32)[None, :]).astype(i32)
    csum = jnp.cumsum(onehot, axis=0)
    rank = jnp.sum((csum - onehot) * onehot, axis=1)
    counts = csum[-1]
    padded = ((counts + tm - 1) // tm) * tm
    ends = jnp.cumsum(padded)
    pos = ((ends - padded)[flat_e] + rank).astype(i32)
    tile_start = jnp.arange(n_tiles, dtype=i32) * tm
    tile_valid = (tile_start < ends[-1]).astype(i32)
    tile_e = jnp.searchsorted(ends, tile_start, side="right").astype(i32)
    last_valid = jnp.maximum(ends[-1] // tm - 1, 0)
    tile_e = jnp.where(tile_valid > 0, tile_e, tile_e[last_valid])
    row_tok = jnp.zeros((n_tiles * tm,), i32).at[pos].set(jnp.arange(flat_e.shape[0], dtype=i32) // 2)
    return pos, tile_e, tile_valid, row_tok


def _moe_layer(x2, y2, gate_m, ln_g, ln_b, scale_f, shift_f, gate_f, ln_g2, ln_b2,
               w_grp, b_grp, w_exp, b_exp, w_gate, w_up, w_down, seq):
    n, d = x2.shape
    pad = LANES - N_GROUPS - N_EXPERTS
    wr = jnp.concatenate([w_grp, w_exp, jnp.zeros((d, pad), f32)], axis=1)
    br = jnp.concatenate([b_grp, b_exp, jnp.zeros((pad,), f32)])[None, :]
    x1, h, rw, rid = _ln_router(x2, y2, gate_m, ln_g[None, :], ln_b[None, :], scale_f, shift_f, wr, br, seq)
    n_tiles = (2 * n) // MOE_TM + N_EXPERTS
    pos, tile_e, tile_valid, row_tok = _moe_plan(rid[:, :2], n_tiles)
    ys = _moe_ffn(h, w_gate, w_up, w_down, tile_e, tile_valid, row_tok, n_tiles)
    return _moe_combine(pos, ys, rw, x1, gate_f, ln_g2[None, :], ln_b2[None, :], seq)


def _rope_table_kernel(pos_ref, fr_ref, sg_ref, c_ref, s_ref):
    ang = pos_ref[...].astype(f32) * fr_ref[...]
    c_ref[...] = jnp.cos(ang)
    s_ref[...] = jnp.sin(ang) * sg_ref[...]


def _rope_tables(pos2, head_dim):
    m = pos2.shape[0]
    half = head_dim // 2
    freqs = ROPE_THETA ** (-jnp.arange(half, dtype=f32) / half)
    reps = LANES // head_dim
    fr = jnp.tile(jnp.concatenate([freqs, freqs]), reps)[None, :]
    sg = jnp.tile(jnp.concatenate([-jnp.ones((half,), f32), jnp.ones((half,), f32)]), reps)[None, :]
    tm = min(m, 1024)
    vec = pl.BlockSpec((1, LANES), lambda i: (0, 0))
    out = pl.BlockSpec((tm, LANES), lambda i: (i, 0))
    return pl.pallas_call(
        _rope_table_kernel,
        grid=(m // tm,),
        in_specs=[pl.BlockSpec((tm, 1), lambda i: (i, 0)), vec, vec],
        out_specs=[out, out],
        out_shape=[jax.ShapeDtypeStruct((m, LANES), f32)] * 2,
        compiler_params=_cparams(("parallel",), 8 * tm * LANES * 4),
        name="rope_tables",
    )(pos2, fr, sg)


def _proj_rope(h_hi, w, c_tab, s_tab, *, col_off, n, scale, tm, tn):
    tab = pl.BlockSpec((tm, LANES), lambda i, j: (i, 0))
    return _mm([h_hi], w, col_off=col_off, n=n, tm=tm, tn=tn, out_dtype=bf16,
               epilogue=functools.partial(_rope_epilogue, scale=scale),
               extra=(c_tab, s_tab), extra_specs=(tab, tab))


IDX_W = IDX_HEADS * IDX_DIM
IDX_RAW = IDX_W + LANES


def _idx_rope_kernel(x_ref, c_ref, s_ref, q_ref, kw_ref):
    c = c_ref[...]
    s = s_ref[...]
    lane = lax.broadcasted_iota(i32, c.shape, 1)
    first = (lane & (IDX_DIM // 2)) == 0

    def rope(blk):
        swapped = jnp.where(first, pltpu.roll(blk, LANES - IDX_DIM // 2, 1), pltpu.roll(blk, IDX_DIM // 2, 1))
        return blk * c + swapped * s

    for k in range(IDX_W // LANES):
        q_ref[:, k * LANES:(k + 1) * LANES] = rope(x_ref[:, k * LANES:(k + 1) * LANES])
    kw = x_ref[:, IDX_W:IDX_RAW]
    kw_ref[...] = jnp.where(lane < IDX_DIM, rope(kw), kw)


def _idx_rope(raw, c_tab, s_tab):
    m = raw.shape[0]
    tm = min(m, 512)
    tab = pl.BlockSpec((tm, LANES), lambda i: (i, 0))
    return pl.pallas_call(
        _idx_rope_kernel,
        grid=(m // tm,),
        in_specs=[pl.BlockSpec((tm, IDX_RAW), lambda i: (i, 0)), tab, tab],
        out_specs=[pl.BlockSpec((tm, IDX_W), lambda i: (i, 0)), tab],
        out_shape=[jax.ShapeDtypeStruct((m, IDX_W), f32), jax.ShapeDtypeStruct((m, LANES), f32)],
        compiler_params=_cparams(("parallel",), 8 * tm * IDX_RAW * 4),
        name="idx_rope",
    )(raw, c_tab, s_tab)


_NT = (((1,), (1,)), ((), ()))
INT_MIN = -2 ** 31


def _dsa_index_kernel(q_ref, kw_ref, qw_ref, m_ref, key_scr, *, topk, scale):
    i = pl.program_id(1)
    tq, s_len = m_ref.shape
    kh, kl = _split_bf16(kw_ref[:, 0:IDX_DIM])
    qw = qw_ref[...]
    score = jnp.zeros((tq, s_len), f32)
    for hd in range(IDX_HEADS):
        qh, ql = _split_bf16(q_ref[:, hd * IDX_DIM:(hd + 1) * IDX_DIM])
        dg = functools.partial(lax.dot_general, dimension_numbers=_NT, preferred_element_type=f32)
        rel = dg(qh, kh) + dg(qh, kl) + dg(ql, kh)
        score = score + jnp.maximum(rel, 0.0) * qw[:, IDX_DIM + hd:IDX_DIM + hd + 1]
    score = score * scale
    q_chunk = (i * tq + lax.broadcasted_iota(i32, (tq, 1), 0)) // CHUNK
    k_idx = lax.broadcasted_iota(i32, (tq, s_len), 1)
    adm = k_idx < (q_chunk + 1) * CHUNK
    bits = pltpu.bitcast(score, i32)
    key = jnp.where(bits < 0, bits ^ 0x7FFFFFFF, bits)
    key_scr[...] = jnp.where(adm, key, INT_MIN)

    def count_ge(cand):
        return jnp.sum((key_scr[...] >= cand).astype(f32), axis=1, keepdims=True)

    kf = jnp.float32(topk)
    cur = jnp.where(count_ge(jnp.zeros((tq, 1), i32)) >= kf, 0, INT_MIN).astype(i32)

    def body(it, cur):
        cand = cur + jnp.left_shift(jnp.int32(1), 30 - it)
        return jnp.where(count_ge(cand) >= kf, cand, cur)

    thr = lax.fori_loop(0, 31, body, cur)
    sel = jnp.logical_and(adm, key_scr[...] >= thr)
    m_ref[...] = sel.astype(bf16)


def _dsa_index(iq, ikw, batch, seq):
    m = iq.shape[0]
    tq = min(seq, 256)
    nq = seq // tq
    topk = min(TOPK_MAX, seq // 4)
    scale = IDX_DIM ** -0.5 * IDX_HEADS ** -0.5
    return pl.pallas_call(
        functools.partial(_dsa_index_kernel, topk=topk, scale=scale),
        grid=(batch, nq),
        in_specs=[pl.BlockSpec((tq, IDX_W), lambda b, i: (b * nq + i, 0)),
                  pl.BlockSpec((seq, LANES), lambda b, i: (b, 0)),
                  pl.BlockSpec((tq, LANES), lambda b, i: (b * nq + i, 0))],
        out_specs=pl.BlockSpec((tq, seq), lambda b, i: (b * nq + i, 0)),
        out_shape=jax.ShapeDtypeStruct((m, seq), bf16),
        scratch_shapes=[pltpu.VMEM((tq, seq), i32)],
        compiler_params=_cparams(("parallel", "parallel"), 8 * tq * seq * 4 + 4 * seq * LANES * 4),
        name="dsa_index",
    )(iq, ikw, ikw)


ATT_HG = 4
MASKED = -1e30


def _dsa_attn_kernel(q_ref, k_ref, v_ref, m_ref, o_ref):
    sel = m_ref[...] > 0
    for h in range(ATT_HG):
        sl = slice(h * HEAD_DIM, (h + 1) * HEAD_DIM)
        logits = lax.dot_general(q_ref[:, sl], k_ref[:, sl], _NT, preferred_element_type=f32)
        logits = jnp.where(sel, logits, MASKED)
        mx = jnp.max(logits, axis=1, keepdims=True)
        p = jnp.exp(logits - mx)
        denom = jnp.sum(p, axis=1, keepdims=True)
        o = jnp.dot(p.astype(bf16), v_ref[:, sl], preferred_element_type=f32) / denom
        o_ref[:, sl] = o.astype(o_ref.dtype)


def _dsa_attn(q, k, v, mask, batch, seq):
    m, w = q.shape
    tq = min(seq, 256)
    nq = seq // tq
    gw = ATT_HG * HEAD_DIM
    qspec = pl.BlockSpec((tq, gw), lambda b, g, i: (b * nq + i, g))
    kvspec = pl.BlockSpec((seq, gw), lambda b, g, i: (b, g))
    vmem = 2 * (2 * tq * gw * 2 + 2 * seq * gw * 2 + tq * seq * 2) + 6 * tq * seq * 4
    return pl.pallas_call(
        _dsa_attn_kernel,
        grid=(batch, w // gw, nq),
        in_specs=[qspec, kvspec, kvspec, pl.BlockSpec((tq, seq), lambda b, g, i: (b * nq + i, 0))],
        out_specs=qspec,
        out_shape=jax.ShapeDtypeStruct((m, w), bf16),
        compiler_params=_cparams(("parallel", "parallel", "arbitrary"), vmem),
        name="dsa_attn",
    )(q, k, v, mask)


RET_HG = 4


def _retention_kernel(q_ref, k_ref, v_ref, g_ref, gn_ref, lg_ref, o_ref, state):
    c = pl.program_id(2)
    t = q_ref.shape[0]

    @pl.when(c == 0)
    def _init():
        state[...] = jnp.zeros_like(state)

    ri = lax.broadcasted_iota(i32, (t, t), 0)
    ci = lax.broadcasted_iota(i32, (t, t), 1)
    diff = (ri - ci).astype(f32)
    pos = lax.broadcasted_iota(i32, (t, 1), 0).astype(f32)
    for h in range(RET_HG):
        sl = slice(h * HEAD_DIM, (h + 1) * HEAD_DIM)
        lg = lg_ref[:, h * HEAD_DIM:h * HEAD_DIM + 1]
        q = q_ref[:, sl]
        k = k_ref[:, sl]
        v = v_ref[:, sl]
        decay = jnp.where(diff >= 0, jnp.exp(lg * jnp.maximum(diff, 0.0)), 0.0)
        scores = lax.dot_general(q, k, _NT, preferred_element_type=f32) * decay
        inner = jnp.dot(scores.astype(bf16), v, preferred_element_type=f32)
        st = state[h]
        cross = jnp.dot(q, st.astype(bf16), preferred_element_type=f32) * jnp.exp(lg * (pos + 1.0))
        kz = k.astype(f32) * jnp.exp(lg * (t - 1.0 - pos))
        kv = jnp.dot(kz.T.astype(bf16), v, preferred_element_type=f32)
        state[h] = jnp.exp(lg * t) * st + kv
        ret = inner + cross
        mu = jnp.mean(ret, axis=1, keepdims=True)
        rc = ret - mu
        var = jnp.mean(rc * rc, axis=1, keepdims=True)
        gate = g_ref[:, sl]
        out = rc * lax.rsqrt(var + LN_EPS) * gn_ref[:, sl] * (gate * _sigmoid(gate))
        o_ref[:, sl] = out.astype(o_ref.dtype)


def _retention(q, k, v, g, gn_g, batch, seq):
    m, w = q.shape
    heads = w // HEAD_DIM
    t = min(seq, 256)
    nc = seq // t
    gw = RET_HG * HEAD_DIM
    log_g = jnp.log1p(-jnp.exp2(-5.0 - jnp.arange(heads, dtype=f32)))
    lg = jnp.repeat(log_g, HEAD_DIM)[None, :]
    blk = pl.BlockSpec((t, gw), lambda b, gi, c: (b * nc + c, gi))
    vec = pl.BlockSpec((1, gw), lambda b, gi, c: (0, gi))
    return pl.pallas_call(
        _retention_kernel,
        grid=(batch, w // gw, nc),
        in_specs=[blk, blk, blk, blk, vec, vec],
        out_specs=blk,
        out_shape=jax.ShapeDtypeStruct((m, w), bf16),
        scratch_shapes=[pltpu.VMEM((RET_HG, HEAD_DIM, HEAD_DIM), f32)],
        compiler_params=_cparams(("parallel", "parallel", "arbitrary"), 16 * t * gw * 4 + 8 * t * t * 4),
        name="retention",
    )(q, k, v, g, gn_g[None, :], lg)


def _mixer_sparse_retention(h_hi, h_lo, pos2, w_in, w_out, gn_g, batch, seq):
    m, d = h_hi.shape
    gw = d // 2
    tm = min(m, 1024)
    tn = 512
    c128, s128 = _rope_tables(pos2, HEAD_DIM)
    c64, s64 = _rope_tables(pos2, IDX_DIM)
    rope = functools.partial(_proj_rope, h_hi, c_tab=c128, s_tab=s128, n=gw, tm=tm, tn=tn)
    plain = functools.partial(_mm, [h_hi], n=gw, tm=tm, tn=tn)
    aq = rope(w_in, col_off=0, scale=HEAD_DIM ** -0.5)
    ak = rope(w_in, col_off=gw, scale=1.0)
    av = plain(w_in, col_off=2 * gw, out_dtype=bf16)
    idx_raw = _mm3(h_hi, h_lo, w_in, col_off=3 * gw, n=IDX_RAW, tm=min(m, 512), tn=IDX_RAW // 3)
    iq, ikw = _idx_rope(idx_raw, c64, s64)
    mask = _dsa_index(iq, ikw, batch, seq)
    a_out = _dsa_attn(aq, ak, av, mask, batch, seq)
    w_b = w_in[:, 3 * gw + IDX_W + IDX_DIM + IDX_HEADS:]
    bq = rope(w_b, col_off=0, scale=1.0)
    bk = rope(w_b, col_off=gw, scale=HEAD_DIM ** -0.5)
    bv = plain(w_b, col_off=2 * gw, out_dtype=bf16)
    bg = plain(w_b, col_off=3 * gw, out_dtype=f32)
    b_out = _retention(bq, bk, bv, bg, gn_g, batch, seq)
    return _mm([a_out, b_out], w_out, col_off=0, n=d, tm=tm, tn=tn, out_dtype=f32)


def _seg_sum64(x):
    lane = lax.broadcasted_iota(i32, (x.shape[0], LANES), 1)
    cols = []
    for k in range(x.shape[1] // LANES):
        s = x[:, k * LANES:(k + 1) * LANES]
        for sh in (1, 2, 4, 8, 16, 32):
            s = s + jnp.where((lane & sh) == 0, pltpu.roll(s, LANES - sh, 1), pltpu.roll(s, sh, 1))
        cols.append(s)
    return jnp.concatenate(cols, axis=1) if len(cols) > 1 else cols[0]


def _neg_softplus_neg(z):
    return jnp.minimum(z, 0.0) - jnp.log(1.0 + jnp.exp(-jnp.abs(z)))


def _rwkv_prep_kernel(p_ref, pp_ref, mu_ref, w0_ref, wup_ref, a0_ref, aup_ref, gup_ref, kk_ref, ka_ref, rk_ref,
                      r_o, w_o, k_o, v_o, kk_o, kka_o, g_o, bon_o, *, seq, gw):
    i = pl.program_id(0)
    tm = p_ref.shape[0]
    p = p_ref[...]
    prev_row = jnp.where((i * tm) % seq == 0, 0.0, pp_ref[SUBLANES - 1:SUBLANES, :])
    row = lax.broadcasted_iota(i32, (tm, 1), 0)
    shifted = jnp.where(row == 0, prev_row, pltpu.roll(p, 1, 0))
    pm = p + (shifted - p) * mu_ref[...]
    r = pm[:, 0:gw]
    k = pm[:, gw:2 * gw]
    v = pm[:, 2 * gw:3 * gw]
    o = 3 * gw
    dw = pm[:, o:o + C_DECAY_RANK]
    da = pm[:, o + C_DECAY_RANK:o + C_DECAY_RANK + C_ICLR_RANK]
    dg = pm[:, o + C_DECAY_RANK + C_ICLR_RANK:]
    w_log = _neg_softplus_neg(w0_ref[...] + _dot3(jnp.tanh(dw), wup_ref[...])) - 0.5
    decay = jnp.exp(-jnp.exp(w_log))
    a = _sigmoid(a0_ref[...] + _dot3(da, aup_ref[...]))
    g = _dot3(_sigmoid(dg), gup_ref[...])
    kk = k * kk_ref[...]
    kk = kk * lax.rsqrt(jnp.maximum(_seg_sum64(kk * kk), 1e-24))
    k2 = k * (1.0 + (a - 1.0) * ka_ref[...])
    r_o[...] = r
    w_o[...] = decay
    k_o[...] = k2
    v_o[...] = v
    kk_o[...] = kk
    kka_o[...] = kk * a
    g_o[...] = g
    bon_o[...] = _seg_sum64(r * k2 * rk_ref[...]) * v


def _rwkv_prep(pc, mu, w0, w_up, a0, a_up, g_up, k_k, k_a, r_k, seq):
    m, cc = pc.shape
    gw = w0.shape[0]
    tm = 128
    nsub = tm // SUBLANES
    vec = pl.BlockSpec((1, gw), lambda i: (0, 0))
    out = pl.BlockSpec((tm, gw), lambda i: (i, 0))
    full = lambda a: pl.BlockSpec(a.shape, lambda i: (0, 0))
    return pl.pallas_call(
        functools.partial(_rwkv_prep_kernel, seq=seq, gw=gw),
        grid=(m // tm,),
        in_specs=[pl.BlockSpec((tm, cc), lambda i: (i, 0)),
                  pl.BlockSpec((SUBLANES, cc), lambda i: (jnp.maximum(i * nsub - 1, 0), 0)),
                  pl.BlockSpec((1, cc), lambda i: (0, 0)),
                  vec, full(w_up), vec, full(a_up), full(g_up), vec, vec, vec],
        out_specs=[out] * 8,
        out_shape=[jax.ShapeDtypeStruct((m, gw), f32)] * 8,
        compiler_params=_cparams(("parallel",), 2 * tm * cc * 4 + 2 * 8 * tm * gw * 4 + 16 * tm * gw * 4),
        name="rwkv_prep",
    )(pc, pc, mu[None, :], w0[None, :], w_up, a0[None, :], a_up, g_up, k_k[None, :], k_a[None, :],
      r_k.reshape(1, -1))


def _rwkv_scan_kernel(w_ref, kk_ref, ka_ref, k_ref, r_ref, v_ref, y_ref, z_ref):
    c = pl.program_id(0)
    steps = w_ref.shape[0]
    nslab = z_ref.shape[0]

    @pl.when(c == 0)
    def _init():
        z_ref[...] = jnp.zeros_like(z_ref)

    def step(t, carry):
        w = w_ref[t]
        kk = kk_ref[t]
        ka = ka_ref[t]
        k = k_ref[t]
        r = r_ref[t]
        vrows = v_ref[t]
        for s in range(nslab):
            z = z_ref[s]
            sk = jnp.sum(z * kk, axis=0, keepdims=True)
            zn = z * w - ka * sk + k * vrows[s:s + 1, :]
            z_ref[s] = zn
            y_ref[t, s:s + 1, :] = jnp.sum(zn * r, axis=0, keepdims=True)
        return carry

    lax.fori_loop(0, steps, step, 0)


def _to_scan_cols(a, batch, seq):
    heads = a.shape[1] // C_HEAD_DIM
    t = a.reshape(batch, seq, heads, C_HEAD_DIM).transpose(1, 3, 0, 2).reshape(seq, C_HEAD_DIM, batch * heads)
    return jnp.concatenate([t, t], axis=-1)


def _rwkv_scan(w, kk, kka, k2, r, v, batch, seq):
    heads = w.shape[1] // C_HEAD_DIM
    assert 2 * batch * heads == LANES
    nslab = C_HEAD_DIM // 2
    cols = [_to_scan_cols(a, batch, seq) for a in (w, kk, kka, k2, r)]
    vr = v.reshape(batch, seq, heads, nslab, 2).transpose(1, 3, 4, 0, 2).reshape(seq, nslab, LANES)
    steps = min(seq, 64)
    col_spec = pl.BlockSpec((steps, C_HEAD_DIM, LANES), lambda c: (c, 0, 0))
    row_spec = pl.BlockSpec((steps, nslab, LANES), lambda c: (c, 0, 0))
    y = pl.pallas_call(
        _rwkv_scan_kernel,
        grid=(seq // steps,),
        in_specs=[col_spec] * 5 + [row_spec],
        out_specs=row_spec,
        out_shape=jax.ShapeDtypeStruct((seq, nslab, LANES), f32),
        scratch_shapes=[pltpu.VMEM((nslab, C_HEAD_DIM, LANES), f32)],
        compiler_params=_cparams(("arbitrary",), 2 * 5 * steps * C_HEAD_DIM * LANES * 4 + 6 * steps * nslab * LANES * 4),
        name="rwkv_scan",
    )(*cols, vr)
    return y.reshape(seq, nslab, 2, batch, heads).transpose(3, 0, 4, 1, 2).reshape(batch * seq, heads * C_HEAD_DIM)


def _rwkv_post_kernel(y_ref, g_ref, bon_ref, lng_ref, lnb_ref, o_ref):
    y = y_ref[...]
    mu = _seg_sum64(y) * (1.0 / C_HEAD_DIM)
    yc = y - mu
    var = _seg_sum64(yc * yc) * (1.0 / C_HEAD_DIM)
    yn = yc * lax.rsqrt(var + C_EPS) * lng_ref[...] + lnb_ref[...]
    o_ref[...] = ((yn + bon_ref[...]) * g_ref[...]).astype(o_ref.dtype)


def _rwkv_post(y, g, bonus, ln_g, ln_b):
    m, gw = y.shape
    tm = 256
    blk = pl.BlockSpec((tm, gw), lambda i: (i, 0))
    vec = pl.BlockSpec((1, gw), lambda i: (0, 0))
    return pl.pallas_call(
        _rwkv_post_kernel,
        grid=(m // tm,),
        in_specs=[blk, blk, blk, vec, vec],
        out_specs=blk,
        out_shape=jax.ShapeDtypeStruct((m, gw), bf16),
        compiler_params=_cparams(("parallel",), 16 * tm * gw * 4),
        name="rwkv_post",
    )(y, g, bonus, ln_g[None, :], ln_b[None, :])


GELU_C = 0.7978845608028654


def _lru_kernel(px_ref, pg_ref, cw_ref, cb_ref, wa_ref, ba_ref, wx_ref, bx_ref, lam_ref, o_ref,
                tail, hcar, a_s, b_s):
    c = pl.program_id(1)
    t = px_ref.shape[0]

    @pl.when(c == 0)
    def _init():
        tail[...] = jnp.zeros_like(tail)
        hcar[...] = jnp.zeros_like(hcar)

    x = px_ref[...]
    ext = jnp.concatenate([tail[...], x], axis=0)
    xc = cb_ref[...]
    for j in range(D_CONV):
        off = SUBLANES - (D_CONV - 1) + j
        xc = xc + cw_ref[j:j + 1, :] * ext[off:off + t, :]
    tail[...] = x[t - SUBLANES:, :]
    lam = lam_ref[...]
    sp = jnp.maximum(-lam, 0.0) + jnp.log(1.0 + jnp.exp(-jnp.abs(lam)))
    bw = wa_ref.shape[1]
    for n in range(wa_ref.shape[0]):
        sl = slice(n * bw, (n + 1) * bw)
        xb = xc[:, sl]
        rg = _sigmoid(_dot3(xb, wa_ref[n]) + ba_ref[:, sl])
        ig = _sigmoid(_dot3(xb, wx_ref[n]) + bx_ref[:, sl])
        log_a = -LRU_C * rg * sp[:, sl]
        a_s[:, sl] = jnp.exp(log_a)
        th = jnp.tanh(log_a)
        one_minus_a2 = -2.0 * th / (1.0 - th)
        b_s[:, sl] = jnp.sqrt(one_minus_a2) * (ig * xb)

    def row(i, h):
        h = a_s[pl.ds(i, 1), :] * h + b_s[pl.ds(i, 1), :]
        b_s[pl.ds(i, 1), :] = h
        return h

    hcar[...] = lax.fori_loop(0, t, row, hcar[...])
    gate = pg_ref[...]
    gelu = 0.5 * gate * (1.0 + jnp.tanh(GELU_C * (gate + 0.044715 * (gate * gate * gate))))
    o_ref[...] = (b_s[...] * gelu).astype(o_ref.dtype)


def _lru(px, pg, conv_w, conv_b, w_a, b_a, w_x, b_x, lam, batch, seq):
    m, w = px.shape
    t = min(seq, 256)
    nc = seq // t
    blk = pl.BlockSpec((t, w), lambda b, c: (b * nc + c, 0))
    vec = pl.BlockSpec((1, w), lambda b, c: (0, 0))
    wblk = pl.BlockSpec(w_a.shape, lambda b, c: (0, 0, 0))
    return pl.pallas_call(
        _lru_kernel,
        grid=(batch, nc),
        in_specs=[blk, blk, pl.BlockSpec((D_CONV, w), lambda b, c: (0, 0)), vec, wblk, vec, wblk, vec, vec],
        out_specs=blk,
        out_shape=jax.ShapeDtypeStruct((m, w), bf16),
        scratch_shapes=[pltpu.VMEM((SUBLANES, w), f32), pltpu.VMEM((1, w), f32),
                        pltpu.VMEM((t, w), f32), pltpu.VMEM((t, w), f32)],
        compiler_params=_cparams(("parallel", "arbitrary"), 16 * t * w * 4),
        name="lru",
    )(px, pg, conv_w, conv_b[None, :], w_a, b_a[None, :], w_x, b_x[None, :], lam[None, :])


def _mixer_rwkv_lru(h_hi, w_in, w_out, mu, w0, w_up, a0, a_up, g_up, k_k, k_a, r_k, ln_g, ln_b,
                    conv_w, conv_b, w_a, b_a, w_x, b_x, lam, batch, seq):
    m, d = h_hi.shape
    gw = d // 2
    c_cols = 3 * gw + C_DECAY_RANK + C_ICLR_RANK + C_GATE_RANK
    tm = min(m, 1024)
    tn = 512
    proj = functools.partial(_mm, [h_hi], w_in, tm=tm, tn=tn, out_dtype=f32)
    pc = proj(col_off=0, n=c_cols)
    pg = proj(col_off=c_cols, n=gw)
    px = proj(col_off=c_cols + gw, n=gw)
    r, w, k2, v, kk, kka, g, bonus = _rwkv_prep(pc, mu, w0, w_up, a0, a_up, g_up, k_k, k_a, r_k, seq)
    y = _rwkv_scan(w, kk, kka, k2, r, v, batch, seq)
    c_out = _rwkv_post(y, g, bonus, ln_g, ln_b)
    d_out = _lru(px, pg, conv_w, conv_b, w_a, b_a, w_x, b_x, lam, batch, seq)
    return _mm([c_out, d_out], w_out, col_off=0, n=d, tm=tm, tn=tn, out_dtype=f32)


ADA_ROWS = 16


def _ada_kernel(c_ref, w_ref, b_ref, o_ref):
    c = c_ref[...]
    s = (c * _sigmoid(c)).astype(bf16)
    o_ref[...] = jnp.dot(s, w_ref[...].astype(bf16), preferred_element_type=f32) + b_ref[...]


def _ada(c, ada_w, ada_b):
    batch, d = c.shape
    n = ada_w.shape[1]
    tn = 512
    cp = jnp.zeros((ADA_ROWS, d), f32).at[:batch].set(c)
    out = pl.pallas_call(
        _ada_kernel,
        grid=(n // tn,),
        in_specs=[pl.BlockSpec((ADA_ROWS, d), lambda j: (0, 0)),
                  pl.BlockSpec((d, tn), lambda j: (0, j)),
                  pl.BlockSpec((1, tn), lambda j: (0, j))],
        out_specs=pl.BlockSpec((ADA_ROWS, tn), lambda j: (0, j)),
        out_shape=jax.ShapeDtypeStruct((ADA_ROWS, n), f32),
        compiler_params=_cparams(("parallel",), 3 * d * tn * 4),
        name="ada",
    )(cp, ada_w, ada_b[None, :])
    return out[:batch]


def kernel(x, c, positions, ada_w, ada_b, ada_table, ln_g, ln_b, ab_w_in, ab_w_out, ret_gn_g, cd_w_in, cd_w_out, rwkv_mu, rwkv_w0, rwkv_w_up, rwkv_a0, rwkv_a_up, rwkv_g_up, rwkv_k_k, rwkv_k_a, rwkv_r_k, rwkv_ln_g, rwkv_ln_b, lru_conv_w, lru_conv_b, lru_w_a, lru_b_a, lru_w_x, lru_b_x, lru_lambda, moe_w_grp, moe_b_grp, moe_w_exp, moe_b_exp, moe_w_gate, moe_w_up, moe_w_down):
    batch, seq, d = x.shape
    x2 = x.reshape(batch * seq, d)
    pos2 = positions.reshape(batch * seq, 1)
    ada = _ada(c, ada_w, ada_b).reshape(batch, 6, 1, d)
    for layer in range(DEPTH):
        mod = ada + ada_table[layer][None, :, None, :]
        shift_m, scale_m, gate_m, shift_f, scale_f, gate_f = (mod[:, i] for i in range(6))
        h_hi, h_lo = _modcast(x2, scale_m, shift_m, seq)
        j = layer // 2
        if layer % 2 == 0:
            y = _mixer_sparse_retention(h_hi, h_lo, pos2, ab_w_in[j], ab_w_out[j], ret_gn_g[j], batch, seq)
        else:
            y = _mixer_rwkv_lru(h_hi, cd_w_in[j], cd_w_out[j], rwkv_mu[j], rwkv_w0[j], rwkv_w_up[j],
                                rwkv_a0[j], rwkv_a_up[j], rwkv_g_up[j], rwkv_k_k[j], rwkv_k_a[j],
                                rwkv_r_k[j], rwkv_ln_g[j], rwkv_ln_b[j], lru_conv_w[j], lru_conv_b[j],
                                lru_w_a[j], lru_b_a[j], lru_w_x[j], lru_b_x[j], lru_lambda[j], batch, seq)
        x2 = _moe_layer(x2, y, gate_m, ln_g[layer, 0], ln_b[layer, 0], scale_f, shift_f, gate_f,
                        ln_g[layer, 1], ln_b[layer, 1], moe_w_grp[layer], moe_b_grp[layer],
                        moe_w_exp[layer], moe_b_exp[layer], moe_w_gate[layer], moe_w_up[layer],
                        moe_w_down[layer], seq)
    return x2.reshape(batch, seq, d)
```

```python
import functools

import jax
import jax.numpy as jnp
from jax import lax
from jax.experimental import pallas as pl
from jax.experimental.pallas import tpu as pltpu

f32 = jnp.float32
bf16 = jnp.bfloat16
i32 = jnp.int32

DEPTH = 2
CHUNK = 64
ROPE_THETA = 10000.0
LN_EPS = 1e-5
ALPHA = (2 * DEPTH) ** 0.25
HEAD_DIM = 128
IDX_HEADS = 16
IDX_DIM = 64
TOPK_MAX = 256
C_HEAD_DIM = 64
C_DECAY_RANK = 128
C_ICLR_RANK = 128
C_GATE_RANK = 256
C_EPS = 64e-5
D_BLOCKS = 16
D_CONV = 4
LRU_C = 8.0
N_GROUPS = 4
EXPERTS_PER_GROUP = 8
N_EXPERTS = N_GROUPS * EXPERTS_PER_GROUP
D_EXPERT = 512

LANES = 128
SUBLANES = 8
VMEM_BYTES_V7X = 64 * 1024 * 1024
VMEM_HEADROOM = 8 * 1024 * 1024


def _cparams(semantics, vmem_bytes):
    limit = min(int(vmem_bytes) + VMEM_HEADROOM, VMEM_BYTES_V7X - VMEM_HEADROOM)
    return pltpu.CompilerParams(dimension_semantics=semantics, vmem_limit_bytes=limit)


def _split_bf16(a):
    hi = a.astype(bf16)
    lo = (a - hi.astype(f32)).astype(bf16)
    return hi, lo


def _dot3(a, b, dims=(((1,), (0,)), ((), ()))):
    ah, al = _split_bf16(a)
    bh, bl = _split_bf16(b)
    dg = functools.partial(lax.dot_general, dimension_numbers=dims, preferred_element_type=f32)
    return dg(ah, bh) + dg(ah, bl) + dg(al, bh)


def _sigmoid(x):
    return 1.0 / (1.0 + jnp.exp(-x))


def _modcast_kernel(x_ref, sc_ref, sh_ref, hi_ref, lo_ref):
    h = x_ref[...] * (1.0 + sc_ref[0]) + sh_ref[0]
    hi, lo = _split_bf16(h)
    hi_ref[...] = hi
    lo_ref[...] = lo


def _modcast(x2, sc, sh, seq):
    m, k = x2.shape
    tm = 256
    row = pl.BlockSpec((tm, k), lambda i: (i, 0))
    per_batch = pl.BlockSpec((1, 1, k), lambda i: (i * tm // seq, 0, 0))
    return pl.pallas_call(
        _modcast_kernel,
        grid=(m // tm,),
        in_specs=[row, per_batch, per_batch],
        out_specs=[row, row],
        out_shape=[jax.ShapeDtypeStruct((m, k), bf16)] * 2,
        compiler_params=_cparams(("parallel",), 2 * tm * k * (4 + 2 + 2)),
        name="modcast",
    )(x2, sc, sh)


def _mm_kernel(*refs, nparts, epilogue):
    acc = None
    for p in range(nparts):
        d = jnp.dot(refs[p][...], refs[nparts + p][...].astype(bf16), preferred_element_type=f32)
        acc = d if acc is None else acc + d
    epilogue(acc, *refs[2 * nparts:])


def _mm3_kernel(xh_ref, xl_ref, w_ref, o_ref):
    wh, wl = _split_bf16(w_ref[...])
    xh = xh_ref[...]
    acc = jnp.dot(xh, wh, preferred_element_type=f32)
    acc += jnp.dot(xh, wl, preferred_element_type=f32)
    acc += jnp.dot(xl_ref[...], wh, preferred_element_type=f32)
    o_ref[...] = acc


def _store_epilogue(acc, o_ref):
    o_ref[...] = acc.astype(o_ref.dtype)


def _rope_epilogue(acc, c_ref, s_ref, o_ref, *, scale):
    c = c_ref[...]
    s = s_ref[...]
    for k in range(acc.shape[1] // LANES):
        blk = acc[:, k * LANES:(k + 1) * LANES]
        rot = blk * c + pltpu.roll(blk, LANES // 2, 1) * s
        if scale != 1.0:
            rot = rot * scale
        o_ref[:, k * LANES:(k + 1) * LANES] = rot.astype(o_ref.dtype)


def _mm(xs, w, *, col_off, n, tm, tn, out_dtype, epilogue=_store_epilogue, extra=(), extra_specs=()):
    m = xs[0].shape[0]
    kp = xs[0].shape[1]
    nparts = len(xs)
    assert all(x.shape == (m, kp) for x in xs) and w.shape[0] == nparts * kp
    assert m % tm == 0 and n % tn == 0 and col_off % tn == 0
    jo = col_off // tn
    x_specs = [pl.BlockSpec((tm, kp), lambda i, j: (i, 0)) for _ in xs]
    w_specs = [pl.BlockSpec((kp, tn), lambda i, j, p=p: (p, jo + j)) for p in range(nparts)]
    vmem = 2 * nparts * (tm * kp * 2 + kp * tn * 4) + nparts * kp * tn * 2 + 4 * tm * tn * 4
    return pl.pallas_call(
        functools.partial(_mm_kernel, nparts=nparts, epilogue=epilogue),
        grid=(m // tm, n // tn),
        in_specs=x_specs + w_specs + list(extra_specs),
        out_specs=pl.BlockSpec((tm, tn), lambda i, j: (i, j)),
        out_shape=jax.ShapeDtypeStruct((m, n), out_dtype),
        compiler_params=_cparams(("parallel", "arbitrary"), vmem),
        name="mm",
    )(*xs, *([w] * nparts), *extra)


def _mm3(xh, xl, w, *, col_off, n, tm, tn):
    m, k = xh.shape
    assert m % tm == 0 and n % tn == 0 and col_off % tn == 0
    jo = col_off // tn
    xspec = pl.BlockSpec((tm, k), lambda i, j: (i, 0))
    vmem = 2 * (2 * tm * k * 2 + k * tn * 4) + 2 * k * tn * 2 + 4 * tm * tn * 4
    return pl.pallas_call(
        _mm3_kernel,
        grid=(m // tm, n // tn),
        in_specs=[xspec, xspec, pl.BlockSpec((k, tn), lambda i, j: (0, jo + j))],
        out_specs=pl.BlockSpec((tm, tn), lambda i, j: (i, j)),
        out_shape=jax.ShapeDtypeStruct((m, n), f32),
        compiler_params=_cparams(("parallel", "arbitrary"), vmem),
        name="mm3",
    )(xh, xl, w)


def _res_ln(x, y, gate, g, b):
    z = ALPHA * x + (1.0 + gate) * y
    mu = jnp.mean(z, axis=-1, keepdims=True)
    zc = z - mu
    var = jnp.mean(zc * zc, axis=-1, keepdims=True)
    return zc * lax.rsqrt(var + LN_EPS) * g + b


def _ln_router_kernel(x_ref, y_ref, gate_ref, g_ref, b_ref, sc_ref, sh_ref, wr_ref, br_ref,
                      xo_ref, h_ref, rw_ref, rid_ref):
    xn = _res_ln(x_ref[...], y_ref[...], gate_ref[0], g_ref[...], b_ref[...])
    xo_ref[...] = xn
    h = xn * (1.0 + sc_ref[0]) + sh_ref[0]
    h_ref[...] = h
    logits = _dot3(h, wr_ref[...]) + br_ref[...]
    lane = lax.broadcasted_iota(i32, logits.shape, 1)
    neg = jnp.float32(-jnp.inf)
    is_grp = lane < N_GROUPS
    gl = jnp.where(is_grp, logits, neg)
    gmax = jnp.max(gl, axis=1, keepdims=True)
    gidx = jnp.min(jnp.where(gl == gmax, lane, LANES), axis=1, keepdims=True)
    gsum = jnp.sum(jnp.where(is_grp, jnp.exp(gl - gmax), 0.0), axis=1, keepdims=True)
    grp_p = 1.0 / gsum
    lo = N_GROUPS + EXPERTS_PER_GROUP * gidx
    in_grp = jnp.logical_and(lane >= lo, lane < lo + EXPERTS_PER_GROUP)
    el = jnp.where(in_grp, logits, neg)
    m1 = jnp.max(el, axis=1, keepdims=True)
    i1 = jnp.min(jnp.where(el == m1, lane, LANES), axis=1, keepdims=True)
    el2 = jnp.where(lane == i1, neg, el)
    m2 = jnp.max(el2, axis=1, keepdims=True)
    i2 = jnp.min(jnp.where(el2 == m2, lane, LANES), axis=1, keepdims=True)
    e2 = jnp.exp(m2 - m1)
    w1 = grp_p / (1.0 + e2)
    w2 = grp_p * e2 / (1.0 + e2)
    rw_ref[...] = jnp.where(lane == 0, w1, jnp.where(lane == 1, w2, 0.0))
    rid_ref[...] = jnp.where(lane == 0, i1 - N_GROUPS, jnp.where(lane == 1, i2 - N_GROUPS, 0))


def _ln_router(x2, y2, gate, g, b, sc, sh, wr, br, seq):
    m, d = x2.shape
    tm = 256
    row = pl.BlockSpec((tm, d), lambda i: (i, 0))
    per_batch = pl.BlockSpec((1, 1, d), lambda i: (i * tm // seq, 0, 0))
    vec = pl.BlockSpec((1, d), lambda i: (0, 0))
    small = pl.BlockSpec((tm, LANES), lambda i: (i, 0))
    return pl.pallas_call(
        _ln_router_kernel,
        grid=(m // tm,),
        in_specs=[row, row, per_batch, vec, vec, per_batch, per_batch,
                  pl.BlockSpec((d, LANES), lambda i: (0, 0)), pl.BlockSpec((1, LANES), lambda i: (0, 0))],
        out_specs=[row, row, small, small],
        out_shape=[jax.ShapeDtypeStruct((m, d), f32), jax.ShapeDtypeStruct((m, d), f32),
                   jax.ShapeDtypeStruct((m, LANES), f32), jax.ShapeDtypeStruct((m, LANES), i32)],
        compiler_params=_cparams(("parallel",), 12 * tm * d * 4 + 2 * d * LANES * 4),
        name="ln_router",
    )(x2, y2, gate, g, b, sc, sh, wr, br)


MOE_TM = 256
MOE_HALF = D_EXPERT // 2
DMA_UNROLL = 8


def _row_copy(src_hbm, row, dst_vmem, r, sem):
    return pltpu.make_async_copy(src_hbm.at[pl.ds(row, 1), :], dst_vmem.at[pl.ds(r, 1), :], sem)


def _gather_start(idx_ref, base, src_hbm, dst_vmem, sem, nrows, stride=1):
    def issue(r, c):
        _row_copy(src_hbm, idx_ref[base + r * stride], dst_vmem, r, sem).start()
        return c

    lax.fori_loop(0, nrows, issue, 0, unroll=DMA_UNROLL)


def _gather_wait(src_hbm, dst_vmem, sem, nrows):
    def wait(r, c):
        _row_copy(src_hbm, 0, dst_vmem, r, sem).wait()
        return c

    lax.fori_loop(0, nrows, wait, 0, unroll=DMA_UNROLL)


def _moe_ffn_kernel(te_ref, tv_ref, tok_ref, h_hbm, wg_ref, wu_ref, wd_ref, o_ref, xbuf, sems):
    i = pl.program_id(0)
    n = pl.num_programs(0)
    slot = i % 2
    valid = tv_ref[i] > 0

    def start(tile, s):
        _gather_start(tok_ref, tile * MOE_TM, h_hbm, xbuf.at[s], sems.at[s], MOE_TM)

    @pl.when(jnp.logical_and(i == 0, valid))
    def _first_gather():
        start(0, 0)

    nxt = jnp.minimum(i + 1, n - 1)

    @pl.when(jnp.logical_and(i + 1 < n, tv_ref[nxt] > 0))
    def _next_gather():
        start(i + 1, 1 - slot)

    @pl.when(valid)
    def _compute():
        _gather_wait(h_hbm, xbuf.at[slot], sems.at[slot], MOE_TM)
        xb = xbuf[slot].astype(bf16)
        y = None
        for hf in range(D_EXPERT // MOE_HALF):
            cs = slice(hf * MOE_HALF, (hf + 1) * MOE_HALF)
            gate = jnp.dot(xb, wg_ref[:, cs].astype(bf16), preferred_element_type=f32)
            up = jnp.dot(xb, wu_ref[:, cs].astype(bf16), preferred_element_type=f32)
            hid = gate * _sigmoid(gate) * up
            part = jnp.dot(hid.astype(bf16), wd_ref[cs, :].astype(bf16), preferred_element_type=f32)
            y = part if y is None else y + part
        o_ref[...] = y

    @pl.when(jnp.logical_not(valid))
    def _empty():
        o_ref[...] = jnp.zeros_like(o_ref)


def _moe_ffn(h2, w_gate, w_up, w_down, layer, tile_e, tile_valid, row_tok, n_tiles):
    n, d = h2.shape
    tm = MOE_TM
    once = pl.Buffered(1)
    grid_spec = pltpu.PrefetchScalarGridSpec(
        num_scalar_prefetch=3,
        grid=(n_tiles,),
        in_specs=[
            pl.BlockSpec(memory_space=pl.ANY),
            pl.BlockSpec((None, None, d, D_EXPERT), lambda i, te, tv, tok: (layer, te[i], 0, 0), pipeline_mode=once),
            pl.BlockSpec((None, None, d, D_EXPERT), lambda i, te, tv, tok: (layer, te[i], 0, 0), pipeline_mode=once),
            pl.BlockSpec((None, None, D_EXPERT, d), lambda i, te, tv, tok: (layer, te[i], 0, 0), pipeline_mode=once),
        ],
        out_specs=pl.BlockSpec((tm, d), lambda i, te, tv, tok: (i, 0)),
        scratch_shapes=[pltpu.VMEM((2, tm, d), f32), pltpu.SemaphoreType.DMA((2,))],
    )
    vmem = 3 * d * D_EXPERT * 4 + 3 * d * MOE_HALF * 2 + 2 * tm * d * 4 + tm * d * 2 + 4 * tm * d * 4
    return pl.pallas_call(
        _moe_ffn_kernel,
        grid_spec=grid_spec,
        out_shape=jax.ShapeDtypeStruct((n_tiles * tm, d), f32),
        compiler_params=_cparams(("arbitrary",), vmem),
        name="moe_ffn",
    )(tile_e, tile_valid, row_tok, h2, w_gate, w_up, w_down)


def _moe_combine_kernel(pos_ref, ys_hbm, rw_ref, x_ref, gate_ref, g_ref, b_ref, o_ref, buf, sems):
    i = pl.program_id(0)
    n = pl.num_programs(0)
    tm = x_ref.shape[0]
    slot = i % 2

    def start(tile, s):
        for k in range(2):
            _gather_start(pos_ref, 2 * tile * tm + k, ys_hbm, buf.at[s, k], sems.at[s, k], tm, stride=2)

    @pl.when(i == 0)
    def _first_gather():
        start(0, 0)

    @pl.when(i + 1 < n)
    def _next_gather():
        start(i + 1, 1 - slot)

    for k in range(2):
        _gather_wait(ys_hbm, buf.at[slot, k], sems.at[slot, k], tm)
    rw = rw_ref[...]
    y = rw[:, 0:1] * buf[slot, 0] + rw[:, 1:2] * buf[slot, 1]
    o_ref[...] = _res_ln(x_ref[...], y, gate_ref[0], g_ref[...], b_ref[...])


def _moe_combine(pos, ys, rw, x2, gate, g, b, seq):
    n, d = x2.shape
    tm = 256
    row = lambda i, pos: (i, 0)
    grid_spec = pltpu.PrefetchScalarGridSpec(
        num_scalar_prefetch=1,
        grid=(n // tm,),
        in_specs=[
            pl.BlockSpec(memory_space=pl.ANY),
            pl.BlockSpec((tm, LANES), row),
            pl.BlockSpec((tm, d), row),
            pl.BlockSpec((1, 1, d), lambda i, pos: (i * tm // seq, 0, 0)),
            pl.BlockSpec((1, d), lambda i, pos: (0, 0)),
            pl.BlockSpec((1, d), lambda i, pos: (0, 0)),
        ],
        out_specs=pl.BlockSpec((tm, d), row),
        scratch_shapes=[pltpu.VMEM((2, 2, tm, d), f32), pltpu.SemaphoreType.DMA((2, 2))],
    )
    return pl.pallas_call(
        _moe_combine_kernel,
        grid_spec=grid_spec,
        out_shape=jax.ShapeDtypeStruct((n, d), f32),
        compiler_params=_cparams(("arbitrary",), 12 * tm * d * 4),
        name="moe_combine",
    )(pos, ys, rw, x2, gate, g, b)


def _moe_plan(eid, n_tiles):
    tm = MOE_TM
    flat_e = eid.reshape(-1)
    onehot = (flat_e[:, None] == jnp.arange(N_EXPERTS, dtype=i32)[None, :]).astype(i32)
    csum = jnp.cumsum(onehot, axis=0)
    rank = jnp.sum((csum - onehot) * onehot, axis=1)
    counts = csum[-1]
    padded = ((counts + tm - 1) // tm) * tm
    ends = jnp.cumsum(padded)
    pos = ((ends - padded)[flat_e] + rank).astype(i32)
    tile_start = jnp.arange(n_tiles, dtype=i32) * tm
    tile_valid = (tile_start < ends[-1]).astype(i32)
    tile_e = jnp.searchsorted(ends, tile_start, side="right").astype(i32)
    last_valid = jnp.maximum(ends[-1] // tm - 1, 0)
    tile_e = jnp.where(tile_valid > 0, tile_e, tile_e[last_valid])
    row_tok = jnp.zeros((n_tiles * tm,), i32).at[pos].set(jnp.arange(flat_e.shape[0], dtype=i32) // 2)
    return pos, tile_e, tile_valid, row_tok


def _moe_layer(x2, y2, gate_m, ln_g, ln_b, scale_f, shift_f, gate_f, ln_g2, ln_b2,
               w_grp, b_grp, w_exp, b_exp, w_gate, w_up, w_down, layer, seq):
    n, d = x2.shape
    pad = LANES - N_GROUPS - N_EXPERTS
    wr = jnp.concatenate([w_grp, w_exp, jnp.zeros((d, pad), f32)], axis=1)
    br = jnp.concatenate([b_grp, b_exp, jnp.zeros((pad,), f32)])[None, :]
    x1, h, rw, rid = _ln_router(x2, y2, gate_m, ln_g[None, :], ln_b[None, :], scale_f, shift_f, wr, br, seq)
    n_tiles = (2 * n) // MOE_TM + N_EXPERTS
    pos, tile_e, tile_valid, row_tok = _moe_plan(rid[:, :2], n_tiles)
    ys = _moe_ffn(h, w_gate, w_up, w_down, layer, tile_e, tile_valid, row_tok, n_tiles)
    return _moe_combine(pos, ys, rw, x1, gate_f, ln_g2[None, :], ln_b2[None, :], seq)


def _rope_table_kernel(pos_ref, fr_ref, sg_ref, c_ref, s_ref):
    ang = pos_ref[...].astype(f32) * fr_ref[...]
    c_ref[...] = jnp.cos(ang)
    s_ref[...] = jnp.sin(ang) * sg_ref[...]


def _rope_tables(pos2, head_dim):
    m = pos2.shape[0]
    half = head_dim // 2
    freqs = ROPE_THETA ** (-jnp.arange(half, dtype=f32) / half)
    reps = LANES // head_dim
    fr = jnp.tile(jnp.concatenate([freqs, freqs]), reps)[None, :]
    sg = jnp.tile(jnp.concatenate([-jnp.ones((half,), f32), jnp.ones((half,), f32)]), reps)[None, :]
    tm = min(m, 1024)
    vec = pl.BlockSpec((1, LANES), lambda i: (0, 0))
    out = pl.BlockSpec((tm, LANES), lambda i: (i, 0))
    return pl.pallas_call(
        _rope_table_kernel,
        grid=(m // tm,),
        in_specs=[pl.BlockSpec((tm, 1), lambda i: (i, 0)), vec, vec],
        out_specs=[out, out],
        out_shape=[jax.ShapeDtypeStruct((m, LANES), f32)] * 2,
        compiler_params=_cparams(("parallel",), 8 * tm * LANES * 4),
        name="rope_tables",
    )(pos2, fr, sg)


def _proj_rope(h_hi, w, c_tab, s_tab, *, col_off, n, scale, tm, tn):
    tab = pl.BlockSpec((tm, LANES), lambda i, j: (i, 0))
    return _mm([h_hi], w, col_off=col_off, n=n, tm=tm, tn=tn, out_dtype=bf16,
               epilogue=functools.partial(_rope_epilogue, scale=scale),
               extra=(c_tab, s_tab), extra_specs=(tab, tab))


IDX_W = IDX_HEADS * IDX_DIM
IDX_RAW = IDX_W + LANES


IDX_K = 4 * IDX_DIM


def _idx_rope_kernel(x_ref, c_ref, s_ref, q_ref, k_ref, w_ref):
    c = c_ref[...]
    s = s_ref[...]
    lane = lax.broadcasted_iota(i32, c.shape, 1)
    first = (lane & (IDX_DIM // 2)) == 0
    low = lane < IDX_DIM

    def rope(blk):
        swapped = jnp.where(first, pltpu.roll(blk, LANES - IDX_DIM // 2, 1), pltpu.roll(blk, IDX_DIM // 2, 1))
        return blk * c + swapped * s

    def hi_lo(x):
        hi = x.astype(bf16).astype(f32)
        return hi, x - hi

    for k in range(IDX_W // LANES):
        hi, lo = hi_lo(rope(x_ref[:, k * LANES:(k + 1) * LANES]))
        hi_sw = pltpu.roll(hi, IDX_DIM, 1)
        lo_sw = pltpu.roll(lo, IDX_DIM, 1)
        base = 2 * k * IDX_K
        q_ref[:, base:base + LANES] = jnp.where(low, hi, hi_sw).astype(bf16)
        q_ref[:, base + LANES:base + IDX_K] = jnp.where(low, lo, 0.0).astype(bf16)
        q_ref[:, base + IDX_K:base + IDX_K + LANES] = jnp.where(low, hi_sw, hi).astype(bf16)
        q_ref[:, base + IDX_K + LANES:base + 2 * IDX_K] = jnp.where(low, lo_sw, 0.0).astype(bf16)
    kw = x_ref[:, IDX_W:IDX_RAW]
    hi, lo = hi_lo(rope(kw))
    k_ref[:, 0:LANES] = jnp.where(low, hi, pltpu.roll(lo, IDX_DIM, 1)).astype(bf16)
    k_ref[:, LANES:IDX_K] = jnp.where(low, hi, 0.0).astype(bf16)
    w_ref[...] = kw


def _idx_rope(raw, c_tab, s_tab):
    m = raw.shape[0]
    tm = min(m, 512)
    tab = pl.BlockSpec((tm, LANES), lambda i: (i, 0))
    return pl.pallas_call(
        _idx_rope_kernel,
        grid=(m // tm,),
        in_specs=[pl.BlockSpec((tm, IDX_RAW), lambda i: (i, 0)), tab, tab],
        out_specs=[pl.BlockSpec((tm, IDX_HEADS * IDX_K), lambda i: (i, 0)),
                   pl.BlockSpec((tm, IDX_K), lambda i: (i, 0)), tab],
        out_shape=[jax.ShapeDtypeStruct((m, IDX_HEADS * IDX_K), bf16), jax.ShapeDtypeStruct((m, IDX_K), bf16),
                   jax.ShapeDtypeStruct((m, LANES), f32)],
        compiler_params=_cparams(("parallel",), 8 * tm * IDX_RAW * 4 + 4 * tm * IDX_HEADS * IDX_K * 2),
        name="idx_rope",
    )(raw, c_tab, s_tab)


_NT = (((1,), (1,)), ((), ()))
INT_MIN = -2 ** 31


DSA_T = 256


def _dsa_index_kernel(q_ref, k_ref, qw_ref, m_ref, key_scr, *, topk, scale):
    i = pl.program_id(1)
    nkb, tq, kb = key_scr.shape
    qw = qw_ref[...]
    q_chunk = (i * tq + lax.broadcasted_iota(i32, (tq, 1), 0)) // CHUNK
    col = lax.broadcasted_iota(i32, (tq, kb), 1)

    def score_block(j, c):
        kblk = k_ref[pl.ds(pl.multiple_of(j * kb, kb), kb), :]
        score = jnp.zeros((tq, kb), f32)
        for hd in range(IDX_HEADS):
            rel = lax.dot_general(q_ref[:, hd * IDX_K:(hd + 1) * IDX_K], kblk, _NT, preferred_element_type=f32)
            score = score + jnp.maximum(rel, 0.0) * qw[:, IDX_DIM + hd:IDX_DIM + hd + 1]
        score = score * scale
        adm = j * kb + col < (q_chunk + 1) * CHUNK
        bits = pltpu.bitcast(score, i32)
        key = jnp.where(bits < 0, bits ^ 0x7FFFFFFF, bits)
        key_scr[j] = jnp.where(adm, key, INT_MIN)
        return c

    lax.fori_loop(0, i + 1, score_block, 0)

    def count_ge(cand):
        def blk(j, acc):
            ge = (key_scr[j] >= cand).astype(f32)
            return acc + ge[:, :LANES] + ge[:, LANES:]

        acc = lax.fori_loop(0, i + 1, blk, jnp.zeros((tq, LANES), f32))
        return jnp.sum(acc, axis=1, keepdims=True)

    kf = jnp.float32(topk)
    cur = jnp.where(count_ge(jnp.zeros((tq, 1), i32)) >= kf, 0, INT_MIN).astype(i32)

    def bisect(it, cur):
        cand = cur + jnp.left_shift(jnp.int32(1), 30 - it)
        return jnp.where(count_ge(cand) >= kf, cand, cur)

    thr = jnp.maximum(lax.fori_loop(0, 31, bisect, cur), INT_MIN + 1)

    def write(j, c):
        m_ref[0, j] = (key_scr[j] >= thr).astype(bf16)
        return c

    def clear(j, c):
        m_ref[0, j] = jnp.zeros((tq, kb), bf16)
        return c

    lax.fori_loop(0, i + 1, write, 0)
    lax.fori_loop(i + 1, nkb, clear, 0)


def _dsa_index(iq, ik, iw, batch, seq):
    tq = min(seq, DSA_T)
    nq = seq // tq
    topk = min(TOPK_MAX, seq // 4)
    scale = IDX_DIM ** -0.5 * IDX_HEADS ** -0.5
    return pl.pallas_call(
        functools.partial(_dsa_index_kernel, topk=topk, scale=scale),
        grid=(batch, nq),
        in_specs=[pl.BlockSpec((tq, IDX_HEADS * IDX_K), lambda b, i: (b * nq + i, 0)),
                  pl.BlockSpec((seq, IDX_K), lambda b, i: (b, 0)),
                  pl.BlockSpec((tq, LANES), lambda b, i: (b * nq + i, 0))],
        out_specs=pl.BlockSpec((1, nq, tq, tq), lambda b, i: (b * nq + i, 0, 0, 0)),
        out_shape=jax.ShapeDtypeStruct((batch * nq, nq, tq, tq), bf16),
        scratch_shapes=[pltpu.VMEM((nq, tq, tq), i32)],
        compiler_params=_cparams(("parallel", "parallel"),
                                 6 * tq * seq * 4 + 4 * seq * IDX_K * 2 + 4 * tq * IDX_HEADS * IDX_K * 2),
        name="dsa_index",
    )(iq, ik, iw)


ATT_HG = 4
MASKED = -1e30


def _dsa_attn_kernel(q_ref, k_ref, v_ref, m_ref, o_ref):
    i = pl.program_id(2)
    _, _, tq, kb = m_ref.shape
    for h in range(ATT_HG):
        sl = slice(h * HEAD_DIM, (h + 1) * HEAD_DIM)
        q = q_ref[:, sl]

        def block(j, carry, q=q, sl=sl):
            mx, den, acc = carry
            rows = pl.ds(pl.multiple_of(j * kb, kb), kb)
            logits = lax.dot_general(q, k_ref[rows, sl], _NT, preferred_element_type=f32)
            logits = jnp.where(m_ref[0, j] > 0, logits, MASKED)
            mx_new = jnp.maximum(mx, jnp.max(logits, axis=1, keepdims=True))
            alpha = jnp.exp(mx - mx_new)
            p = jnp.exp(logits - mx_new)
            den = alpha * den + jnp.sum(p, axis=1, keepdims=True)
            acc = alpha * acc + jnp.dot(p.astype(bf16), v_ref[rows, sl], preferred_element_type=f32)
            return mx_new, den, acc

        init = (jnp.full((tq, 1), MASKED, f32), jnp.zeros((tq, 1), f32), jnp.zeros((tq, HEAD_DIM), f32))
        _, den, acc = lax.fori_loop(0, i + 1, block, init)
        o_ref[:, sl] = (acc / den).astype(o_ref.dtype)


def _dsa_attn(q, k, v, mask, batch, seq):
    m, w = q.shape
    _, nq, tq, _ = mask.shape
    gw = ATT_HG * HEAD_DIM
    qspec = pl.BlockSpec((tq, gw), lambda b, g, i: (b * nq + i, g))
    kvspec = pl.BlockSpec((seq, gw), lambda b, g, i: (b, g))
    vmem = 2 * (2 * tq * gw * 2 + 2 * seq * gw * 2 + tq * seq * 2) + 8 * tq * tq * 4
    return pl.pallas_call(
        _dsa_attn_kernel,
        grid=(batch, w // gw, nq),
        in_specs=[qspec, kvspec, kvspec, pl.BlockSpec((1, nq, tq, tq), lambda b, g, i: (b * nq + i, 0, 0, 0))],
        out_specs=qspec,
        out_shape=jax.ShapeDtypeStruct((m, w), bf16),
        compiler_params=_cparams(("parallel", "parallel", "arbitrary"), vmem),
        name="dsa_attn",
    )(q, k, v, mask)


RET_HG = 4


def _retention_kernel(q_ref, k_ref, v_ref, g_ref, gn_ref, lg_ref, o_ref, state):
    c = pl.program_id(2)
    t = q_ref.shape[0]

    @pl.when(c == 0)
    def _init():
        state[...] = jnp.zeros_like(state)

    ri = lax.broadcasted_iota(i32, (t, t), 0)
    ci = lax.broadcasted_iota(i32, (t, t), 1)
    diff = (ri - ci).astype(f32)
    pos = lax.broadcasted_iota(i32, (t, 1), 0).astype(f32)
    for h in range(RET_HG):
        sl = slice(h * HEAD_DIM, (h + 1) * HEAD_DIM)
        lg = lg_ref[:, h * HEAD_DIM:h * HEAD_DIM + 1]
        q = q_ref[:, sl]
        k = k_ref[:, sl]
        v = v_ref[:, sl]
        decay = jnp.where(diff >= 0, jnp.exp(lg * jnp.maximum(diff, 0.0)), 0.0)
        scores = lax.dot_general(q, k, _NT, preferred_element_type=f32) * decay
        inner = jnp.dot(scores.astype(bf16), v, preferred_element_type=f32)
        st = state[h]
        cross = jnp.dot(q, st.astype(bf16), preferred_element_type=f32) * jnp.exp(lg * (pos + 1.0))
        kz = k.astype(f32) * jnp.exp(lg * (t - 1.0 - pos))
        kv = jnp.dot(kz.T.astype(bf16), v, preferred_element_type=f32)
        state[h] = jnp.exp(lg * t) * st + kv
        ret = inner + cross
        mu = jnp.mean(ret, axis=1, keepdims=True)
        rc = ret - mu
        var = jnp.mean(rc * rc, axis=1, keepdims=True)
        gate = g_ref[:, sl]
        out = rc * lax.rsqrt(var + LN_EPS) * gn_ref[:, sl] * (gate * _sigmoid(gate))
        o_ref[:, sl] = out.astype(o_ref.dtype)


def _retention(q, k, v, g, gn_g, batch, seq):
    m, w = q.shape
    heads = w // HEAD_DIM
    t = min(seq, 256)
    nc = seq // t
    gw = RET_HG * HEAD_DIM
    log_g = jnp.log1p(-jnp.exp2(-5.0 - jnp.arange(heads, dtype=f32)))
    lg = jnp.repeat(log_g, HEAD_DIM)[None, :]
    blk = pl.BlockSpec((t, gw), lambda b, gi, c: (b * nc + c, gi))
    vec = pl.BlockSpec((1, gw), lambda b, gi, c: (0, gi))
    return pl.pallas_call(
        _retention_kernel,
        grid=(batch, w // gw, nc),
        in_specs=[blk, blk, blk, blk, vec, vec],
        out_specs=blk,
        out_shape=jax.ShapeDtypeStruct((m, w), bf16),
        scratch_shapes=[pltpu.VMEM((RET_HG, HEAD_DIM, HEAD_DIM), f32)],
        compiler_params=_cparams(("parallel", "parallel", "arbitrary"), 16 * t * gw * 4 + 8 * t * t * 4),
        name="retention",
    )(q, k, v, g, gn_g[None, :], lg)


def _mixer_sparse_retention(h_hi, h_lo, pos2, w_in, w_out, gn_g, batch, seq):
    m, d = h_hi.shape
    gw = d // 2
    tm = min(m, 1024)
    tn = 512
    c128, s128 = _rope_tables(pos2, HEAD_DIM)
    c64, s64 = _rope_tables(pos2, IDX_DIM)
    rope = functools.partial(_proj_rope, h_hi, c_tab=c128, s_tab=s128, n=gw, tm=tm, tn=tn)
    plain = functools.partial(_mm, [h_hi], n=gw, tm=tm, tn=tn)
    aq = rope(w_in, col_off=0, scale=HEAD_DIM ** -0.5)
    ak = rope(w_in, col_off=gw, scale=1.0)
    av = plain(w_in, col_off=2 * gw, out_dtype=bf16)
    idx_raw = _mm3(h_hi, h_lo, w_in, col_off=3 * gw, n=IDX_RAW, tm=min(m, 512), tn=IDX_RAW // 3)
    iq, ik, iw = _idx_rope(idx_raw, c64, s64)
    mask = _dsa_index(iq, ik, iw, batch, seq)
    a_out = _dsa_attn(aq, ak, av, mask, batch, seq)
    w_b = w_in[:, 3 * gw + IDX_W + IDX_DIM + IDX_HEADS:]
    bq = rope(w_b, col_off=0, scale=1.0)
    bk = rope(w_b, col_off=gw, scale=HEAD_DIM ** -0.5)
    bv = plain(w_b, col_off=2 * gw, out_dtype=bf16)
    bg = plain(w_b, col_off=3 * gw, out_dtype=f32)
    b_out = _retention(bq, bk, bv, bg, gn_g, batch, seq)
    return _mm([a_out, b_out], w_out, col_off=0, n=d, tm=tm, tn=tn, out_dtype=f32)


MXU_N = 256


def _seg_sum64(x):
    r = lax.broadcasted_iota(i32, (MXU_N, MXU_N), 0) // C_HEAD_DIM
    c = lax.broadcasted_iota(i32, (MXU_N, MXU_N), 1) // C_HEAD_DIM
    ones = (r == c).astype(bf16)
    hi = x.astype(bf16)
    r1 = x - hi.astype(f32)
    mid = r1.astype(bf16)
    lo = (r1 - mid.astype(f32)).astype(bf16)
    cols = []
    for k in range(x.shape[1] // MXU_N):
        sl = slice(k * MXU_N, (k + 1) * MXU_N)
        s = jnp.dot(hi[:, sl], ones, preferred_element_type=f32)
        s += jnp.dot(mid[:, sl], ones, preferred_element_type=f32)
        s += jnp.dot(lo[:, sl], ones, preferred_element_type=f32)
        cols.append(s)
    return jnp.concatenate(cols, axis=1) if len(cols) > 1 else cols[0]


def _neg_softplus_neg(z):
    return jnp.minimum(z, 0.0) - jnp.log(1.0 + jnp.exp(-jnp.abs(z)))


def _rwkv_prep_kernel(p_ref, pp_ref, mu_ref, w0_ref, wup_ref, a0_ref, aup_ref, gup_ref, ka_ref, rk_ref,
                      r_o, w_o, k_o, a_o, v_o, g_o, bon_o, *, seq, gw):
    i = pl.program_id(0)
    tm = p_ref.shape[0]
    p = p_ref[...]
    prev_row = jnp.where((i * tm) % seq == 0, 0.0, pp_ref[SUBLANES - 1:SUBLANES, :])
    row = lax.broadcasted_iota(i32, (tm, 1), 0)
    shifted = jnp.where(row == 0, prev_row, pltpu.roll(p, 1, 0))
    pm = p + (shifted - p) * mu_ref[...]
    r = pm[:, 0:gw]
    k = pm[:, gw:2 * gw]
    v = pm[:, 2 * gw:3 * gw]
    o = 3 * gw
    dw = pm[:, o:o + C_DECAY_RANK]
    da = pm[:, o + C_DECAY_RANK:o + C_DECAY_RANK + C_ICLR_RANK]
    dg = pm[:, o + C_DECAY_RANK + C_ICLR_RANK:]
    w_log = _neg_softplus_neg(w0_ref[...] + _dot3(jnp.tanh(dw), wup_ref[...])) - 0.5
    decay = jnp.exp(-jnp.exp(w_log))
    a = _sigmoid(a0_ref[...] + _dot3(da, aup_ref[...]))
    g = _dot3(_sigmoid(dg), gup_ref[...])
    k2 = k * (1.0 + (a - 1.0) * ka_ref[...])
    r_o[...] = r
    w_o[...] = decay
    k_o[...] = k
    a_o[...] = a
    v_o[...] = v
    g_o[...] = g
    bon_o[...] = _seg_sum64(r * k2 * rk_ref[...]) * v


def _rwkv_prep(pc, mu, w0, w_up, a0, a_up, g_up, k_a, r_k, seq):
    m, cc = pc.shape
    gw = w0.shape[0]
    tm = 128
    nsub = tm // SUBLANES
    vec = pl.BlockSpec((1, gw), lambda i: (0, 0))
    out = pl.BlockSpec((tm, gw), lambda i: (i, 0))
    full = lambda a: pl.BlockSpec(a.shape, lambda i: (0, 0))
    return pl.pallas_call(
        functools.partial(_rwkv_prep_kernel, seq=seq, gw=gw),
        grid=(m // tm,),
        in_specs=[pl.BlockSpec((tm, cc), lambda i: (i, 0)),
                  pl.BlockSpec((SUBLANES, cc), lambda i: (jnp.maximum(i * nsub - 1, 0), 0)),
                  pl.BlockSpec((1, cc), lambda i: (0, 0)),
                  vec, full(w_up), vec, full(a_up), full(g_up), vec, vec],
        out_specs=[out] * 7,
        out_shape=[jax.ShapeDtypeStruct((m, gw), f32)] * 7,
        compiler_params=_cparams(("parallel",), 2 * tm * cc * 4 + 2 * 7 * tm * gw * 4 + 16 * tm * gw * 4),
        name="rwkv_prep",
    )(pc, pc, mu[None, :], w0[None, :], w_up, a0[None, :], a_up, g_up, k_a[None, :], r_k.reshape(1, -1))


def _rwkv_scan_kernel(w_ref, a_ref, k_ref, r_ref, v_ref, kkw_ref, kaw_ref, y_ref, z_ref):
    c = pl.program_id(0)
    steps = w_ref.shape[0]
    nslab = z_ref.shape[0]

    @pl.when(c == 0)
    def _init():
        z_ref[...] = jnp.zeros_like(z_ref)

    def both_halves(x):
        return jnp.concatenate([x, x], axis=1)

    kkw = both_halves(kkw_ref[...])
    kaw = both_halves(kaw_ref[...])

    def step(t, carry):
        w = both_halves(w_ref[t])
        a = both_halves(a_ref[t])
        kraw = both_halves(k_ref[t])
        r = both_halves(r_ref[t])
        kk = kraw * kkw
        kk = kk * lax.rsqrt(jnp.maximum(jnp.sum(kk * kk, axis=0, keepdims=True), 1e-24))
        ka = kk * a
        k = kraw * (1.0 + (a - 1.0) * kaw)
        vrows = v_ref[t]
        for s in range(nslab):
            z = z_ref[s]
            sk = jnp.sum(z * kk, axis=0, keepdims=True)
            zn = z * w - ka * sk + k * vrows[s:s + 1, :]
            z_ref[s] = zn
            y_ref[t, s:s + 1, :] = jnp.sum(zn * r, axis=0, keepdims=True)
        return carry

    lax.fori_loop(0, steps, step, 0)


def _to_scan_cols(a, batch, seq):
    heads = a.shape[1] // C_HEAD_DIM
    return a.reshape(batch, seq, heads, C_HEAD_DIM).transpose(1, 3, 0, 2).reshape(seq, C_HEAD_DIM, batch * heads)


def _rwkv_scan(w, a, k, r, v, k_k, k_a, batch, seq):
    heads = w.shape[1] // C_HEAD_DIM
    half = batch * heads
    assert 2 * half == LANES
    nslab = C_HEAD_DIM // 2
    cols = [_to_scan_cols(x, batch, seq) for x in (w, a, k, r)]
    consts = [jnp.tile(p.reshape(heads, C_HEAD_DIM).T, (1, batch)) for p in (k_k, k_a)]
    vr = v.reshape(batch, seq, heads, nslab, 2).transpose(1, 3, 4, 0, 2).reshape(seq, nslab, LANES)
    steps = min(seq, 64)
    col_spec = pl.BlockSpec((steps, C_HEAD_DIM, half), lambda c: (c, 0, 0))
    row_spec = pl.BlockSpec((steps, nslab, LANES), lambda c: (c, 0, 0))
    const_spec = pl.BlockSpec((C_HEAD_DIM, half), lambda c: (0, 0))
    y = pl.pallas_call(
        _rwkv_scan_kernel,
        grid=(seq // steps,),
        in_specs=[col_spec] * 4 + [row_spec, const_spec, const_spec],
        out_specs=row_spec,
        out_shape=jax.ShapeDtypeStruct((seq, nslab, LANES), f32),
        scratch_shapes=[pltpu.VMEM((nslab, C_HEAD_DIM, LANES), f32)],
        compiler_params=_cparams(("arbitrary",), 2 * 4 * steps * C_HEAD_DIM * LANES * 4 + 6 * steps * nslab * LANES * 4),
        name="rwkv_scan",
    )(*cols, vr, *consts)
    return y.reshape(seq, nslab, 2, batch, heads).transpose(3, 0, 4, 1, 2).reshape(batch * seq, heads * C_HEAD_DIM)


def _rwkv_post_kernel(y_ref, g_ref, bon_ref, lng_ref, lnb_ref, o_ref):
    y = y_ref[...]
    mu = _seg_sum64(y) * (1.0 / C_HEAD_DIM)
    yc = y - mu
    var = _seg_sum64(yc * yc) * (1.0 / C_HEAD_DIM)
    yn = yc * lax.rsqrt(var + C_EPS) * lng_ref[...] + lnb_ref[...]
    o_ref[...] = ((yn + bon_ref[...]) * g_ref[...]).astype(o_ref.dtype)


def _rwkv_post(y, g, bonus, ln_g, ln_b):
    m, gw = y.shape
    tm = 256
    blk = pl.BlockSpec((tm, gw), lambda i: (i, 0))
    vec = pl.BlockSpec((1, gw), lambda i: (0, 0))
    return pl.pallas_call(
        _rwkv_post_kernel,
        grid=(m // tm,),
        in_specs=[blk, blk, blk, vec, vec],
        out_specs=blk,
        out_shape=jax.ShapeDtypeStruct((m, gw), bf16),
        compiler_params=_cparams(("parallel",), 16 * tm * gw * 4),
        name="rwkv_post",
    )(y, g, bonus, ln_g[None, :], ln_b[None, :])


GELU_C = 0.7978845608028654


def _lru_kernel(px_ref, pg_ref, cw_ref, cb_ref, wa_ref, ba_ref, wx_ref, bx_ref, lam_ref, o_ref,
                tail, hcar, a_s, b_s):
    c = pl.program_id(1)
    t = px_ref.shape[0]

    @pl.when(c == 0)
    def _init():
        tail[...] = jnp.zeros_like(tail)
        hcar[...] = jnp.zeros_like(hcar)

    x = px_ref[...]
    ext = jnp.concatenate([tail[...], x], axis=0)
    xc = cb_ref[...]
    for j in range(D_CONV):
        off = SUBLANES - (D_CONV - 1) + j
        xc = xc + cw_ref[j:j + 1, :] * ext[off:off + t, :]
    tail[...] = x[t - SUBLANES:, :]
    lam = lam_ref[...]
    sp = jnp.maximum(-lam, 0.0) + jnp.log(1.0 + jnp.exp(-jnp.abs(lam)))
    bw = wa_ref.shape[1]
    for n in range(wa_ref.shape[0]):
        sl = slice(n * bw, (n + 1) * bw)
        xb = xc[:, sl]
        rg = _sigmoid(_dot3(xb, wa_ref[n]) + ba_ref[:, sl])
        ig = _sigmoid(_dot3(xb, wx_ref[n]) + bx_ref[:, sl])
        log_a = -LRU_C * rg * sp[:, sl]
        a_s[:, sl] = jnp.exp(log_a)
        th = jnp.tanh(log_a)
        one_minus_a2 = -2.0 * th / (1.0 - th)
        b_s[:, sl] = jnp.sqrt(one_minus_a2) * (ig * xb)

    def row(i, h):
        h = a_s[pl.ds(i, 1), :] * h + b_s[pl.ds(i, 1), :]
        b_s[pl.ds(i, 1), :] = h
        return h

    hcar[...] = lax.fori_loop(0, t, row, hcar[...])
    gate = pg_ref[...]
    gelu = 0.5 * gate * (1.0 + jnp.tanh(GELU_C * (gate + 0.044715 * (gate * gate * gate))))
    o_ref[...] = (b_s[...] * gelu).astype(o_ref.dtype)


def _lru(px, pg, conv_w, conv_b, w_a, b_a, w_x, b_x, lam, batch, seq):
    m, w = px.shape
    t = min(seq, 256)
    nc = seq // t
    blk = pl.BlockSpec((t, w), lambda b, c: (b * nc + c, 0))
    vec = pl.BlockSpec((1, w), lambda b, c: (0, 0))
    wblk = pl.BlockSpec(w_a.shape, lambda b, c: (0, 0, 0))
    return pl.pallas_call(
        _lru_kernel,
        grid=(batch, nc),
        in_specs=[blk, blk, pl.BlockSpec((D_CONV, w), lambda b, c: (0, 0)), vec, wblk, vec, wblk, vec, vec],
        out_specs=blk,
        out_shape=jax.ShapeDtypeStruct((m, w), bf16),
        scratch_shapes=[pltpu.VMEM((SUBLANES, w), f32), pltpu.VMEM((1, w), f32),
                        pltpu.VMEM((t, w), f32), pltpu.VMEM((t, w), f32)],
        compiler_params=_cparams(("parallel", "arbitrary"), 16 * t * w * 4),
        name="lru",
    )(px, pg, conv_w, conv_b[None, :], w_a, b_a[None, :], w_x, b_x[None, :], lam[None, :])


def _mixer_rwkv_lru(h_hi, w_in, w_out, mu, w0, w_up, a0, a_up, g_up, k_k, k_a, r_k, ln_g, ln_b,
                    conv_w, conv_b, w_a, b_a, w_x, b_x, lam, batch, seq):
    m, d = h_hi.shape
    gw = d // 2
    c_cols = 3 * gw + C_DECAY_RANK + C_ICLR_RANK + C_GATE_RANK
    tm = min(m, 1024)
    tn = 512
    proj = functools.partial(_mm, [h_hi], w_in, tm=tm, tn=tn, out_dtype=f32)
    pc = proj(col_off=0, n=c_cols)
    pg = proj(col_off=c_cols, n=gw)
    px = proj(col_off=c_cols + gw, n=gw)
    r, w, k, a, v, g, bonus = _rwkv_prep(pc, mu, w0, w_up, a0, a_up, g_up, k_a, r_k, seq)
    y = _rwkv_scan(w, a, k, r, v, k_k, k_a, batch, seq)
    c_out = _rwkv_post(y, g, bonus, ln_g, ln_b)
    d_out = _lru(px, pg, conv_w, conv_b, w_a, b_a, w_x, b_x, lam, batch, seq)
    return _mm([c_out, d_out], w_out, col_off=0, n=d, tm=tm, tn=tn, out_dtype=f32)


ADA_ROWS = 16


def _ada_kernel(c_ref, w_ref, b_ref, o_ref):
    c = c_ref[...]
    s = (c * _sigmoid(c)).astype(bf16)
    o_ref[...] = jnp.dot(s, w_ref[...].astype(bf16), preferred_element_type=f32) + b_ref[...]


def _ada(c, ada_w, ada_b):
    batch, d = c.shape
    n = ada_w.shape[1]
    tn = 512
    cp = jnp.zeros((ADA_ROWS, d), f32).at[:batch].set(c)
    out = pl.pallas_call(
        _ada_kernel,
        grid=(n // tn,),
        in_specs=[pl.BlockSpec((ADA_ROWS, d), lambda j: (0, 0)),
                  pl.BlockSpec((d, tn), lambda j: (0, j)),
                  pl.BlockSpec((1, tn), lambda j: (0, j))],
        out_specs=pl.BlockSpec((ADA_ROWS, tn), lambda j: (0, j)),
        out_shape=jax.ShapeDtypeStruct((ADA_ROWS, n), f32),
        compiler_params=_cparams(("parallel",), 3 * d * tn * 4),
        name="ada",
    )(cp, ada_w, ada_b[None, :])
    return out[:batch]


def kernel(x, c, positions, ada_w, ada_b, ada_table, ln_g, ln_b, ab_w_in, ab_w_out, ret_gn_g, cd_w_in, cd_w_out, rwkv_mu, rwkv_w0, rwkv_w_up, rwkv_a0, rwkv_a_up, rwkv_g_up, rwkv_k_k, rwkv_k_a, rwkv_r_k, rwkv_ln_g, rwkv_ln_b, lru_conv_w, lru_conv_b, lru_w_a, lru_b_a, lru_w_x, lru_b_x, lru_lambda, moe_w_grp, moe_b_grp, moe_w_exp, moe_b_exp, moe_w_gate, moe_w_up, moe_w_down):
    batch, seq, d = x.shape
    x2 = x.reshape(batch * seq, d)
    pos2 = positions.reshape(batch * seq, 1)
    ada = _ada(c, ada_w, ada_b).reshape(batch, 6, 1, d)
    for layer in range(DEPTH):
        mod = ada + ada_table[layer][None, :, None, :]
        shift_m, scale_m, gate_m, shift_f, scale_f, gate_f = (mod[:, i] for i in range(6))
        h_hi, h_lo = _modcast(x2, scale_m, shift_m, seq)
        j = layer // 2
        if layer % 2 == 0:
            y = _mixer_sparse_retention(h_hi, h_lo, pos2, ab_w_in[j], ab_w_out[j], ret_gn_g[j], batch, seq)
        else:
            y = _mixer_rwkv_lru(h_hi, cd_w_in[j], cd_w_out[j], rwkv_mu[j], rwkv_w0[j], rwkv_w_up[j],
                                rwkv_a0[j], rwkv_a_up[j], rwkv_g_up[j], rwkv_k_k[j], rwkv_k_a[j],
                                rwkv_r_k[j], rwkv_ln_g[j], rwkv_ln_b[j], lru_conv_w[j], lru_conv_b[j],
                                lru_w_a[j], lru_b_a[j], lru_w_x[j], lru_b_x[j], lru_lambda[j], batch, seq)
        x2 = _moe_layer(x2, y, gate_m, ln_g[layer, 0], ln_b[layer, 0], scale_f, shift_f, gate_f,
                        ln_g[layer, 1], ln_b[layer, 1], moe_w_grp[layer], moe_b_grp[layer],
                        moe_w_exp[layer], moe_b_exp[layer], moe_w_gate, moe_w_up, moe_w_down, layer, seq)
    return x2.reshape(batch, seq, d)
```

```python
import functools

import jax
import jax.numpy as jnp
from jax import lax
from jax.experimental import pallas as pl
from jax.experimental.pallas import tpu as pltpu

f32 = jnp.float32
bf16 = jnp.bfloat16
i32 = jnp.int32

DEPTH = 2
CHUNK = 64
ROPE_THETA = 10000.0
LN_EPS = 1e-5
ALPHA = (2 * DEPTH) ** 0.25
HEAD_DIM = 128
IDX_HEADS = 16
IDX_DIM = 64
TOPK_MAX = 256
C_HEAD_DIM = 64
C_DECAY_RANK = 128
C_ICLR_RANK = 128
C_GATE_RANK = 256
C_EPS = 64e-5
D_BLOCKS = 16
D_CONV = 4
LRU_C = 8.0
N_GROUPS = 4
EXPERTS_PER_GROUP = 8
N_EXPERTS = N_GROUPS * EXPERTS_PER_GROUP
D_EXPERT = 512

LANES = 128
SUBLANES = 8
VMEM_BYTES_V7X = 64 * 1024 * 1024
VMEM_HEADROOM = 8 * 1024 * 1024


def _cparams(semantics, vmem_bytes):
    limit = min(int(vmem_bytes) + VMEM_HEADROOM, VMEM_BYTES_V7X - VMEM_HEADROOM)
    return pltpu.CompilerParams(dimension_semantics=semantics, vmem_limit_bytes=limit)


def _split_bf16(a):
    hi = a.astype(bf16)
    lo = (a - hi.astype(f32)).astype(bf16)
    return hi, lo


def _dot3(a, b, dims=(((1,), (0,)), ((), ()))):
    ah, al = _split_bf16(a)
    bh, bl = _split_bf16(b)
    dg = functools.partial(lax.dot_general, dimension_numbers=dims, preferred_element_type=f32)
    return dg(ah, bh) + dg(ah, bl) + dg(al, bh)


def _sigmoid(x):
    return 1.0 / (1.0 + jnp.exp(-x))


def _modcast_kernel(x_ref, sc_ref, sh_ref, hi_ref, lo_ref):
    h = x_ref[...] * (1.0 + sc_ref[0]) + sh_ref[0]
    hi, lo = _split_bf16(h)
    hi_ref[...] = hi
    lo_ref[...] = lo


def _modcast(x2, sc, sh, seq):
    m, k = x2.shape
    tm = 256
    row = pl.BlockSpec((tm, k), lambda i: (i, 0))
    per_batch = pl.BlockSpec((1, 1, k), lambda i: (i * tm // seq, 0, 0))
    return pl.pallas_call(
        _modcast_kernel,
        grid=(m // tm,),
        in_specs=[row, per_batch, per_batch],
        out_specs=[row, row],
        out_shape=[jax.ShapeDtypeStruct((m, k), bf16)] * 2,
        compiler_params=_cparams(("parallel",), 2 * tm * k * (4 + 2 + 2)),
        name="modcast",
    )(x2, sc, sh)


def _mm_kernel(*refs, nparts, epilogue):
    acc = None
    for p in range(nparts):
        d = jnp.dot(refs[p][...], refs[nparts + p][...].astype(bf16), preferred_element_type=f32)
        acc = d if acc is None else acc + d
    epilogue(acc, *refs[2 * nparts:])


def _mm3_kernel(xh_ref, xl_ref, w_ref, o_ref):
    wh, wl = _split_bf16(w_ref[...])
    xh = xh_ref[...]
    acc = jnp.dot(xh, wh, preferred_element_type=f32)
    acc += jnp.dot(xh, wl, preferred_element_type=f32)
    acc += jnp.dot(xl_ref[...], wh, preferred_element_type=f32)
    o_ref[...] = acc


def _store_epilogue(acc, o_ref):
    o_ref[...] = acc.astype(o_ref.dtype)


def _rope_epilogue(acc, c_ref, s_ref, o_ref, *, scale):
    c = c_ref[...]
    s = s_ref[...]
    for k in range(acc.shape[1] // LANES):
        blk = acc[:, k * LANES:(k + 1) * LANES]
        rot = blk * c + pltpu.roll(blk, LANES // 2, 1) * s
        if scale != 1.0:
            rot = rot * scale
        o_ref[:, k * LANES:(k + 1) * LANES] = rot.astype(o_ref.dtype)


def _mm(xs, w, *, col_off, n, tm, tn, out_dtype, epilogue=_store_epilogue, extra=(), extra_specs=()):
    m = xs[0].shape[0]
    kp = xs[0].shape[1]
    nparts = len(xs)
    assert all(x.shape == (m, kp) for x in xs) and w.shape[0] == nparts * kp
    assert m % tm == 0 and n % tn == 0 and col_off % tn == 0
    jo = col_off // tn
    x_specs = [pl.BlockSpec((tm, kp), lambda i, j: (i, 0)) for _ in xs]
    w_specs = [pl.BlockSpec((kp, tn), lambda i, j, p=p: (p, jo + j)) for p in range(nparts)]
    vmem = 2 * nparts * (tm * kp * 2 + kp * tn * 4) + nparts * kp * tn * 2 + 4 * tm * tn * 4
    return pl.pallas_call(
        functools.partial(_mm_kernel, nparts=nparts, epilogue=epilogue),
        grid=(m // tm, n // tn),
        in_specs=x_specs + w_specs + list(extra_specs),
        out_specs=pl.BlockSpec((tm, tn), lambda i, j: (i, j)),
        out_shape=jax.ShapeDtypeStruct((m, n), out_dtype),
        compiler_params=_cparams(("parallel", "arbitrary"), vmem),
        name="mm",
    )(*xs, *([w] * nparts), *extra)


def _col_window_kernel(x_ref, halo_ref, o_ref, *, shift):
    lane = lax.broadcasted_iota(i32, (x_ref.shape[0], LANES), 1)
    nchunk = x_ref.shape[1] // LANES
    back = LANES - shift
    prev = pltpu.roll(x_ref[:, 0:LANES], back, 1)
    for k in range(nchunk):
        nxt_src = x_ref[:, (k + 1) * LANES:(k + 2) * LANES] if k + 1 < nchunk else halo_ref[...]
        nxt = pltpu.roll(nxt_src, back, 1)
        o_ref[:, k * LANES:(k + 1) * LANES] = jnp.where(lane < back, prev, nxt).astype(o_ref.dtype)
        prev = nxt


def _col_window(w, col_start, n):
    k = w.shape[0]
    shift = col_start % LANES
    base = col_start - shift
    tk, tn = 512, 512
    assert shift > 0 and base % tn == 0 and n % tn == 0 and k % tk == 0
    return pl.pallas_call(
        functools.partial(_col_window_kernel, shift=shift),
        grid=(k // tk, n // tn),
        in_specs=[pl.BlockSpec((tk, tn), lambda i, j: (i, base // tn + j)),
                  pl.BlockSpec((tk, LANES), lambda i, j: (i, (base + (j + 1) * tn) // LANES))],
        out_specs=pl.BlockSpec((tk, tn), lambda i, j: (i, j)),
        out_shape=jax.ShapeDtypeStruct((k, n), bf16),
        compiler_params=_cparams(("parallel", "parallel"), 8 * tk * tn * 4),
        name="col_window",
    )(w, w)


def _mm3(xh, xl, w, *, col_off, n, tm, tn):
    m, k = xh.shape
    assert m % tm == 0 and n % tn == 0 and col_off % tn == 0
    jo = col_off // tn
    xspec = pl.BlockSpec((tm, k), lambda i, j: (i, 0))
    vmem = 2 * (2 * tm * k * 2 + k * tn * 4) + 2 * k * tn * 2 + 4 * tm * tn * 4
    return pl.pallas_call(
        _mm3_kernel,
        grid=(m // tm, n // tn),
        in_specs=[xspec, xspec, pl.BlockSpec((k, tn), lambda i, j: (0, jo + j))],
        out_specs=pl.BlockSpec((tm, tn), lambda i, j: (i, j)),
        out_shape=jax.ShapeDtypeStruct((m, n), f32),
        compiler_params=_cparams(("parallel", "arbitrary"), vmem),
        name="mm3",
    )(xh, xl, w)


def _res_ln(x, y, gate, g, b):
    z = ALPHA * x + (1.0 + gate) * y
    mu = jnp.mean(z, axis=-1, keepdims=True)
    zc = z - mu
    var = jnp.mean(zc * zc, axis=-1, keepdims=True)
    return zc * lax.rsqrt(var + LN_EPS) * g + b


def _ln_router_kernel(x_ref, y_ref, gate_ref, g_ref, b_ref, sc_ref, sh_ref, wr_ref, br_ref,
                      xo_ref, h_ref, rw_ref, rid_ref):
    xn = _res_ln(x_ref[...], y_ref[...], gate_ref[0], g_ref[...], b_ref[...])
    xo_ref[...] = xn
    h = xn * (1.0 + sc_ref[0]) + sh_ref[0]
    h_ref[...] = h
    logits = _dot3(h, wr_ref[...]) + br_ref[...]
    lane = lax.broadcasted_iota(i32, logits.shape, 1)
    neg = jnp.float32(-jnp.inf)
    is_grp = lane < N_GROUPS
    gl = jnp.where(is_grp, logits, neg)
    gmax = jnp.max(gl, axis=1, keepdims=True)
    gidx = jnp.min(jnp.where(gl == gmax, lane, LANES), axis=1, keepdims=True)
    gsum = jnp.sum(jnp.where(is_grp, jnp.exp(gl - gmax), 0.0), axis=1, keepdims=True)
    grp_p = 1.0 / gsum
    lo = N_GROUPS + EXPERTS_PER_GROUP * gidx
    in_grp = jnp.logical_and(lane >= lo, lane < lo + EXPERTS_PER_GROUP)
    el = jnp.where(in_grp, logits, neg)
    m1 = jnp.max(el, axis=1, keepdims=True)
    i1 = jnp.min(jnp.where(el == m1, lane, LANES), axis=1, keepdims=True)
    el2 = jnp.where(lane == i1, neg, el)
    m2 = jnp.max(el2, axis=1, keepdims=True)
    i2 = jnp.min(jnp.where(el2 == m2, lane, LANES), axis=1, keepdims=True)
    e2 = jnp.exp(m2 - m1)
    w1 = grp_p / (1.0 + e2)
    w2 = grp_p * e2 / (1.0 + e2)
    rw_ref[...] = jnp.where(lane == 0, w1, jnp.where(lane == 1, w2, 0.0))
    rid_ref[...] = jnp.where(lane == 0, i1 - N_GROUPS, jnp.where(lane == 1, i2 - N_GROUPS, 0))


def _ln_router(x2, y2, gate, g, b, sc, sh, wr, br, seq):
    m, d = x2.shape
    tm = 256
    row = pl.BlockSpec((tm, d), lambda i: (i, 0))
    per_batch = pl.BlockSpec((1, 1, d), lambda i: (i * tm // seq, 0, 0))
    vec = pl.BlockSpec((1, d), lambda i: (0, 0))
    small = pl.BlockSpec((tm, LANES), lambda i: (i, 0))
    return pl.pallas_call(
        _ln_router_kernel,
        grid=(m // tm,),
        in_specs=[row, row, per_batch, vec, vec, per_batch, per_batch,
                  pl.BlockSpec((d, LANES), lambda i: (0, 0)), pl.BlockSpec((1, LANES), lambda i: (0, 0))],
        out_specs=[row, row, small, small],
        out_shape=[jax.ShapeDtypeStruct((m, d), f32), jax.ShapeDtypeStruct((m, d), f32),
                   jax.ShapeDtypeStruct((m, LANES), f32), jax.ShapeDtypeStruct((m, LANES), i32)],
        compiler_params=_cparams(("parallel",), 12 * tm * d * 4 + 2 * d * LANES * 4),
        name="ln_router",
    )(x2, y2, gate, g, b, sc, sh, wr, br)


MOE_TM = 256
MOE_HALF = D_EXPERT // 2
DMA_UNROLL = 8


def _row_copy(src_hbm, row, dst_vmem, r, sem):
    return pltpu.make_async_copy(src_hbm.at[pl.ds(row, 1), :], dst_vmem.at[pl.ds(r, 1), :], sem)


def _gather_start(idx_ref, base, src_hbm, dst_vmem, sem, nrows, stride=1):
    def issue(r, c):
        _row_copy(src_hbm, idx_ref[base + r * stride], dst_vmem, r, sem).start()
        return c

    lax.fori_loop(0, nrows, issue, 0, unroll=DMA_UNROLL)


def _gather_wait(src_hbm, dst_vmem, sem, nrows):
    def wait(r, c):
        _row_copy(src_hbm, 0, dst_vmem, r, sem).wait()
        return c

    lax.fori_loop(0, nrows, wait, 0, unroll=DMA_UNROLL)


def _moe_ffn_kernel(te_ref, tv_ref, tok_ref, h_hbm, wg_ref, wu_ref, wd_ref, o_ref, xbuf, sems):
    i = pl.program_id(0)
    n = pl.num_programs(0)
    slot = i % 2
    valid = tv_ref[i] > 0

    def start(tile, s):
        _gather_start(tok_ref, tile * MOE_TM, h_hbm, xbuf.at[s], sems.at[s], MOE_TM)

    @pl.when(jnp.logical_and(i == 0, valid))
    def _first_gather():
        start(0, 0)

    nxt = jnp.minimum(i + 1, n - 1)

    @pl.when(jnp.logical_and(i + 1 < n, tv_ref[nxt] > 0))
    def _next_gather():
        start(i + 1, 1 - slot)

    @pl.when(valid)
    def _compute():
        _gather_wait(h_hbm, xbuf.at[slot], sems.at[slot], MOE_TM)
        xb = xbuf[slot].astype(bf16)
        y = None
        for hf in range(D_EXPERT // MOE_HALF):
            cs = slice(hf * MOE_HALF, (hf + 1) * MOE_HALF)
            gate = jnp.dot(xb, wg_ref[:, cs].astype(bf16), preferred_element_type=f32)
            up = jnp.dot(xb, wu_ref[:, cs].astype(bf16), preferred_element_type=f32)
            hid = gate * _sigmoid(gate) * up
            part = jnp.dot(hid.astype(bf16), wd_ref[cs, :].astype(bf16), preferred_element_type=f32)
            y = part if y is None else y + part
        o_ref[...] = y

    @pl.when(jnp.logical_not(valid))
    def _empty():
        o_ref[...] = jnp.zeros_like(o_ref)


def _moe_ffn(h2, w_gate, w_up, w_down, layer, tile_e, tile_valid, row_tok, n_tiles):
    n, d = h2.shape
    tm = MOE_TM
    once = pl.Buffered(1)
    grid_spec = pltpu.PrefetchScalarGridSpec(
        num_scalar_prefetch=3,
        grid=(n_tiles,),
        in_specs=[
            pl.BlockSpec(memory_space=pl.ANY),
            pl.BlockSpec((None, None, d, D_EXPERT), lambda i, te, tv, tok: (layer, te[i], 0, 0), pipeline_mode=once),
            pl.BlockSpec((None, None, d, D_EXPERT), lambda i, te, tv, tok: (layer, te[i], 0, 0), pipeline_mode=once),
            pl.BlockSpec((None, None, D_EXPERT, d), lambda i, te, tv, tok: (layer, te[i], 0, 0), pipeline_mode=once),
        ],
        out_specs=pl.BlockSpec((tm, d), lambda i, te, tv, tok: (i, 0)),
        scratch_shapes=[pltpu.VMEM((2, tm, d), f32), pltpu.SemaphoreType.DMA((2,))],
    )
    vmem = 3 * d * D_EXPERT * 4 + 3 * d * MOE_HALF * 2 + 2 * tm * d * 4 + tm * d * 2 + 4 * tm * d * 4
    return pl.pallas_call(
        _moe_ffn_kernel,
        grid_spec=grid_spec,
        out_shape=jax.ShapeDtypeStruct((n_tiles * tm, d), f32),
        compiler_params=_cparams(("arbitrary",), vmem),
        name="moe_ffn",
    )(tile_e, tile_valid, row_tok, h2, w_gate, w_up, w_down)


def _moe_combine_kernel(pos_ref, ys_hbm, rw_ref, x_ref, gate_ref, g_ref, b_ref, o_ref, buf, sems):
    i = pl.program_id(0)
    n = pl.num_programs(0)
    tm = x_ref.shape[0]
    slot = i % 2

    def start(tile, s):
        for k in range(2):
            _gather_start(pos_ref, 2 * tile * tm + k, ys_hbm, buf.at[s, k], sems.at[s, k], tm, stride=2)

    @pl.when(i == 0)
    def _first_gather():
        start(0, 0)

    @pl.when(i + 1 < n)
    def _next_gather():
        start(i + 1, 1 - slot)

    for k in range(2):
        _gather_wait(ys_hbm, buf.at[slot, k], sems.at[slot, k], tm)
    rw = rw_ref[...]
    y = rw[:, 0:1] * buf[slot, 0] + rw[:, 1:2] * buf[slot, 1]
    o_ref[...] = _res_ln(x_ref[...], y, gate_ref[0], g_ref[...], b_ref[...])


def _moe_combine(pos, ys, rw, x2, gate, g, b, seq):
    n, d = x2.shape
    tm = 256
    row = lambda i, pos: (i, 0)
    grid_spec = pltpu.PrefetchScalarGridSpec(
        num_scalar_prefetch=1,
        grid=(n // tm,),
        in_specs=[
            pl.BlockSpec(memory_space=pl.ANY),
            pl.BlockSpec((tm, LANES), row),
            pl.BlockSpec((tm, d), row),
            pl.BlockSpec((1, 1, d), lambda i, pos: (i * tm // seq, 0, 0)),
            pl.BlockSpec((1, d), lambda i, pos: (0, 0)),
            pl.BlockSpec((1, d), lambda i, pos: (0, 0)),
        ],
        out_specs=pl.BlockSpec((tm, d), row),
        scratch_shapes=[pltpu.VMEM((2, 2, tm, d), f32), pltpu.SemaphoreType.DMA((2, 2))],
    )
    return pl.pallas_call(
        _moe_combine_kernel,
        grid_spec=grid_spec,
        out_shape=jax.ShapeDtypeStruct((n, d), f32),
        compiler_params=_cparams(("arbitrary",), 12 * tm * d * 4),
        name="moe_combine",
    )(pos, ys, rw, x2, gate, g, b)


def _moe_plan(eid, n_tiles):
    tm = MOE_TM
    flat_e = eid.reshape(-1)
    onehot = (flat_e[:, None] == jnp.arange(N_EXPERTS, dtype=i32)[None, :]).astype(i32)
    csum = jnp.cumsum(onehot, axis=0)
    rank = jnp.sum((csum - onehot) * onehot, axis=1)
    counts = csum[-1]
    padded = ((counts + tm - 1) // tm) * tm
    ends = jnp.cumsum(padded)
    pos = ((ends - padded)[flat_e] + rank).astype(i32)
    tile_start = jnp.arange(n_tiles, dtype=i32) * tm
    tile_valid = (tile_start < ends[-1]).astype(i32)
    tile_e = jnp.searchsorted(ends, tile_start, side="right").astype(i32)
    last_valid = jnp.maximum(ends[-1] // tm - 1, 0)
    tile_e = jnp.where(tile_valid > 0, tile_e, tile_e[last_valid])
    row_tok = jnp.zeros((n_tiles * tm,), i32).at[pos].set(jnp.arange(flat_e.shape[0], dtype=i32) // 2)
    return pos, tile_e, tile_valid, row_tok


def _moe_layer(x2, y2, gate_m, ln_g, ln_b, scale_f, shift_f, gate_f, ln_g2, ln_b2,
               w_grp, b_grp, w_exp, b_exp, w_gate, w_up, w_down, layer, seq):
    n, d = x2.shape
    pad = LANES - N_GROUPS - N_EXPERTS
    wr = jnp.concatenate([w_grp, w_exp, jnp.zeros((d, pad), f32)], axis=1)
    br = jnp.concatenate([b_grp, b_exp, jnp.zeros((pad,), f32)])[None, :]
    x1, h, rw, rid = _ln_router(x2, y2, gate_m, ln_g[None, :], ln_b[None, :], scale_f, shift_f, wr, br, seq)
    n_tiles = (2 * n) // MOE_TM + N_EXPERTS
    pos, tile_e, tile_valid, row_tok = _moe_plan(rid[:, :2], n_tiles)
    ys = _moe_ffn(h, w_gate, w_up, w_down, layer, tile_e, tile_valid, row_tok, n_tiles)
    return _moe_combine(pos, ys, rw, x1, gate_f, ln_g2[None, :], ln_b2[None, :], seq)


def _rope_table_kernel(pos_ref, fr_ref, sg_ref, c_ref, s_ref):
    ang = pos_ref[...].astype(f32) * fr_ref[...]
    c_ref[...] = jnp.cos(ang)
    s_ref[...] = jnp.sin(ang) * sg_ref[...]


def _rope_tables(pos2, head_dim):
    m = pos2.shape[0]
    half = head_dim // 2
    freqs = ROPE_THETA ** (-jnp.arange(half, dtype=f32) / half)
    reps = LANES // head_dim
    fr = jnp.tile(jnp.concatenate([freqs, freqs]), reps)[None, :]
    sg = jnp.tile(jnp.concatenate([-jnp.ones((half,), f32), jnp.ones((half,), f32)]), reps)[None, :]
    tm = min(m, 1024)
    vec = pl.BlockSpec((1, LANES), lambda i: (0, 0))
    out = pl.BlockSpec((tm, LANES), lambda i: (i, 0))
    return pl.pallas_call(
        _rope_table_kernel,
        grid=(m // tm,),
        in_specs=[pl.BlockSpec((tm, 1), lambda i: (i, 0)), vec, vec],
        out_specs=[out, out],
        out_shape=[jax.ShapeDtypeStruct((m, LANES), f32)] * 2,
        compiler_params=_cparams(("parallel",), 8 * tm * LANES * 4),
        name="rope_tables",
    )(pos2, fr, sg)


def _proj_rope(h_hi, w, c_tab, s_tab, *, col_off, n, scale, tm, tn):
    tab = pl.BlockSpec((tm, LANES), lambda i, j: (i, 0))
    return _mm([h_hi], w, col_off=col_off, n=n, tm=tm, tn=tn, out_dtype=bf16,
               epilogue=functools.partial(_rope_epilogue, scale=scale),
               extra=(c_tab, s_tab), extra_specs=(tab, tab))


IDX_W = IDX_HEADS * IDX_DIM
IDX_RAW = IDX_W + LANES


IDX_K = 4 * IDX_DIM


def _idx_rope_kernel(x_ref, c_ref, s_ref, q_ref, k_ref, w_ref):
    c = c_ref[...]
    s = s_ref[...]
    lane = lax.broadcasted_iota(i32, c.shape, 1)
    first = (lane & (IDX_DIM // 2)) == 0
    low = lane < IDX_DIM

    def rope(blk):
        swapped = jnp.where(first, pltpu.roll(blk, LANES - IDX_DIM // 2, 1), pltpu.roll(blk, IDX_DIM // 2, 1))
        return blk * c + swapped * s

    def hi_lo(x):
        hi = x.astype(bf16).astype(f32)
        return hi, x - hi

    for k in range(IDX_W // LANES):
        hi, lo = hi_lo(rope(x_ref[:, k * LANES:(k + 1) * LANES]))
        hi_sw = pltpu.roll(hi, IDX_DIM, 1)
        lo_sw = pltpu.roll(lo, IDX_DIM, 1)
        base = 2 * k * IDX_K
        q_ref[:, base:base + LANES] = jnp.where(low, hi, hi_sw).astype(bf16)
        q_ref[:, base + LANES:base + IDX_K] = jnp.where(low, lo, 0.0).astype(bf16)
        q_ref[:, base + IDX_K:base + IDX_K + LANES] = jnp.where(low, hi_sw, hi).astype(bf16)
        q_ref[:, base + IDX_K + LANES:base + 2 * IDX_K] = jnp.where(low, lo_sw, 0.0).astype(bf16)
    kw = x_ref[:, IDX_W:IDX_RAW]
    hi, lo = hi_lo(rope(kw))
    k_ref[:, 0:LANES] = jnp.where(low, hi, pltpu.roll(lo, IDX_DIM, 1)).astype(bf16)
    k_ref[:, LANES:IDX_K] = jnp.where(low, hi, 0.0).astype(bf16)
    w_ref[...] = kw


def _idx_rope(raw, c_tab, s_tab):
    m = raw.shape[0]
    tm = min(m, 512)
    tab = pl.BlockSpec((tm, LANES), lambda i: (i, 0))
    return pl.pallas_call(
        _idx_rope_kernel,
        grid=(m // tm,),
        in_specs=[pl.BlockSpec((tm, IDX_RAW), lambda i: (i, 0)), tab, tab],
        out_specs=[pl.BlockSpec((tm, IDX_HEADS * IDX_K), lambda i: (i, 0)),
                   pl.BlockSpec((tm, IDX_K), lambda i: (i, 0)), tab],
        out_shape=[jax.ShapeDtypeStruct((m, IDX_HEADS * IDX_K), bf16), jax.ShapeDtypeStruct((m, IDX_K), bf16),
                   jax.ShapeDtypeStruct((m, LANES), f32)],
        compiler_params=_cparams(("parallel",), 8 * tm * IDX_RAW * 4 + 4 * tm * IDX_HEADS * IDX_K * 2),
        name="idx_rope",
    )(raw, c_tab, s_tab)


_NT = (((1,), (1,)), ((), ()))
INT_MIN = -2 ** 31


DSA_T = 256
DSA_KG = 4


def _key_group(nkb):
    return DSA_KG if nkb % DSA_KG == 0 else 1


def _dsa_index_kernel(q_ref, k_ref, qw_ref, m_ref, key_scr, *, topk, scale):
    i = pl.program_id(1)
    nkb, tq, kb = key_scr.shape
    qw = qw_ref[...]
    q_chunk = (i * tq + lax.broadcasted_iota(i32, (tq, 1), 0)) // CHUNK
    col = lax.broadcasted_iota(i32, (tq, kb), 1)

    def score_block(j, c):
        kblk = k_ref[pl.ds(pl.multiple_of(j * kb, kb), kb), :]
        score = jnp.zeros((tq, kb), f32)
        for hd in range(IDX_HEADS):
            rel = lax.dot_general(q_ref[:, hd * IDX_K:(hd + 1) * IDX_K], kblk, _NT, preferred_element_type=f32)
            score = score + jnp.maximum(rel, 0.0) * qw[:, IDX_DIM + hd:IDX_DIM + hd + 1]
        score = score * scale
        adm = j * kb + col < (q_chunk + 1) * CHUNK
        bits = pltpu.bitcast(score, i32)
        key = jnp.where(bits < 0, bits ^ 0x7FFFFFFF, bits)
        key_scr[j] = jnp.where(adm, key, INT_MIN)
        return c

    lax.fori_loop(0, i + 1, score_block, 0)

    kg = _key_group(nkb)
    ngroups = (i + kg) // kg

    def fill(j, c):
        key_scr[j] = jnp.full((tq, kb), INT_MIN, i32)
        return c

    lax.fori_loop(i + 1, ngroups * kg, fill, 0)

    def count_ge(cand):
        def grp(g, acc):
            for u in range(kg):
                ge = (key_scr[g * kg + u] >= cand).astype(f32)
                acc = acc + ge[:, :LANES] + ge[:, LANES:]
            return acc

        acc = lax.fori_loop(0, ngroups, grp, jnp.zeros((tq, LANES), f32))
        return jnp.sum(acc, axis=1, keepdims=True)

    kf = jnp.float32(topk)
    cur = jnp.where(count_ge(jnp.zeros((tq, 1), i32)) >= kf, 0, INT_MIN).astype(i32)

    def bisect(it, cur):
        cand = cur + jnp.left_shift(jnp.int32(1), 30 - it)
        return jnp.where(count_ge(cand) >= kf, cand, cur)

    thr = jnp.maximum(lax.fori_loop(0, 31, bisect, cur), INT_MIN + 1)

    def write(j, c):
        m_ref[0, j] = (key_scr[j] >= thr).astype(bf16)
        return c

    def clear(j, c):
        m_ref[0, j] = jnp.zeros((tq, kb), bf16)
        return c

    lax.fori_loop(0, i + 1, write, 0)
    lax.fori_loop(i + 1, nkb, clear, 0)


def _dsa_index(iq, ik, iw, batch, seq):
    tq = min(seq, DSA_T)
    nq = seq // tq
    topk = min(TOPK_MAX, seq // 4)
    scale = IDX_DIM ** -0.5 * IDX_HEADS ** -0.5
    return pl.pallas_call(
        functools.partial(_dsa_index_kernel, topk=topk, scale=scale),
        grid=(batch, nq),
        in_specs=[pl.BlockSpec((tq, IDX_HEADS * IDX_K), lambda b, i: (b * nq + i, 0)),
                  pl.BlockSpec((seq, IDX_K), lambda b, i: (b, 0)),
                  pl.BlockSpec((tq, LANES), lambda b, i: (b * nq + i, 0))],
        out_specs=pl.BlockSpec((1, nq, tq, tq), lambda b, i: (b * nq + i, 0, 0, 0)),
        out_shape=jax.ShapeDtypeStruct((batch * nq, nq, tq, tq), bf16),
        scratch_shapes=[pltpu.VMEM((nq, tq, tq), i32)],
        compiler_params=_cparams(("parallel", "parallel"),
                                 6 * tq * seq * 4 + 4 * seq * IDX_K * 2 + 4 * tq * IDX_HEADS * IDX_K * 2),
        name="dsa_index",
    )(iq, ik, iw)


ATT_HG = 4
MASKED = -1e30


def _dsa_attn_kernel(q_ref, k_ref, v_ref, m_ref, o_ref, mx_s, den_s, acc_s):
    i = pl.program_id(2)
    _, nkb, tq, kb = m_ref.shape
    kg = _key_group(nkb)
    mx_s[...] = jnp.full(mx_s.shape, MASKED, f32)
    den_s[...] = jnp.zeros_like(den_s)
    acc_s[...] = jnp.zeros_like(acc_s)

    def group(g, c):
        rows = pl.ds(pl.multiple_of(g * (kg * kb), kg * kb), kg * kb)
        parts = [m_ref[0, g * kg + u] for u in range(kg)]
        sel = (jnp.concatenate(parts, axis=1) if kg > 1 else parts[0]) > 0
        for h in range(ATT_HG):
            sl = slice(h * HEAD_DIM, (h + 1) * HEAD_DIM)
            logits = lax.dot_general(q_ref[:, sl], k_ref[rows, sl], _NT, preferred_element_type=f32)
            logits = jnp.where(sel, logits, MASKED)
            mx = mx_s[h]
            mx_new = jnp.maximum(mx, jnp.max(logits, axis=1, keepdims=True))
            alpha = jnp.exp(mx - mx_new)
            p = jnp.exp(logits - mx_new)
            den_s[h] = alpha * den_s[h] + jnp.sum(p, axis=1, keepdims=True)
            acc_s[h] = alpha * acc_s[h] + jnp.dot(p.astype(bf16), v_ref[rows, sl], preferred_element_type=f32)
            mx_s[h] = mx_new
        return c

    lax.fori_loop(0, (i + kg) // kg, group, 0)
    for h in range(ATT_HG):
        o_ref[:, h * HEAD_DIM:(h + 1) * HEAD_DIM] = (acc_s[h] / den_s[h]).astype(o_ref.dtype)


def _dsa_attn(q, k, v, mask, batch, seq):
    m, w = q.shape
    _, nq, tq, _ = mask.shape
    gw = ATT_HG * HEAD_DIM
    qspec = pl.BlockSpec((tq, gw), lambda b, g, i: (b * nq + i, g))
    kvspec = pl.BlockSpec((seq, gw), lambda b, g, i: (b, g))
    vmem = 2 * (2 * tq * gw * 2 + 2 * seq * gw * 2 + tq * seq * 2) + 8 * tq * DSA_KG * tq * 4
    return pl.pallas_call(
        _dsa_attn_kernel,
        grid=(batch, w // gw, nq),
        in_specs=[qspec, kvspec, kvspec, pl.BlockSpec((1, nq, tq, tq), lambda b, g, i: (b * nq + i, 0, 0, 0))],
        out_specs=qspec,
        out_shape=jax.ShapeDtypeStruct((m, w), bf16),
        scratch_shapes=[pltpu.VMEM((ATT_HG, tq, 1), f32), pltpu.VMEM((ATT_HG, tq, 1), f32),
                        pltpu.VMEM((ATT_HG, tq, HEAD_DIM), f32)],
        compiler_params=_cparams(("parallel", "parallel", "arbitrary"), vmem),
        name="dsa_attn",
    )(q, k, v, mask)


RET_HG = 4


def _retention_kernel(q_ref, k_ref, v_ref, g_ref, gn_ref, lg_ref, o_ref, state):
    c = pl.program_id(2)
    t = q_ref.shape[0]

    @pl.when(c == 0)
    def _init():
        state[...] = jnp.zeros_like(state)

    ri = lax.broadcasted_iota(i32, (t, t), 0)
    ci = lax.broadcasted_iota(i32, (t, t), 1)
    diff = (ri - ci).astype(f32)
    pos = lax.broadcasted_iota(i32, (t, 1), 0).astype(f32)
    for h in range(RET_HG):
        sl = slice(h * HEAD_DIM, (h + 1) * HEAD_DIM)
        lg = lg_ref[:, h * HEAD_DIM:h * HEAD_DIM + 1]
        q = q_ref[:, sl]
        k = k_ref[:, sl]
        v = v_ref[:, sl]
        decay = jnp.where(diff >= 0, jnp.exp(lg * jnp.maximum(diff, 0.0)), 0.0)
        scores = lax.dot_general(q, k, _NT, preferred_element_type=f32) * decay
        inner = jnp.dot(scores.astype(bf16), v, preferred_element_type=f32)
        st = state[h]
        cross = jnp.dot(q, st.astype(bf16), preferred_element_type=f32) * jnp.exp(lg * (pos + 1.0))
        kz = k.astype(f32) * jnp.exp(lg * (t - 1.0 - pos))
        kv = jnp.dot(kz.T.astype(bf16), v, preferred_element_type=f32)
        state[h] = jnp.exp(lg * t) * st + kv
        ret = inner + cross
        mu = jnp.mean(ret, axis=1, keepdims=True)
        rc = ret - mu
        var = jnp.mean(rc * rc, axis=1, keepdims=True)
        gate = g_ref[:, sl]
        out = rc * lax.rsqrt(var + LN_EPS) * gn_ref[:, sl] * (gate * _sigmoid(gate))
        o_ref[:, sl] = out.astype(o_ref.dtype)


def _retention(q, k, v, g, gn_g, batch, seq):
    m, w = q.shape
    heads = w // HEAD_DIM
    t = min(seq, 256)
    nc = seq // t
    gw = RET_HG * HEAD_DIM
    log_g = jnp.log1p(-jnp.exp2(-5.0 - jnp.arange(heads, dtype=f32)))
    lg = jnp.repeat(log_g, HEAD_DIM)[None, :]
    blk = pl.BlockSpec((t, gw), lambda b, gi, c: (b * nc + c, gi))
    vec = pl.BlockSpec((1, gw), lambda b, gi, c: (0, gi))
    return pl.pallas_call(
        _retention_kernel,
        grid=(batch, w // gw, nc),
        in_specs=[blk, blk, blk, blk, vec, vec],
        out_specs=blk,
        out_shape=jax.ShapeDtypeStruct((m, w), bf16),
        scratch_shapes=[pltpu.VMEM((RET_HG, HEAD_DIM, HEAD_DIM), f32)],
        compiler_params=_cparams(("parallel", "parallel", "arbitrary"), 16 * t * gw * 4 + 8 * t * t * 4),
        name="retention",
    )(q, k, v, g, gn_g[None, :], lg)


def _mixer_sparse_retention(h_hi, h_lo, pos2, w_in, w_out, gn_g, batch, seq):
    m, d = h_hi.shape
    gw = d // 2
    tm = min(m, 1024)
    tn = 512
    c128, s128 = _rope_tables(pos2, HEAD_DIM)
    c64, s64 = _rope_tables(pos2, IDX_DIM)
    rope = functools.partial(_proj_rope, h_hi, c_tab=c128, s_tab=s128, n=gw, tm=tm, tn=tn)
    plain = functools.partial(_mm, [h_hi], n=gw, tm=tm, tn=tn)
    aq = rope(w_in, col_off=0, scale=HEAD_DIM ** -0.5)
    ak = rope(w_in, col_off=gw, scale=1.0)
    av = plain(w_in, col_off=2 * gw, out_dtype=bf16)
    idx_raw = _mm3(h_hi, h_lo, w_in, col_off=3 * gw, n=IDX_RAW, tm=min(m, 512), tn=IDX_RAW // 3)
    iq, ik, iw = _idx_rope(idx_raw, c64, s64)
    mask = _dsa_index(iq, ik, iw, batch, seq)
    a_out = _dsa_attn(aq, ak, av, mask, batch, seq)
    w_b = _col_window(w_in, 3 * gw + IDX_W + IDX_DIM + IDX_HEADS, 4 * gw)
    bq = rope(w_b, col_off=0, scale=1.0)
    bk = rope(w_b, col_off=gw, scale=HEAD_DIM ** -0.5)
    bv = plain(w_b, col_off=2 * gw, out_dtype=bf16)
    bg = plain(w_b, col_off=3 * gw, out_dtype=f32)
    b_out = _retention(bq, bk, bv, bg, gn_g, batch, seq)
    return _mm([a_out, b_out], w_out, col_off=0, n=d, tm=tm, tn=tn, out_dtype=f32)


MXU_N = 256


def _seg_sum64(x):
    r = lax.broadcasted_iota(i32, (MXU_N, MXU_N), 0) // C_HEAD_DIM
    c = lax.broadcasted_iota(i32, (MXU_N, MXU_N), 1) // C_HEAD_DIM
    ones = (r == c).astype(bf16)
    hi = x.astype(bf16)
    r1 = x - hi.astype(f32)
    mid = r1.astype(bf16)
    lo = (r1 - mid.astype(f32)).astype(bf16)
    cols = []
    for k in range(x.shape[1] // MXU_N):
        sl = slice(k * MXU_N, (k + 1) * MXU_N)
        s = jnp.dot(hi[:, sl], ones, preferred_element_type=f32)
        s += jnp.dot(mid[:, sl], ones, preferred_element_type=f32)
        s += jnp.dot(lo[:, sl], ones, preferred_element_type=f32)
        cols.append(s)
    return jnp.concatenate(cols, axis=1) if len(cols) > 1 else cols[0]


def _neg_softplus_neg(z):
    return jnp.minimum(z, 0.0) - jnp.log(1.0 + jnp.exp(-jnp.abs(z)))


def _rwkv_prep_kernel(p_ref, pp_ref, mu_ref, w0_ref, wup_ref, a0_ref, aup_ref, gup_ref, ka_ref, rk_ref,
                      r_o, w_o, k_o, a_o, v_o, g_o, bon_o, *, seq, gw):
    i = pl.program_id(0)
    tm = p_ref.shape[0]
    p = p_ref[...]
    prev_row = jnp.where((i * tm) % seq == 0, 0.0, pp_ref[SUBLANES - 1:SUBLANES, :])
    row = lax.broadcasted_iota(i32, (tm, 1), 0)
    shifted = jnp.where(row == 0, prev_row, pltpu.roll(p, 1, 0))
    pm = p + (shifted - p) * mu_ref[...]
    r = pm[:, 0:gw]
    k = pm[:, gw:2 * gw]
    v = pm[:, 2 * gw:3 * gw]
    o = 3 * gw
    dw = pm[:, o:o + C_DECAY_RANK]
    da = pm[:, o + C_DECAY_RANK:o + C_DECAY_RANK + C_ICLR_RANK]
    dg = pm[:, o + C_DECAY_RANK + C_ICLR_RANK:]
    w_log = _neg_softplus_neg(w0_ref[...] + _dot3(jnp.tanh(dw), wup_ref[...])) - 0.5
    decay = jnp.exp(-jnp.exp(w_log))
    a = _sigmoid(a0_ref[...] + _dot3(da, aup_ref[...]))
    g = _dot3(_sigmoid(dg), gup_ref[...])
    k2 = k * (1.0 + (a - 1.0) * ka_ref[...])
    r_o[...] = r
    w_o[...] = decay
    k_o[...] = k
    a_o[...] = a
    v_o[...] = v
    g_o[...] = g
    bon_o[...] = _seg_sum64(r * k2 * rk_ref[...]) * v


def _rwkv_prep(pc, mu, w0, w_up, a0, a_up, g_up, k_a, r_k, seq):
    m, cc = pc.shape
    gw = w0.shape[0]
    tm = 128
    nsub = tm // SUBLANES
    vec = pl.BlockSpec((1, gw), lambda i: (0, 0))
    out = pl.BlockSpec((tm, gw), lambda i: (i, 0))
    full = lambda a: pl.BlockSpec(a.shape, lambda i: (0, 0))
    return pl.pallas_call(
        functools.partial(_rwkv_prep_kernel, seq=seq, gw=gw),
        grid=(m // tm,),
        in_specs=[pl.BlockSpec((tm, cc), lambda i: (i, 0)),
                  pl.BlockSpec((SUBLANES, cc), lambda i: (jnp.maximum(i * nsub - 1, 0), 0)),
                  pl.BlockSpec((1, cc), lambda i: (0, 0)),
                  vec, full(w_up), vec, full(a_up), full(g_up), vec, vec],
        out_specs=[out] * 7,
        out_shape=[jax.ShapeDtypeStruct((m, gw), f32)] * 7,
        compiler_params=_cparams(("parallel",), 2 * tm * cc * 4 + 2 * 7 * tm * gw * 4 + 16 * tm * gw * 4),
        name="rwkv_prep",
    )(pc, pc, mu[None, :], w0[None, :], w_up, a0[None, :], a_up, g_up, k_a[None, :], r_k.reshape(1, -1))


def _rwkv_scan_kernel(w_ref, a_ref, k_ref, r_ref, v_ref, kkw_ref, kaw_ref, y_ref, z_ref,
                      w_s, kk_s, ka_s, k_s, r_s):
    c = pl.program_id(0)
    steps = w_ref.shape[0]
    nslab = z_ref.shape[0]

    @pl.when(c == 0)
    def _init():
        z_ref[...] = jnp.zeros_like(z_ref)

    def both_halves(x):
        return jnp.concatenate([x, x], axis=1)

    kkw = both_halves(kkw_ref[...])
    kaw = both_halves(kaw_ref[...])

    def derive(t, carry):
        a = both_halves(a_ref[t])
        kraw = both_halves(k_ref[t])
        kk = kraw * kkw
        kk = kk * lax.rsqrt(jnp.maximum(jnp.sum(kk * kk, axis=0, keepdims=True), 1e-24))
        w_s[t] = both_halves(w_ref[t])
        r_s[t] = both_halves(r_ref[t])
        kk_s[t] = kk
        ka_s[t] = kk * a
        k_s[t] = kraw * (1.0 + (a - 1.0) * kaw)
        return carry

    lax.fori_loop(0, steps, derive, 0)

    def step(t, carry):
        w = w_s[t]
        kk = kk_s[t]
        ka = ka_s[t]
        k = k_s[t]
        r = r_s[t]
        vrows = v_ref[t]
        for s in range(nslab):
            z = z_ref[s]
            sk = jnp.sum(z * kk, axis=0, keepdims=True)
            zn = z * w - ka * sk + k * vrows[s:s + 1, :]
            z_ref[s] = zn
            y_ref[t, s:s + 1, :] = jnp.sum(zn * r, axis=0, keepdims=True)
        return carry

    lax.fori_loop(0, steps, step, 0)


def _to_scan_cols(a, batch, seq):
    heads = a.shape[1] // C_HEAD_DIM
    return a.reshape(batch, seq, heads, C_HEAD_DIM).transpose(1, 3, 0, 2).reshape(seq, C_HEAD_DIM, batch * heads)


def _rwkv_scan(w, a, k, r, v, k_k, k_a, batch, seq):
    heads = w.shape[1] // C_HEAD_DIM
    half = batch * heads
    assert 2 * half == LANES
    nslab = C_HEAD_DIM // 2
    cols = [_to_scan_cols(x, batch, seq) for x in (w, a, k, r)]
    consts = [jnp.tile(p.reshape(heads, C_HEAD_DIM).T, (1, batch)) for p in (k_k, k_a)]
    vr = v.reshape(batch, seq, heads, nslab, 2).transpose(1, 3, 4, 0, 2).reshape(seq, nslab, LANES)
    steps = min(seq, 64)
    col_spec = pl.BlockSpec((steps, C_HEAD_DIM, half), lambda c: (c, 0, 0))
    row_spec = pl.BlockSpec((steps, nslab, LANES), lambda c: (c, 0, 0))
    const_spec = pl.BlockSpec((C_HEAD_DIM, half), lambda c: (0, 0))
    y = pl.pallas_call(
        _rwkv_scan_kernel,
        grid=(seq // steps,),
        in_specs=[col_spec] * 4 + [row_spec, const_spec, const_spec],
        out_specs=row_spec,
        out_shape=jax.ShapeDtypeStruct((seq, nslab, LANES), f32),
        scratch_shapes=[pltpu.VMEM((nslab, C_HEAD_DIM, LANES), f32)]
        + [pltpu.VMEM((steps, C_HEAD_DIM, LANES), f32)] * 5,
        compiler_params=_cparams(("arbitrary",), (2 * 4 + 5) * steps * C_HEAD_DIM * LANES * 4 + 6 * steps * nslab * LANES * 4),
        name="rwkv_scan",
    )(*cols, vr, *consts)
    return y.reshape(seq, nslab, 2, batch, heads).transpose(3, 0, 4, 1, 2).reshape(batch * seq, heads * C_HEAD_DIM)


def _rwkv_post_kernel(y_ref, g_ref, bon_ref, lng_ref, lnb_ref, o_ref):
    y = y_ref[...]
    mu = _seg_sum64(y) * (1.0 / C_HEAD_DIM)
    yc = y - mu
    var = _seg_sum64(yc * yc) * (1.0 / C_HEAD_DIM)
    yn = yc * lax.rsqrt(var + C_EPS) * lng_ref[...] + lnb_ref[...]
    o_ref[...] = ((yn + bon_ref[...]) * g_ref[...]).astype(o_ref.dtype)


def _rwkv_post(y, g, bonus, ln_g, ln_b):
    m, gw = y.shape
    tm = 256
    blk = pl.BlockSpec((tm, gw), lambda i: (i, 0))
    vec = pl.BlockSpec((1, gw), lambda i: (0, 0))
    return pl.pallas_call(
        _rwkv_post_kernel,
        grid=(m // tm,),
        in_specs=[blk, blk, blk, vec, vec],
        out_specs=blk,
        out_shape=jax.ShapeDtypeStruct((m, gw), bf16),
        compiler_params=_cparams(("parallel",), 16 * tm * gw * 4),
        name="rwkv_post",
    )(y, g, bonus, ln_g[None, :], ln_b[None, :])


GELU_C = 0.7978845608028654


def _lru_kernel(px_ref, pg_ref, cw_ref, cb_ref, wa_ref, ba_ref, wx_ref, bx_ref, lam_ref, o_ref,
                tail, hcar, a_s, b_s):
    c = pl.program_id(1)
    t = px_ref.shape[0]

    @pl.when(c == 0)
    def _init():
        tail[...] = jnp.zeros_like(tail)
        hcar[...] = jnp.zeros_like(hcar)

    x = px_ref[...]
    ext = jnp.concatenate([tail[...], x], axis=0)
    xc = cb_ref[...]
    for j in range(D_CONV):
        off = SUBLANES - (D_CONV - 1) + j
        xc = xc + cw_ref[j:j + 1, :] * ext[off:off + t, :]
    tail[...] = x[t - SUBLANES:, :]
    lam = lam_ref[...]
    sp = jnp.maximum(-lam, 0.0) + jnp.log(1.0 + jnp.exp(-jnp.abs(lam)))
    bw = wa_ref.shape[1]
    for n in range(wa_ref.shape[0]):
        sl = slice(n * bw, (n + 1) * bw)
        xb = xc[:, sl]
        rg = _sigmoid(_dot3(xb, wa_ref[n]) + ba_ref[:, sl])
        ig = _sigmoid(_dot3(xb, wx_ref[n]) + bx_ref[:, sl])
        log_a = -LRU_C * rg * sp[:, sl]
        a_s[:, sl] = jnp.exp(log_a)
        th = jnp.tanh(log_a)
        one_minus_a2 = -2.0 * th / (1.0 - th)
        b_s[:, sl] = jnp.sqrt(one_minus_a2) * (ig * xb)

    sub = lax.broadcasted_iota(i32, (SUBLANES, a_s.shape[1]), 0)

    def rows8(g, h):
        r0 = pl.multiple_of(g * SUBLANES, SUBLANES)
        a = a_s[pl.ds(r0, SUBLANES), :]
        b = b_s[pl.ds(r0, SUBLANES), :]
        for s in (1, 2, 4):
            b = a * jnp.where(sub >= s, pltpu.roll(b, s, 0), 0.0) + b
            a = a * jnp.where(sub >= s, pltpu.roll(a, s, 0), 1.0)
        hs = a * h + b
        b_s[pl.ds(r0, SUBLANES), :] = hs
        return hs[SUBLANES - 1:SUBLANES, :]

    hcar[...] = lax.fori_loop(0, t // SUBLANES, rows8, hcar[...])
    gate = pg_ref[...]
    gelu = 0.5 * gate * (1.0 + jnp.tanh(GELU_C * (gate + 0.044715 * (gate * gate * gate))))
    o_ref[...] = (b_s[...] * gelu).astype(o_ref.dtype)


def _lru(px, pg, conv_w, conv_b, w_a, b_a, w_x, b_x, lam, batch, seq):
    m, w = px.shape
    t = min(seq, 256)
    nc = seq // t
    blk = pl.BlockSpec((t, w), lambda b, c: (b * nc + c, 0))
    vec = pl.BlockSpec((1, w), lambda b, c: (0, 0))
    wblk = pl.BlockSpec(w_a.shape, lambda b, c: (0, 0, 0))
    return pl.pallas_call(
        _lru_kernel,
        grid=(batch, nc),
        in_specs=[blk, blk, pl.BlockSpec((D_CONV, w), lambda b, c: (0, 0)), vec, wblk, vec, wblk, vec, vec],
        out_specs=blk,
        out_shape=jax.ShapeDtypeStruct((m, w), bf16),
        scratch_shapes=[pltpu.VMEM((SUBLANES, w), f32), pltpu.VMEM((1, w), f32),
                        pltpu.VMEM((t, w), f32), pltpu.VMEM((t, w), f32)],
        compiler_params=_cparams(("parallel", "arbitrary"), 16 * t * w * 4),
        name="lru",
    )(px, pg, conv_w, conv_b[None, :], w_a, b_a[None, :], w_x, b_x[None, :], lam[None, :])


def _mixer_rwkv_lru(h_hi, w_in, w_out, mu, w0, w_up, a0, a_up, g_up, k_k, k_a, r_k, ln_g, ln_b,
                    conv_w, conv_b, w_a, b_a, w_x, b_x, lam, batch, seq):
    m, d = h_hi.shape
    gw = d // 2
    c_cols = 3 * gw + C_DECAY_RANK + C_ICLR_RANK + C_GATE_RANK
    tm = min(m, 1024)
    tn = 512
    proj = functools.partial(_mm, [h_hi], w_in, tm=tm, tn=tn, out_dtype=f32)
    pc = proj(col_off=0, n=c_cols)
    pg = proj(col_off=c_cols, n=gw)
    px = proj(col_off=c_cols + gw, n=gw)
    r, w, k, a, v, g, bonus = _rwkv_prep(pc, mu, w0, w_up, a0, a_up, g_up, k_a, r_k, seq)
    y = _rwkv_scan(w, a, k, r, v, k_k, k_a, batch, seq)
    c_out = _rwkv_post(y, g, bonus, ln_g, ln_b)
    d_out = _lru(px, pg, conv_w, conv_b, w_a, b_a, w_x, b_x, lam, batch, seq)
    return _mm([c_out, d_out], w_out, col_off=0, n=d, tm=tm, tn=tn, out_dtype=f32)


ADA_ROWS = 16


def _ada_kernel(c_ref, w_ref, b_ref, o_ref):
    c = c_ref[...]
    s = (c * _sigmoid(c)).astype(bf16)
    o_ref[...] = jnp.dot(s, w_ref[...].astype(bf16), preferred_element_type=f32) + b_ref[...]


def _ada(c, ada_w, ada_b):
    batch, d = c.shape
    n = ada_w.shape[1]
    tn = 512
    cp = jnp.zeros((ADA_ROWS, d), f32).at[:batch].set(c)
    out = pl.pallas_call(
        _ada_kernel,
        grid=(n // tn,),
        in_specs=[pl.BlockSpec((ADA_ROWS, d), lambda j: (0, 0)),
                  pl.BlockSpec((d, tn), lambda j: (0, j)),
                  pl.BlockSpec((1, tn), lambda j: (0, j))],
        out_specs=pl.BlockSpec((ADA_ROWS, tn), lambda j: (0, j)),
        out_shape=jax.ShapeDtypeStruct((ADA_ROWS, n), f32),
        compiler_params=_cparams(("parallel",), 3 * d * tn * 4),
        name="ada",
    )(cp, ada_w, ada_b[None, :])
    return out[:batch]


def kernel(x, c, positions, ada_w, ada_b, ada_table, ln_g, ln_b, ab_w_in, ab_w_out, ret_gn_g, cd_w_in, cd_w_out, rwkv_mu, rwkv_w0, rwkv_w_up, rwkv_a0, rwkv_a_up, rwkv_g_up, rwkv_k_k, rwkv_k_a, rwkv_r_k, rwkv_ln_g, rwkv_ln_b, lru_conv_w, lru_conv_b, lru_w_a, lru_b_a, lru_w_x, lru_b_x, lru_lambda, moe_w_grp, moe_b_grp, moe_w_exp, moe_b_exp, moe_w_gate, moe_w_up, moe_w_down):
    batch, seq, d = x.shape
    x2 = x.reshape(batch * seq, d)
    pos2 = positions.reshape(batch * seq, 1)
    ada = _ada(c, ada_w, ada_b).reshape(batch, 6, 1, d)
    for layer in range(DEPTH):
        mod = ada + ada_table[layer][None, :, None, :]
        shift_m, scale_m, gate_m, shift_f, scale_f, gate_f = (mod[:, i] for i in range(6))
        h_hi, h_lo = _modcast(x2, scale_m, shift_m, seq)
        j = layer // 2
        if layer % 2 == 0:
            y = _mixer_sparse_retention(h_hi, h_lo, pos2, ab_w_in[j], ab_w_out[j], ret_gn_g[j], batch, seq)
        else:
            y = _mixer_rwkv_lru(h_hi, cd_w_in[j], cd_w_out[j], rwkv_mu[j], rwkv_w0[j], rwkv_w_up[j],
                                rwkv_a0[j], rwkv_a_up[j], rwkv_g_up[j], rwkv_k_k[j], rwkv_k_a[j],
                                rwkv_r_k[j], rwkv_ln_g[j], rwkv_ln_b[j], lru_conv_w[j], lru_conv_b[j],
                                lru_w_a[j], lru_b_a[j], lru_w_x[j], lru_b_x[j], lru_lambda[j], batch, seq)
        x2 = _moe_layer(x2, y, gate_m, ln_g[layer, 0], ln_b[layer, 0], scale_f, shift_f, gate_f,
                        ln_g[layer, 1], ln_b[layer, 1], moe_w_grp[layer], moe_b_grp[layer],
                        moe_w_exp[layer], moe_b_exp[layer], moe_w_gate, moe_w_up, moe_w_down, layer, seq)
    return x2.reshape(batch, seq, d)
```

```python
import functools

import jax
import jax.numpy as jnp
from jax import lax
from jax.experimental import pallas as pl
from jax.experimental.pallas import tpu as pltpu

f32 = jnp.float32
bf16 = jnp.bfloat16
i32 = jnp.int32

DEPTH = 2
CHUNK = 64
ROPE_THETA = 10000.0
LN_EPS = 1e-5
ALPHA = (2 * DEPTH) ** 0.25
HEAD_DIM = 128
IDX_HEADS = 16
IDX_DIM = 64
TOPK_MAX = 256
C_HEAD_DIM = 64
C_DECAY_RANK = 128
C_ICLR_RANK = 128
C_GATE_RANK = 256
C_EPS = 64e-5
D_BLOCKS = 16
D_CONV = 4
LRU_C = 8.0
N_GROUPS = 4
EXPERTS_PER_GROUP = 8
N_EXPERTS = N_GROUPS * EXPERTS_PER_GROUP
D_EXPERT = 512

LANES = 128
SUBLANES = 8
VMEM_BYTES_V7X = 64 * 1024 * 1024
VMEM_HEADROOM = 8 * 1024 * 1024


def _cparams(semantics, vmem_bytes):
    limit = min(int(vmem_bytes) + VMEM_HEADROOM, VMEM_BYTES_V7X - VMEM_HEADROOM)
    return pltpu.CompilerParams(dimension_semantics=semantics, vmem_limit_bytes=limit)


def _split_bf16(a):
    hi = a.astype(bf16)
    lo = (a - hi.astype(f32)).astype(bf16)
    return hi, lo


def _dot3(a, b, dims=(((1,), (0,)), ((), ()))):
    ah, al = _split_bf16(a)
    bh, bl = _split_bf16(b)
    dg = functools.partial(lax.dot_general, dimension_numbers=dims, preferred_element_type=f32)
    return dg(ah, bh) + dg(ah, bl) + dg(al, bh)


def _sigmoid(x):
    return 1.0 / (1.0 + jnp.exp(-x))


def _modcast_kernel(x_ref, sc_ref, sh_ref, hi_ref, lo_ref):
    h = x_ref[...] * (1.0 + sc_ref[0]) + sh_ref[0]
    hi, lo = _split_bf16(h)
    hi_ref[...] = hi
    lo_ref[...] = lo


def _modcast(x2, sc, sh, seq):
    m, k = x2.shape
    tm = 256
    row = pl.BlockSpec((tm, k), lambda i: (i, 0))
    per_batch = pl.BlockSpec((1, 1, k), lambda i: (i * tm // seq, 0, 0))
    return pl.pallas_call(
        _modcast_kernel,
        grid=(m // tm,),
        in_specs=[row, per_batch, per_batch],
        out_specs=[row, row],
        out_shape=[jax.ShapeDtypeStruct((m, k), bf16)] * 2,
        compiler_params=_cparams(("parallel",), 2 * tm * k * (4 + 2 + 2)),
        name="modcast",
    )(x2, sc, sh)


_NN = (((1,), (0,)), ((), ()))
_NT = (((1,), (1,)), ((), ()))


def _mm_kernel(*refs, nparts, epilogue, dims):
    acc = None
    for p in range(nparts):
        d = lax.dot_general(refs[p][...], refs[nparts + p][...].astype(bf16), dims, preferred_element_type=f32)
        acc = d if acc is None else acc + d
    epilogue(acc, *refs[2 * nparts:])


def _mm3_kernel(xh_ref, xl_ref, w_ref, o_ref, *, dims):
    wh, wl = _split_bf16(w_ref[...])
    xh = xh_ref[...]
    dg = functools.partial(lax.dot_general, dimension_numbers=dims, preferred_element_type=f32)
    o_ref[...] = dg(xh, wh) + dg(xh, wl) + dg(xl_ref[...], wh)


def _w_spec(kp, tn, k_off, col_off, transposed):
    if transposed:
        assert col_off % SUBLANES == 0 and tn % SUBLANES == 0
        return pl.BlockSpec((pl.Element(tn), pl.Element(kp)),
                            lambda i, j: (pl.multiple_of(col_off + j * tn, SUBLANES), k_off))
    assert col_off % tn == 0 and k_off % kp == 0
    return pl.BlockSpec((kp, tn), lambda i, j: (k_off // kp, col_off // tn + j))


def _store_epilogue(acc, o_ref):
    o_ref[...] = acc.astype(o_ref.dtype)


def _rope_epilogue(acc, c_ref, s_ref, o_ref, *, scale):
    c = c_ref[...]
    s = s_ref[...]
    for k in range(acc.shape[1] // LANES):
        blk = acc[:, k * LANES:(k + 1) * LANES]
        rot = blk * c + pltpu.roll(blk, LANES // 2, 1) * s
        if scale != 1.0:
            rot = rot * scale
        o_ref[:, k * LANES:(k + 1) * LANES] = rot.astype(o_ref.dtype)


def _mm(xs, w, *, col_off, n, tm, tn, out_dtype, epilogue=_store_epilogue, extra=(), extra_specs=(),
        transposed=False):
    m = xs[0].shape[0]
    kp = xs[0].shape[1]
    nparts = len(xs)
    assert all(x.shape == (m, kp) for x in xs) and w.shape[1 if transposed else 0] == nparts * kp
    assert m % tm == 0 and n % tn == 0
    x_specs = [pl.BlockSpec((tm, kp), lambda i, j: (i, 0)) for _ in xs]
    w_specs = [_w_spec(kp, tn, p * kp, col_off, transposed) for p in range(nparts)]
    vmem = 2 * nparts * (tm * kp * 2 + kp * tn * 4) + nparts * kp * tn * 2 + 4 * tm * tn * 4
    return pl.pallas_call(
        functools.partial(_mm_kernel, nparts=nparts, epilogue=epilogue, dims=_NT if transposed else _NN),
        grid=(m // tm, n // tn),
        in_specs=x_specs + w_specs + list(extra_specs),
        out_specs=pl.BlockSpec((tm, tn), lambda i, j: (i, j)),
        out_shape=jax.ShapeDtypeStruct((m, n), out_dtype),
        compiler_params=_cparams(("parallel", "arbitrary"), vmem),
        name="mm",
    )(*xs, *([w] * nparts), *extra)


def _mm3(xh, xl, w, *, col_off, n, tm, tn, transposed=False):
    m, k = xh.shape
    assert m % tm == 0 and n % tn == 0
    xspec = pl.BlockSpec((tm, k), lambda i, j: (i, 0))
    vmem = 2 * (2 * tm * k * 2 + k * tn * 4) + 2 * k * tn * 2 + 4 * tm * tn * 4
    return pl.pallas_call(
        functools.partial(_mm3_kernel, dims=_NT if transposed else _NN),
        grid=(m // tm, n // tn),
        in_specs=[xspec, xspec, _w_spec(k, tn, 0, col_off, transposed)],
        out_specs=pl.BlockSpec((tm, tn), lambda i, j: (i, j)),
        out_shape=jax.ShapeDtypeStruct((m, n), f32),
        compiler_params=_cparams(("parallel", "arbitrary"), vmem),
        name="mm3",
    )(xh, xl, w)


def _res_ln(x, y, gate, g, b):
    z = ALPHA * x + (1.0 + gate) * y
    mu = jnp.mean(z, axis=-1, keepdims=True)
    zc = z - mu
    var = jnp.mean(zc * zc, axis=-1, keepdims=True)
    return zc * lax.rsqrt(var + LN_EPS) * g + b


def _ln_router_kernel(x_ref, y_ref, gate_ref, g_ref, b_ref, sc_ref, sh_ref, wr_ref, br_ref,
                      xo_ref, h_ref, rw_ref, rid_ref):
    xn = _res_ln(x_ref[...], y_ref[...], gate_ref[0], g_ref[...], b_ref[...])
    xo_ref[...] = xn
    h = xn * (1.0 + sc_ref[0]) + sh_ref[0]
    h_ref[...] = h
    logits = _dot3(h, wr_ref[...]) + br_ref[...]
    lane = lax.broadcasted_iota(i32, logits.shape, 1)
    neg = jnp.float32(-jnp.inf)
    is_grp = lane < N_GROUPS
    gl = jnp.where(is_grp, logits, neg)
    gmax = jnp.max(gl, axis=1, keepdims=True)
    gidx = jnp.min(jnp.where(gl == gmax, lane, LANES), axis=1, keepdims=True)
    gsum = jnp.sum(jnp.where(is_grp, jnp.exp(gl - gmax), 0.0), axis=1, keepdims=True)
    grp_p = 1.0 / gsum
    lo = N_GROUPS + EXPERTS_PER_GROUP * gidx
    in_grp = jnp.logical_and(lane >= lo, lane < lo + EXPERTS_PER_GROUP)
    el = jnp.where(in_grp, logits, neg)
    m1 = jnp.max(el, axis=1, keepdims=True)
    i1 = jnp.min(jnp.where(el == m1, lane, LANES), axis=1, keepdims=True)
    el2 = jnp.where(lane == i1, neg, el)
    m2 = jnp.max(el2, axis=1, keepdims=True)
    i2 = jnp.min(jnp.where(el2 == m2, lane, LANES), axis=1, keepdims=True)
    e2 = jnp.exp(m2 - m1)
    w1 = grp_p / (1.0 + e2)
    w2 = grp_p * e2 / (1.0 + e2)
    rw_ref[...] = jnp.where(lane == 0, w1, jnp.where(lane == 1, w2, 0.0))
    rid_ref[...] = jnp.where(lane == 0, i1 - N_GROUPS, jnp.where(lane == 1, i2 - N_GROUPS, 0))


def _ln_router(x2, y2, gate, g, b, sc, sh, wr, br, seq):
    m, d = x2.shape
    tm = 256
    row = pl.BlockSpec((tm, d), lambda i: (i, 0))
    per_batch = pl.BlockSpec((1, 1, d), lambda i: (i * tm // seq, 0, 0))
    vec = pl.BlockSpec((1, d), lambda i: (0, 0))
    small = pl.BlockSpec((tm, LANES), lambda i: (i, 0))
    return pl.pallas_call(
        _ln_router_kernel,
        grid=(m // tm,),
        in_specs=[row, row, per_batch, vec, vec, per_batch, per_batch,
                  pl.BlockSpec((d, LANES), lambda i: (0, 0)), pl.BlockSpec((1, LANES), lambda i: (0, 0))],
        out_specs=[row, row, small, small],
        out_shape=[jax.ShapeDtypeStruct((m, d), f32), jax.ShapeDtypeStruct((m, d), f32),
                   jax.ShapeDtypeStruct((m, LANES), f32), jax.ShapeDtypeStruct((m, LANES), i32)],
        compiler_params=_cparams(("parallel",), 12 * tm * d * 4 + 2 * d * LANES * 4),
        name="ln_router",
    )(x2, y2, gate, g, b, sc, sh, wr, br)


MOE_TM = 256
MOE_HALF = D_EXPERT // 2
DMA_UNROLL = 8


def _row_copy(src_hbm, row, dst_vmem, r, sem):
    return pltpu.make_async_copy(src_hbm.at[pl.ds(row, 1), :], dst_vmem.at[pl.ds(r, 1), :], sem)


def _gather_start(idx_ref, base, src_hbm, dst_vmem, sem, nrows, stride=1):
    def issue(r, c):
        _row_copy(src_hbm, idx_ref[base + r * stride], dst_vmem, r, sem).start()
        return c

    lax.fori_loop(0, nrows, issue, 0, unroll=DMA_UNROLL)


def _gather_wait(src_hbm, dst_vmem, sem, nrows):
    def wait(r, c):
        _row_copy(src_hbm, 0, dst_vmem, r, sem).wait()
        return c

    lax.fori_loop(0, nrows, wait, 0, unroll=DMA_UNROLL)


MOE_NHALF = D_EXPERT // MOE_HALF


def _moe_ffn_kernel(te_ref, tv_ref, tok_ref, h_hbm, wg_hbm, wu_hbm, wd_hbm, o_ref,
                    xbuf, wgb, wub, wdb, xsems, wsems, *, layer):
    i = pl.program_id(0)
    n = pl.num_programs(0)
    slot = i % 2
    valid = tv_ref[i] > 0
    nxt = jnp.minimum(i + 1, n - 1)
    has_next = jnp.logical_and(i + 1 < n, tv_ref[nxt] > 0)
    e = te_ref[i]
    next_e = te_ref[nxt]
    first = jnp.logical_or(i == 0, e != te_ref[jnp.maximum(i - 1, 0)])
    last = jnp.logical_and(has_next, next_e != e)

    def gather(tile, s):
        _gather_start(tok_ref, tile * MOE_TM, h_hbm, xbuf.at[s], xsems.at[s], MOE_TM)

    def weight_copies(expert, hf):
        cs = pl.ds(hf * MOE_HALF, MOE_HALF)
        return (pltpu.make_async_copy(wg_hbm.at[layer, expert, :, cs], wgb.at[hf], wsems.at[hf, 0]),
                pltpu.make_async_copy(wu_hbm.at[layer, expert, :, cs], wub.at[hf], wsems.at[hf, 1]),
                pltpu.make_async_copy(wd_hbm.at[layer, expert, cs, :], wdb.at[hf], wsems.at[hf, 2]))

    @pl.when(jnp.logical_and(i == 0, valid))
    def _prime():
        gather(0, 0)
        for hf in range(MOE_NHALF):
            for cp in weight_copies(e, hf):
                cp.start()

    @pl.when(has_next)
    def _next_gather():
        gather(i + 1, 1 - slot)

    @pl.when(valid)
    def _compute():
        _gather_wait(h_hbm, xbuf.at[slot], xsems.at[slot], MOE_TM)
        xb = xbuf[slot].astype(bf16)
        y = None
        for hf in range(MOE_NHALF):
            @pl.when(first)
            def _arrived(hf=hf):
                for cp in weight_copies(e, hf):
                    cp.wait()

            gate = jnp.dot(xb, wgb[hf].astype(bf16), preferred_element_type=f32)
            up = jnp.dot(xb, wub[hf].astype(bf16), preferred_element_type=f32)
            hid = gate * _sigmoid(gate) * up
            part = jnp.dot(hid.astype(bf16), wdb[hf].astype(bf16), preferred_element_type=f32)
            y = part if y is None else y + part

            @pl.when(last)
            def _reload(hf=hf):
                for cp in weight_copies(next_e, hf):
                    cp.start()

        o_ref[...] = y

    @pl.when(jnp.logical_not(valid))
    def _empty():
        o_ref[...] = jnp.zeros_like(o_ref)


def _moe_ffn(h2, w_gate, w_up, w_down, layer, tile_e, tile_valid, row_tok, n_tiles):
    n, d = h2.shape
    tm = MOE_TM
    anyspace = pl.BlockSpec(memory_space=pl.ANY)
    grid_spec = pltpu.PrefetchScalarGridSpec(
        num_scalar_prefetch=3,
        grid=(n_tiles,),
        in_specs=[anyspace, anyspace, anyspace, anyspace],
        out_specs=pl.BlockSpec((tm, d), lambda i, te, tv, tok: (i, 0)),
        scratch_shapes=[pltpu.VMEM((2, tm, d), f32),
                        pltpu.VMEM((MOE_NHALF, d, MOE_HALF), f32),
                        pltpu.VMEM((MOE_NHALF, d, MOE_HALF), f32),
                        pltpu.VMEM((MOE_NHALF, MOE_HALF, d), f32),
                        pltpu.SemaphoreType.DMA((2,)),
                        pltpu.SemaphoreType.DMA((MOE_NHALF, 3))],
    )
    vmem = 3 * d * D_EXPERT * 4 + 3 * d * MOE_HALF * 2 + 2 * tm * d * 4 + tm * d * 2 + 4 * tm * d * 4
    return pl.pallas_call(
        functools.partial(_moe_ffn_kernel, layer=layer),
        grid_spec=grid_spec,
        out_shape=jax.ShapeDtypeStruct((n_tiles * tm, d), f32),
        compiler_params=_cparams(("arbitrary",), vmem),
        name="moe_ffn",
    )(tile_e, tile_valid, row_tok, h2, w_gate, w_up, w_down)


def _moe_combine_kernel(pos_ref, ys_hbm, rw_ref, x_ref, gate_ref, g_ref, b_ref, o_ref, buf, sems):
    i = pl.program_id(0)
    n = pl.num_programs(0)
    tm = x_ref.shape[0]
    slot = i % 2

    def start(tile, s):
        for k in range(2):
            _gather_start(pos_ref, 2 * tile * tm + k, ys_hbm, buf.at[s, k], sems.at[s, k], tm, stride=2)

    @pl.when(i == 0)
    def _first_gather():
        start(0, 0)

    @pl.when(i + 1 < n)
    def _next_gather():
        start(i + 1, 1 - slot)

    for k in range(2):
        _gather_wait(ys_hbm, buf.at[slot, k], sems.at[slot, k], tm)
    rw = rw_ref[...]
    y = rw[:, 0:1] * buf[slot, 0] + rw[:, 1:2] * buf[slot, 1]
    o_ref[...] = _res_ln(x_ref[...], y, gate_ref[0], g_ref[...], b_ref[...])


def _moe_combine(pos, ys, rw, x2, gate, g, b, seq):
    n, d = x2.shape
    tm = 256
    row = lambda i, pos: (i, 0)
    grid_spec = pltpu.PrefetchScalarGridSpec(
        num_scalar_prefetch=1,
        grid=(n // tm,),
        in_specs=[
            pl.BlockSpec(memory_space=pl.ANY),
            pl.BlockSpec((tm, LANES), row),
            pl.BlockSpec((tm, d), row),
            pl.BlockSpec((1, 1, d), lambda i, pos: (i * tm // seq, 0, 0)),
            pl.BlockSpec((1, d), lambda i, pos: (0, 0)),
            pl.BlockSpec((1, d), lambda i, pos: (0, 0)),
        ],
        out_specs=pl.BlockSpec((tm, d), row),
        scratch_shapes=[pltpu.VMEM((2, 2, tm, d), f32), pltpu.SemaphoreType.DMA((2, 2))],
    )
    return pl.pallas_call(
        _moe_combine_kernel,
        grid_spec=grid_spec,
        out_shape=jax.ShapeDtypeStruct((n, d), f32),
        compiler_params=_cparams(("arbitrary",), 12 * tm * d * 4),
        name="moe_combine",
    )(pos, ys, rw, x2, gate, g, b)


def _moe_plan(eid, n_tiles):
    tm = MOE_TM
    flat_e = eid.reshape(-1)
    onehot = (flat_e[:, None] == jnp.arange(N_EXPERTS, dtype=i32)[None, :]).astype(i32)
    csum = jnp.cumsum(onehot, axis=0)
    rank = jnp.sum((csum - onehot) * onehot, axis=1)
    counts = csum[-1]
    padded = ((counts + tm - 1) // tm) * tm
    ends = jnp.cumsum(padded)
    pos = ((ends - padded)[flat_e] + rank).astype(i32)
    tile_start = jnp.arange(n_tiles, dtype=i32) * tm
    tile_valid = (tile_start < ends[-1]).astype(i32)
    tile_e = jnp.searchsorted(ends, tile_start, side="right").astype(i32)
    last_valid = jnp.maximum(ends[-1] // tm - 1, 0)
    tile_e = jnp.where(tile_valid > 0, tile_e, tile_e[last_valid])
    row_tok = jnp.zeros((n_tiles * tm,), i32).at[pos].set(jnp.arange(flat_e.shape[0], dtype=i32) // 2)
    return pos, tile_e, tile_valid, row_tok


def _moe_layer(x2, y2, gate_m, ln_g, ln_b, scale_f, shift_f, gate_f, ln_g2, ln_b2,
               w_grp, b_grp, w_exp, b_exp, w_gate, w_up, w_down, layer, seq):
    n, d = x2.shape
    pad = LANES - N_GROUPS - N_EXPERTS
    wr = jnp.concatenate([w_grp, w_exp, jnp.zeros((d, pad), f32)], axis=1)
    br = jnp.concatenate([b_grp, b_exp, jnp.zeros((pad,), f32)])[None, :]
    x1, h, rw, rid = _ln_router(x2, y2, gate_m, ln_g[None, :], ln_b[None, :], scale_f, shift_f, wr, br, seq)
    n_tiles = (2 * n) // MOE_TM + N_EXPERTS
    pos, tile_e, tile_valid, row_tok = _moe_plan(rid[:, :2], n_tiles)
    ys = _moe_ffn(h, w_gate, w_up, w_down, layer, tile_e, tile_valid, row_tok, n_tiles)
    return _moe_combine(pos, ys, rw, x1, gate_f, ln_g2[None, :], ln_b2[None, :], seq)


def _rope_table_kernel(pos_ref, fr_ref, sg_ref, c_ref, s_ref):
    ang = pos_ref[...].astype(f32) * fr_ref[...]
    c_ref[...] = jnp.cos(ang)
    s_ref[...] = jnp.sin(ang) * sg_ref[...]


def _rope_tables(pos2, head_dim):
    m = pos2.shape[0]
    half = head_dim // 2
    freqs = ROPE_THETA ** (-jnp.arange(half, dtype=f32) / half)
    reps = LANES // head_dim
    fr = jnp.tile(jnp.concatenate([freqs, freqs]), reps)[None, :]
    sg = jnp.tile(jnp.concatenate([-jnp.ones((half,), f32), jnp.ones((half,), f32)]), reps)[None, :]
    tm = min(m, 1024)
    vec = pl.BlockSpec((1, LANES), lambda i: (0, 0))
    out = pl.BlockSpec((tm, LANES), lambda i: (i, 0))
    return pl.pallas_call(
        _rope_table_kernel,
        grid=(m // tm,),
        in_specs=[pl.BlockSpec((tm, 1), lambda i: (i, 0)), vec, vec],
        out_specs=[out, out],
        out_shape=[jax.ShapeDtypeStruct((m, LANES), f32)] * 2,
        compiler_params=_cparams(("parallel",), 8 * tm * LANES * 4),
        name="rope_tables",
    )(pos2, fr, sg)


def _proj_rope(h_hi, w_t, c_tab, s_tab, *, col_off, n, scale, tm, tn):
    tab = pl.BlockSpec((tm, LANES), lambda i, j: (i, 0))
    return _mm([h_hi], w_t, col_off=col_off, n=n, tm=tm, tn=tn, out_dtype=bf16, transposed=True,
               epilogue=functools.partial(_rope_epilogue, scale=scale),
               extra=(c_tab, s_tab), extra_specs=(tab, tab))


IDX_W = IDX_HEADS * IDX_DIM
IDX_RAW = IDX_W + LANES


IDX_K = 4 * IDX_DIM


def _idx_rope_kernel(x_ref, c_ref, s_ref, q_ref, k_ref, w_ref):
    c = c_ref[...]
    s = s_ref[...]
    lane = lax.broadcasted_iota(i32, c.shape, 1)
    first = (lane & (IDX_DIM // 2)) == 0
    low = lane < IDX_DIM

    def rope(blk):
        swapped = jnp.where(first, pltpu.roll(blk, LANES - IDX_DIM // 2, 1), pltpu.roll(blk, IDX_DIM // 2, 1))
        return blk * c + swapped * s

    def hi_lo(x):
        hi = x.astype(bf16).astype(f32)
        return hi, x - hi

    for k in range(IDX_W // LANES):
        hi, lo = hi_lo(rope(x_ref[:, k * LANES:(k + 1) * LANES]))
        hi_sw = pltpu.roll(hi, IDX_DIM, 1)
        lo_sw = pltpu.roll(lo, IDX_DIM, 1)
        base = 2 * k * IDX_K
        q_ref[:, base:base + LANES] = jnp.where(low, hi, hi_sw).astype(bf16)
        q_ref[:, base + LANES:base + IDX_K] = jnp.where(low, lo, 0.0).astype(bf16)
        q_ref[:, base + IDX_K:base + IDX_K + LANES] = jnp.where(low, hi_sw, hi).astype(bf16)
        q_ref[:, base + IDX_K + LANES:base + 2 * IDX_K] = jnp.where(low, lo_sw, 0.0).astype(bf16)
    kw = x_ref[:, IDX_W:IDX_RAW]
    hi, lo = hi_lo(rope(kw))
    k_ref[:, 0:LANES] = jnp.where(low, hi, pltpu.roll(lo, IDX_DIM, 1)).astype(bf16)
    k_ref[:, LANES:IDX_K] = jnp.where(low, hi, 0.0).astype(bf16)
    w_ref[...] = kw


def _idx_rope(raw, c_tab, s_tab):
    m = raw.shape[0]
    tm = min(m, 512)
    tab = pl.BlockSpec((tm, LANES), lambda i: (i, 0))
    return pl.pallas_call(
        _idx_rope_kernel,
        grid=(m // tm,),
        in_specs=[pl.BlockSpec((tm, IDX_RAW), lambda i: (i, 0)), tab, tab],
        out_specs=[pl.BlockSpec((tm, IDX_HEADS * IDX_K), lambda i: (i, 0)),
                   pl.BlockSpec((tm, IDX_K), lambda i: (i, 0)), tab],
        out_shape=[jax.ShapeDtypeStruct((m, IDX_HEADS * IDX_K), bf16), jax.ShapeDtypeStruct((m, IDX_K), bf16),
                   jax.ShapeDtypeStruct((m, LANES), f32)],
        compiler_params=_cparams(("parallel",), 8 * tm * IDX_RAW * 4 + 4 * tm * IDX_HEADS * IDX_K * 2),
        name="idx_rope",
    )(raw, c_tab, s_tab)


INT_MIN = -2 ** 31


DSA_T = 256
DSA_KG = 4
COUNT_ROWS = 64


def _key_group(nkb):
    return DSA_KG if nkb % DSA_KG == 0 else 1


def _dsa_index_kernel(q_ref, k_ref, qw_ref, m_ref, key_scr, *, topk, scale):
    i = pl.program_id(1)
    nkb, tq, kb = key_scr.shape
    qw = qw_ref[...]
    q_chunk = (i * tq + lax.broadcasted_iota(i32, (tq, 1), 0)) // CHUNK
    col = lax.broadcasted_iota(i32, (tq, kb), 1)

    def score_block(j, c):
        kblk = k_ref[pl.ds(pl.multiple_of(j * kb, kb), kb), :]
        score = jnp.zeros((tq, kb), f32)
        for hd in range(IDX_HEADS):
            rel = lax.dot_general(q_ref[:, hd * IDX_K:(hd + 1) * IDX_K], kblk, _NT, preferred_element_type=f32)
            score = score + jnp.maximum(rel, 0.0) * qw[:, IDX_DIM + hd:IDX_DIM + hd + 1]
        score = score * scale
        adm = j * kb + col < (q_chunk + 1) * CHUNK
        bits = pltpu.bitcast(score, i32)
        key = jnp.where(bits < 0, bits ^ 0x7FFFFFFF, bits)
        key_scr[j] = jnp.where(adm, key, INT_MIN)
        return c

    lax.fori_loop(0, i + 1, score_block, 0)

    kg = _key_group(nkb)
    ngroups = (i + kg) // kg

    def fill(j, c):
        key_scr[j] = jnp.full((tq, kb), INT_MIN, i32)
        return c

    lax.fori_loop(i + 1, ngroups * kg, fill, 0)

    def count_ge(cand):
        counts = []
        for r0 in range(0, tq, COUNT_ROWS):
            rows = slice(r0, min(r0 + COUNT_ROWS, tq))
            cr = cand[rows]

            def grp(g, acc, rows=rows, cr=cr):
                for u in range(kg):
                    ge = (key_scr[g * kg + u, rows, :] >= cr).astype(f32)
                    acc = acc + ge[:, :LANES] + ge[:, LANES:]
                return acc

            acc = lax.fori_loop(0, ngroups, grp, jnp.zeros((cr.shape[0], LANES), f32))
            counts.append(jnp.sum(acc, axis=1, keepdims=True))
        return jnp.concatenate(counts, axis=0) if len(counts) > 1 else counts[0]

    kf = jnp.float32(topk)
    cur = jnp.where(count_ge(jnp.zeros((tq, 1), i32)) >= kf, 0, INT_MIN).astype(i32)

    def bisect(it, cur):
        cand = cur + jnp.left_shift(jnp.int32(1), 30 - it)
        return jnp.where(count_ge(cand) >= kf, cand, cur)

    thr = jnp.maximum(lax.fori_loop(0, 31, bisect, cur), INT_MIN + 1)

    def write(j, c):
        m_ref[0, j] = (key_scr[j] >= thr).astype(bf16)
        return c

    def clear(j, c):
        m_ref[0, j] = jnp.zeros((tq, kb), bf16)
        return c

    lax.fori_loop(0, i + 1, write, 0)
    lax.fori_loop(i + 1, nkb, clear, 0)


def _dsa_index(iq, ik, iw, batch, seq):
    tq = min(seq, DSA_T)
    nq = seq // tq
    topk = min(TOPK_MAX, seq // 4)
    scale = IDX_DIM ** -0.5 * IDX_HEADS ** -0.5
    return pl.pallas_call(
        functools.partial(_dsa_index_kernel, topk=topk, scale=scale),
        grid=(batch, nq),
        in_specs=[pl.BlockSpec((tq, IDX_HEADS * IDX_K), lambda b, i: (b * nq + i, 0)),
                  pl.BlockSpec((seq, IDX_K), lambda b, i: (b, 0)),
                  pl.BlockSpec((tq, LANES), lambda b, i: (b * nq + i, 0))],
        out_specs=pl.BlockSpec((1, nq, tq, tq), lambda b, i: (b * nq + i, 0, 0, 0)),
        out_shape=jax.ShapeDtypeStruct((batch * nq, nq, tq, tq), bf16),
        scratch_shapes=[pltpu.VMEM((nq, tq, tq), i32)],
        compiler_params=_cparams(("parallel", "parallel"),
                                 6 * tq * seq * 4 + 4 * seq * IDX_K * 2 + 4 * tq * IDX_HEADS * IDX_K * 2),
        name="dsa_index",
    )(iq, ik, iw)


ATT_HG = 4
MASKED = -1e30


def _dsa_attn_kernel(q_ref, k_ref, v_ref, m_ref, o_ref):
    i = pl.program_id(2)
    _, nkb, tq, kb = m_ref.shape
    kg = _key_group(nkb)

    def group(g, carry):
        rows = pl.ds(pl.multiple_of(g * (kg * kb), kg * kb), kg * kb)
        parts = [m_ref[0, g * kg + u] for u in range(kg)]
        sel = (jnp.concatenate(parts, axis=1) if kg > 1 else parts[0]) > 0
        out = []
        for h in range(ATT_HG):
            mx, den, acc = carry[h]
            sl = slice(h * HEAD_DIM, (h + 1) * HEAD_DIM)
            logits = lax.dot_general(q_ref[:, sl], k_ref[rows, sl], _NT, preferred_element_type=f32)
            logits = jnp.where(sel, logits, MASKED)
            mx_new = jnp.maximum(mx, jnp.max(logits, axis=1, keepdims=True))
            alpha = jnp.exp(mx - mx_new)
            p = jnp.exp(logits - mx_new)
            den = alpha * den + jnp.sum(p, axis=1, keepdims=True)
            acc = alpha * acc + jnp.dot(p.astype(bf16), v_ref[rows, sl], preferred_element_type=f32)
            out.append((mx_new, den, acc))
        return tuple(out)

    init = tuple((jnp.full((tq, 1), MASKED, f32), jnp.zeros((tq, 1), f32), jnp.zeros((tq, HEAD_DIM), f32))
                 for _ in range(ATT_HG))
    final = lax.fori_loop(0, (i + kg) // kg, group, init)
    for h in range(ATT_HG):
        _, den, acc = final[h]
        o_ref[:, h * HEAD_DIM:(h + 1) * HEAD_DIM] = (acc / den).astype(o_ref.dtype)


def _dsa_attn(q, k, v, mask, batch, seq):
    m, w = q.shape
    _, nq, tq, _ = mask.shape
    gw = ATT_HG * HEAD_DIM
    qspec = pl.BlockSpec((tq, gw), lambda b, g, i: (b * nq + i, g))
    kvspec = pl.BlockSpec((seq, gw), lambda b, g, i: (b, g))
    vmem = 2 * (2 * tq * gw * 2 + 2 * seq * gw * 2 + tq * seq * 2) + 8 * tq * DSA_KG * tq * 4
    return pl.pallas_call(
        _dsa_attn_kernel,
        grid=(batch, w // gw, nq),
        in_specs=[qspec, kvspec, kvspec, pl.BlockSpec((1, nq, tq, tq), lambda b, g, i: (b * nq + i, 0, 0, 0))],
        out_specs=qspec,
        out_shape=jax.ShapeDtypeStruct((m, w), bf16),
        compiler_params=_cparams(("parallel", "parallel", "arbitrary"), vmem),
        name="dsa_attn",
    )(q, k, v, mask)


RET_HG = 4


def _retention_kernel(q_ref, k_ref, v_ref, g_ref, gn_ref, lg_ref, o_ref, state):
    c = pl.program_id(2)
    t = q_ref.shape[0]

    @pl.when(c == 0)
    def _init():
        state[...] = jnp.zeros_like(state)

    ri = lax.broadcasted_iota(i32, (t, t), 0)
    ci = lax.broadcasted_iota(i32, (t, t), 1)
    diff = (ri - ci).astype(f32)
    pos = lax.broadcasted_iota(i32, (t, 1), 0).astype(f32)
    for h in range(RET_HG):
        sl = slice(h * HEAD_DIM, (h + 1) * HEAD_DIM)
        lg = lg_ref[:, h * HEAD_DIM:h * HEAD_DIM + 1]
        q = q_ref[:, sl]
        k = k_ref[:, sl]
        v = v_ref[:, sl]
        decay = jnp.where(diff >= 0, jnp.exp(lg * jnp.maximum(diff, 0.0)), 0.0)
        scores = lax.dot_general(q, k, _NT, preferred_element_type=f32) * decay
        inner = jnp.dot(scores.astype(bf16), v, preferred_element_type=f32)
        st = state[h]
        cross = jnp.dot(q, st.astype(bf16), preferred_element_type=f32) * jnp.exp(lg * (pos + 1.0))
        kz = k.astype(f32) * jnp.exp(lg * (t - 1.0 - pos))
        kv = jnp.dot(kz.T.astype(bf16), v, preferred_element_type=f32)
        state[h] = jnp.exp(lg * t) * st + kv
        ret = inner + cross
        mu = jnp.mean(ret, axis=1, keepdims=True)
        rc = ret - mu
        var = jnp.mean(rc * rc, axis=1, keepdims=True)
        gate = g_ref[:, sl]
        out = rc * lax.rsqrt(var + LN_EPS) * gn_ref[:, sl] * (gate * _sigmoid(gate))
        o_ref[:, sl] = out.astype(o_ref.dtype)


def _retention(q, k, v, g, gn_g, batch, seq):
    m, w = q.shape
    heads = w // HEAD_DIM
    t = min(seq, 256)
    nc = seq // t
    gw = RET_HG * HEAD_DIM
    log_g = jnp.log1p(-jnp.exp2(-5.0 - jnp.arange(heads, dtype=f32)))
    lg = jnp.repeat(log_g, HEAD_DIM)[None, :]
    blk = pl.BlockSpec((t, gw), lambda b, gi, c: (b * nc + c, gi))
    vec = pl.BlockSpec((1, gw), lambda b, gi, c: (0, gi))
    return pl.pallas_call(
        _retention_kernel,
        grid=(batch, w // gw, nc),
        in_specs=[blk, blk, blk, blk, vec, vec],
        out_specs=blk,
        out_shape=jax.ShapeDtypeStruct((m, w), bf16),
        scratch_shapes=[pltpu.VMEM((RET_HG, HEAD_DIM, HEAD_DIM), f32)],
        compiler_params=_cparams(("parallel", "parallel", "arbitrary"), 16 * t * gw * 4 + 8 * t * t * 4),
        name="retention",
    )(q, k, v, g, gn_g[None, :], lg)


def _mixer_sparse_retention(h_hi, h_lo, pos2, w_in, w_out, gn_g, batch, seq):
    m, d = h_hi.shape
    gw = d // 2
    tm = min(m, 1024)
    tn = 512
    c128, s128 = _rope_tables(pos2, HEAD_DIM)
    c64, s64 = _rope_tables(pos2, IDX_DIM)
    w_t = jnp.swapaxes(w_in, 0, 1)
    rope = functools.partial(_proj_rope, h_hi, w_t, c_tab=c128, s_tab=s128, n=gw, tm=tm, tn=tn)
    plain = functools.partial(_mm, [h_hi], w_t, n=gw, tm=tm, tn=tn, transposed=True)
    aq = rope(col_off=0, scale=HEAD_DIM ** -0.5)
    ak = rope(col_off=gw, scale=1.0)
    av = plain(col_off=2 * gw, out_dtype=bf16)
    idx_raw = _mm3(h_hi, h_lo, w_t, col_off=3 * gw, n=IDX_RAW, tm=min(m, 512), tn=IDX_RAW // 3, transposed=True)
    iq, ik, iw = _idx_rope(idx_raw, c64, s64)
    mask = _dsa_index(iq, ik, iw, batch, seq)
    a_out = _dsa_attn(aq, ak, av, mask, batch, seq)
    b_off = 3 * gw + IDX_W + IDX_DIM + IDX_HEADS
    bq = rope(col_off=b_off, scale=1.0)
    bk = rope(col_off=b_off + gw, scale=HEAD_DIM ** -0.5)
    bv = plain(col_off=b_off + 2 * gw, out_dtype=bf16)
    bg = plain(col_off=b_off + 3 * gw, out_dtype=f32)
    b_out = _retention(bq, bk, bv, bg, gn_g, batch, seq)
    return _mm([a_out, b_out], w_out, col_off=0, n=d, tm=tm, tn=tn, out_dtype=f32)


MXU_N = 256


def _seg_sum64(x):
    r = lax.broadcasted_iota(i32, (MXU_N, MXU_N), 0) // C_HEAD_DIM
    c = lax.broadcasted_iota(i32, (MXU_N, MXU_N), 1) // C_HEAD_DIM
    ones = (r == c).astype(bf16)
    hi = x.astype(bf16)
    r1 = x - hi.astype(f32)
    mid = r1.astype(bf16)
    lo = (r1 - mid.astype(f32)).astype(bf16)
    cols = []
    for k in range(x.shape[1] // MXU_N):
        sl = slice(k * MXU_N, (k + 1) * MXU_N)
        s = jnp.dot(hi[:, sl], ones, preferred_element_type=f32)
        s += jnp.dot(mid[:, sl], ones, preferred_element_type=f32)
        s += jnp.dot(lo[:, sl], ones, preferred_element_type=f32)
        cols.append(s)
    return jnp.concatenate(cols, axis=1) if len(cols) > 1 else cols[0]


def _neg_softplus_neg(z):
    return jnp.minimum(z, 0.0) - jnp.log(1.0 + jnp.exp(-jnp.abs(z)))


def _rwkv_prep_kernel(p_ref, pp_ref, mu_ref, w0_ref, wup_ref, a0_ref, aup_ref, gup_ref, ka_ref, rk_ref,
                      r_o, w_o, k_o, a_o, v_o, g_o, bon_o, *, seq, gw):
    i = pl.program_id(0)
    tm = p_ref.shape[0]
    p = p_ref[...]
    prev_row = jnp.where((i * tm) % seq == 0, 0.0, pp_ref[SUBLANES - 1:SUBLANES, :])
    row = lax.broadcasted_iota(i32, (tm, 1), 0)
    shifted = jnp.where(row == 0, prev_row, pltpu.roll(p, 1, 0))
    pm = p + (shifted - p) * mu_ref[...]
    r = pm[:, 0:gw]
    k = pm[:, gw:2 * gw]
    v = pm[:, 2 * gw:3 * gw]
    o = 3 * gw
    dw = pm[:, o:o + C_DECAY_RANK]
    da = pm[:, o + C_DECAY_RANK:o + C_DECAY_RANK + C_ICLR_RANK]
    dg = pm[:, o + C_DECAY_RANK + C_ICLR_RANK:]
    w_log = _neg_softplus_neg(w0_ref[...] + _dot3(jnp.tanh(dw), wup_ref[...])) - 0.5
    decay = jnp.exp(-jnp.exp(w_log))
    a = _sigmoid(a0_ref[...] + _dot3(da, aup_ref[...]))
    g = _dot3(_sigmoid(dg), gup_ref[...])
    k2 = k * (1.0 + (a - 1.0) * ka_ref[...])
    r_o[...] = r
    w_o[...] = decay
    k_o[...] = k
    a_o[...] = a
    v_o[...] = v
    g_o[...] = g
    bon_o[...] = _seg_sum64(r * k2 * rk_ref[...]) * v


def _rwkv_prep(pc, mu, w0, w_up, a0, a_up, g_up, k_a, r_k, seq):
    m, cc = pc.shape
    gw = w0.shape[0]
    tm = 128
    nsub = tm // SUBLANES
    vec = pl.BlockSpec((1, gw), lambda i: (0, 0))
    out = pl.BlockSpec((tm, gw), lambda i: (i, 0))
    full = lambda a: pl.BlockSpec(a.shape, lambda i: (0, 0))
    return pl.pallas_call(
        functools.partial(_rwkv_prep_kernel, seq=seq, gw=gw),
        grid=(m // tm,),
        in_specs=[pl.BlockSpec((tm, cc), lambda i: (i, 0)),
                  pl.BlockSpec((SUBLANES, cc), lambda i: (jnp.maximum(i * nsub - 1, 0), 0)),
                  pl.BlockSpec((1, cc), lambda i: (0, 0)),
                  vec, full(w_up), vec, full(a_up), full(g_up), vec, vec],
        out_specs=[out] * 7,
        out_shape=[jax.ShapeDtypeStruct((m, gw), f32)] * 7,
        compiler_params=_cparams(("parallel",), 2 * tm * cc * 4 + 2 * 7 * tm * gw * 4 + 16 * tm * gw * 4),
        name="rwkv_prep",
    )(pc, pc, mu[None, :], w0[None, :], w_up, a0[None, :], a_up, g_up, k_a[None, :], r_k.reshape(1, -1))


def _rwkv_scan_kernel(w_ref, a_ref, k_ref, r_ref, v_ref, kkw_ref, kaw_ref, y_ref, z_ref,
                      w_s, kk_s, ka_s, k_s, r_s):
    c = pl.program_id(0)
    steps = w_ref.shape[0]
    nslab = z_ref.shape[0]

    @pl.when(c == 0)
    def _init():
        z_ref[...] = jnp.zeros_like(z_ref)

    def both_halves(x):
        return jnp.concatenate([x, x], axis=1)

    kkw = both_halves(kkw_ref[...])
    kaw = both_halves(kaw_ref[...])

    def derive(t, carry):
        a = both_halves(a_ref[t])
        kraw = both_halves(k_ref[t])
        kk = kraw * kkw
        kk = kk * lax.rsqrt(jnp.maximum(jnp.sum(kk * kk, axis=0, keepdims=True), 1e-24))
        w_s[t] = both_halves(w_ref[t])
        r_s[t] = both_halves(r_ref[t])
        kk_s[t] = kk
        ka_s[t] = kk * a
        k_s[t] = kraw * (1.0 + (a - 1.0) * kaw)
        return carry

    lax.fori_loop(0, steps, derive, 0, unroll=4)

    def step(t, carry):
        w = w_s[t]
        kk = kk_s[t]
        ka = ka_s[t]
        k = k_s[t]
        r = r_s[t]
        vrows = v_ref[t]
        for s in range(nslab):
            z = z_ref[s]
            sk = jnp.sum(z * kk, axis=0, keepdims=True)
            zn = z * w - ka * sk + k * vrows[s:s + 1, :]
            z_ref[s] = zn
            y_ref[t, s:s + 1, :] = jnp.sum(zn * r, axis=0, keepdims=True)
        return carry

    lax.fori_loop(0, steps, step, 0)


def _to_scan_cols(a, batch, seq):
    heads = a.shape[1] // C_HEAD_DIM
    return a.reshape(batch, seq, heads, C_HEAD_DIM).transpose(1, 3, 0, 2).reshape(seq, C_HEAD_DIM, batch * heads)


def _rwkv_scan(w, a, k, r, v, k_k, k_a, batch, seq):
    heads = w.shape[1] // C_HEAD_DIM
    half = batch * heads
    assert 2 * half == LANES
    nslab = C_HEAD_DIM // 2
    cols = [_to_scan_cols(x, batch, seq) for x in (w, a, k, r)]
    consts = [jnp.tile(p.reshape(heads, C_HEAD_DIM).T, (1, batch)) for p in (k_k, k_a)]
    vr = v.reshape(batch, seq, heads, nslab, 2).transpose(1, 3, 4, 0, 2).reshape(seq, nslab, LANES)
    steps = min(seq, 64)
    col_spec = pl.BlockSpec((steps, C_HEAD_DIM, half), lambda c: (c, 0, 0))
    row_spec = pl.BlockSpec((steps, nslab, LANES), lambda c: (c, 0, 0))
    const_spec = pl.BlockSpec((C_HEAD_DIM, half), lambda c: (0, 0))
    y = pl.pallas_call(
        _rwkv_scan_kernel,
        grid=(seq // steps,),
        in_specs=[col_spec] * 4 + [row_spec, const_spec, const_spec],
        out_specs=row_spec,
        out_shape=jax.ShapeDtypeStruct((seq, nslab, LANES), f32),
        scratch_shapes=[pltpu.VMEM((nslab, C_HEAD_DIM, LANES), f32)]
        + [pltpu.VMEM((steps, C_HEAD_DIM, LANES), f32)] * 5,
        compiler_params=_cparams(("arbitrary",), (2 * 4 + 5) * steps * C_HEAD_DIM * LANES * 4 + 6 * steps * nslab * LANES * 4),
        name="rwkv_scan",
    )(*cols, vr, *consts)
    return y.reshape(seq, nslab, 2, batch, heads).transpose(3, 0, 4, 1, 2).reshape(batch * seq, heads * C_HEAD_DIM)


def _rwkv_post_kernel(y_ref, g_ref, bon_ref, lng_ref, lnb_ref, o_ref):
    y = y_ref[...]
    mu = _seg_sum64(y) * (1.0 / C_HEAD_DIM)
    yc = y - mu
    var = _seg_sum64(yc * yc) * (1.0 / C_HEAD_DIM)
    yn = yc * lax.rsqrt(var + C_EPS) * lng_ref[...] + lnb_ref[...]
    o_ref[...] = ((yn + bon_ref[...]) * g_ref[...]).astype(o_ref.dtype)


def _rwkv_post(y, g, bonus, ln_g, ln_b):
    m, gw = y.shape
    tm = 256
    blk = pl.BlockSpec((tm, gw), lambda i: (i, 0))
    vec = pl.BlockSpec((1, gw), lambda i: (0, 0))
    return pl.pallas_call(
        _rwkv_post_kernel,
        grid=(m // tm,),
        in_specs=[blk, blk, blk, vec, vec],
        out_specs=blk,
        out_shape=jax.ShapeDtypeStruct((m, gw), bf16),
        compiler_params=_cparams(("parallel",), 16 * tm * gw * 4),
        name="rwkv_post",
    )(y, g, bonus, ln_g[None, :], ln_b[None, :])


GELU_C = 0.7978845608028654


def _lru_kernel(px_ref, pg_ref, cw_ref, cb_ref, wa_ref, ba_ref, wx_ref, bx_ref, lam_ref, o_ref,
                tail, hcar, a_s, b_s):
    c = pl.program_id(1)
    t = px_ref.shape[0]

    @pl.when(c == 0)
    def _init():
        tail[...] = jnp.zeros_like(tail)
        hcar[...] = jnp.zeros_like(hcar)

    x = px_ref[...]
    ext = jnp.concatenate([tail[...], x], axis=0)
    xc = cb_ref[...]
    for j in range(D_CONV):
        off = SUBLANES - (D_CONV - 1) + j
        xc = xc + cw_ref[j:j + 1, :] * ext[off:off + t, :]
    tail[...] = x[t - SUBLANES:, :]
    lam = lam_ref[...]
    sp = jnp.maximum(-lam, 0.0) + jnp.log(1.0 + jnp.exp(-jnp.abs(lam)))
    bw = wa_ref.shape[1]
    for n in range(wa_ref.shape[0]):
        sl = slice(n * bw, (n + 1) * bw)
        xb = xc[:, sl]
        rg = _sigmoid(_dot3(xb, wa_ref[n]) + ba_ref[:, sl])
        ig = _sigmoid(_dot3(xb, wx_ref[n]) + bx_ref[:, sl])
        log_a = -LRU_C * rg * sp[:, sl]
        a_s[:, sl] = jnp.exp(log_a)
        th = jnp.tanh(log_a)
        one_minus_a2 = -2.0 * th / (1.0 - th)
        b_s[:, sl] = jnp.sqrt(one_minus_a2) * (ig * xb)

    sub = lax.broadcasted_iota(i32, (SUBLANES, a_s.shape[1]), 0)

    def rows8(g, h):
        r0 = pl.multiple_of(g * SUBLANES, SUBLANES)
        a = a_s[pl.ds(r0, SUBLANES), :]
        b = b_s[pl.ds(r0, SUBLANES), :]
        for s in (1, 2, 4):
            b = a * jnp.where(sub >= s, pltpu.roll(b, s, 0), 0.0) + b
            a = a * jnp.where(sub >= s, pltpu.roll(a, s, 0), 1.0)
        hs = a * h + b
        b_s[pl.ds(r0, SUBLANES), :] = hs
        return hs[SUBLANES - 1:SUBLANES, :]

    hcar[...] = lax.fori_loop(0, t // SUBLANES, rows8, hcar[...])
    gate = pg_ref[...]
    gelu = 0.5 * gate * (1.0 + jnp.tanh(GELU_C * (gate + 0.044715 * (gate * gate * gate))))
    o_ref[...] = (b_s[...] * gelu).astype(o_ref.dtype)


def _lru(px, pg, conv_w, conv_b, w_a, b_a, w_x, b_x, lam, batch, seq):
    m, w = px.shape
    t = min(seq, 256)
    nc = seq // t
    blk = pl.BlockSpec((t, w), lambda b, c: (b * nc + c, 0))
    vec = pl.BlockSpec((1, w), lambda b, c: (0, 0))
    wblk = pl.BlockSpec(w_a.shape, lambda b, c: (0, 0, 0))
    return pl.pallas_call(
        _lru_kernel,
        grid=(batch, nc),
        in_specs=[blk, blk, pl.BlockSpec((D_CONV, w), lambda b, c: (0, 0)), vec, wblk, vec, wblk, vec, vec],
        out_specs=blk,
        out_shape=jax.ShapeDtypeStruct((m, w), bf16),
        scratch_shapes=[pltpu.VMEM((SUBLANES, w), f32), pltpu.VMEM((1, w), f32),
                        pltpu.VMEM((t, w), f32), pltpu.VMEM((t, w), f32)],
        compiler_params=_cparams(("parallel", "arbitrary"), 16 * t * w * 4),
        name="lru",
    )(px, pg, conv_w, conv_b[None, :], w_a, b_a[None, :], w_x, b_x[None, :], lam[None, :])


def _mixer_rwkv_lru(h_hi, w_in, w_out, mu, w0, w_up, a0, a_up, g_up, k_k, k_a, r_k, ln_g, ln_b,
                    conv_w, conv_b, w_a, b_a, w_x, b_x, lam, batch, seq):
    m, d = h_hi.shape
    gw = d // 2
    c_cols = 3 * gw + C_DECAY_RANK + C_ICLR_RANK + C_GATE_RANK
    tm = min(m, 1024)
    tn = 512
    proj = functools.partial(_mm, [h_hi], w_in, tm=tm, tn=tn, out_dtype=f32)
    pc = proj(col_off=0, n=c_cols)
    pg = proj(col_off=c_cols, n=gw)
    px = proj(col_off=c_cols + gw, n=gw)
    r, w, k, a, v, g, bonus = _rwkv_prep(pc, mu, w0, w_up, a0, a_up, g_up, k_a, r_k, seq)
    y = _rwkv_scan(w, a, k, r, v, k_k, k_a, batch, seq)
    c_out = _rwkv_post(y, g, bonus, ln_g, ln_b)
    d_out = _lru(px, pg, conv_w, conv_b, w_a, b_a, w_x, b_x, lam, batch, seq)
    return _mm([c_out, d_out], w_out, col_off=0, n=d, tm=tm, tn=tn, out_dtype=f32)


ADA_ROWS = 16


def _ada_kernel(c_ref, w_ref, b_ref, o_ref):
    c = c_ref[...]
    s = (c * _sigmoid(c)).astype(bf16)
    o_ref[...] = jnp.dot(s, w_ref[...].astype(bf16), preferred_element_type=f32) + b_ref[...]


def _ada(c, ada_w, ada_b):
    batch, d = c.shape
    n = ada_w.shape[1]
    tn = 512
    cp = jnp.zeros((ADA_ROWS, d), f32).at[:batch].set(c)
    out = pl.pallas_call(
        _ada_kernel,
        grid=(n // tn,),
        in_specs=[pl.BlockSpec((ADA_ROWS, d), lambda j: (0, 0)),
                  pl.BlockSpec((d, tn), lambda j: (0, j)),
                  pl.BlockSpec((1, tn), lambda j: (0, j))],
        out_specs=pl.BlockSpec((ADA_ROWS, tn), lambda j: (0, j)),
        out_shape=jax.ShapeDtypeStruct((ADA_ROWS, n), f32),
        compiler_params=_cparams(("parallel",), 3 * d * tn * 4),
        name="ada",
    )(cp, ada_w, ada_b[None, :])
    return out[:batch]


def kernel(x, c, positions, ada_w, ada_b, ada_table, ln_g, ln_b, ab_w_in, ab_w_out, ret_gn_g, cd_w_in, cd_w_out, rwkv_mu, rwkv_w0, rwkv_w_up, rwkv_a0, rwkv_a_up, rwkv_g_up, rwkv_k_k, rwkv_k_a, rwkv_r_k, rwkv_ln_g, rwkv_ln_b, lru_conv_w, lru_conv_b, lru_w_a, lru_b_a, lru_w_x, lru_b_x, lru_lambda, moe_w_grp, moe_b_grp, moe_w_exp, moe_b_exp, moe_w_gate, moe_w_up, moe_w_down):
    batch, seq, d = x.shape
    x2 = x.reshape(batch * seq, d)
    pos2 = positions.reshape(batch * seq, 1)
    ada = _ada(c, ada_w, ada_b).reshape(batch, 6, 1, d)
    for layer in range(DEPTH):
        mod = ada + ada_table[layer][None, :, None, :]
        shift_m, scale_m, gate_m, shift_f, scale_f, gate_f = (mod[:, i] for i in range(6))
        h_hi, h_lo = _modcast(x2, scale_m, shift_m, seq)
        j = layer // 2
        if layer % 2 == 0:
            y = _mixer_sparse_retention(h_hi, h_lo, pos2, ab_w_in[j], ab_w_out[j], ret_gn_g[j], batch, seq)
        else:
            y = _mixer_rwkv_lru(h_hi, cd_w_in[j], cd_w_out[j], rwkv_mu[j], rwkv_w0[j], rwkv_w_up[j],
                                rwkv_a0[j], rwkv_a_up[j], rwkv_g_up[j], rwkv_k_k[j], rwkv_k_a[j],
                                rwkv_r_k[j], rwkv_ln_g[j], rwkv_ln_b[j], lru_conv_w[j], lru_conv_b[j],
                                lru_w_a[j], lru_b_a[j], lru_w_x[j], lru_b_x[j], lru_lambda[j], batch, seq)
        x2 = _moe_layer(x2, y, gate_m, ln_g[layer, 0], ln_b[layer, 0], scale_f, shift_f, gate_f,
                        ln_g[layer, 1], ln_b[layer, 1], moe_w_grp[layer], moe_b_grp[layer],
                        moe_w_exp[layer], moe_b_exp[layer], moe_w_gate, moe_w_up, moe_w_down, layer, seq)
    return x2.reshape(batch, seq, d)
```

```python
import functools

import jax
import jax.numpy as jnp
from jax import lax
from jax.experimental import pallas as pl
from jax.experimental.pallas import tpu as pltpu

f32 = jnp.float32
bf16 = jnp.bfloat16
i32 = jnp.int32

DEPTH = 2
CHUNK = 64
ROPE_THETA = 10000.0
LN_EPS = 1e-5
ALPHA = (2 * DEPTH) ** 0.25
HEAD_DIM = 128
IDX_HEADS = 16
IDX_DIM = 64
TOPK_MAX = 256
C_HEAD_DIM = 64
C_DECAY_RANK = 128
C_ICLR_RANK = 128
C_GATE_RANK = 256
C_EPS = 64e-5
D_BLOCKS = 16
D_CONV = 4
LRU_C = 8.0
N_GROUPS = 4
EXPERTS_PER_GROUP = 8
N_EXPERTS = N_GROUPS * EXPERTS_PER_GROUP
D_EXPERT = 512

LANES = 128
SUBLANES = 8
VMEM_BYTES_V7X = 64 * 1024 * 1024
VMEM_HEADROOM = 8 * 1024 * 1024


def _cparams(semantics, vmem_bytes):
    limit = min(int(vmem_bytes) + VMEM_HEADROOM, VMEM_BYTES_V7X - VMEM_HEADROOM)
    return pltpu.CompilerParams(dimension_semantics=semantics, vmem_limit_bytes=limit)


def _split_bf16(a):
    hi = a.astype(bf16)
    lo = (a - hi.astype(f32)).astype(bf16)
    return hi, lo


def _dot3(a, b, dims=(((1,), (0,)), ((), ()))):
    ah, al = _split_bf16(a)
    bh, bl = _split_bf16(b)
    dg = functools.partial(lax.dot_general, dimension_numbers=dims, preferred_element_type=f32)
    return dg(ah, bh) + dg(ah, bl) + dg(al, bh)


def _sigmoid(x):
    return 1.0 / (1.0 + jnp.exp(-x))


def _modcast_kernel(x_ref, sc_ref, sh_ref, hi_ref, lo_ref):
    h = x_ref[...] * (1.0 + sc_ref[0]) + sh_ref[0]
    hi, lo = _split_bf16(h)
    hi_ref[...] = hi
    lo_ref[...] = lo


def _modcast(x2, sc, sh, seq):
    m, k = x2.shape
    tm = 256
    row = pl.BlockSpec((tm, k), lambda i: (i, 0))
    per_batch = pl.BlockSpec((1, 1, k), lambda i: (i * tm // seq, 0, 0))
    return pl.pallas_call(
        _modcast_kernel,
        grid=(m // tm,),
        in_specs=[row, per_batch, per_batch],
        out_specs=[row, row],
        out_shape=[jax.ShapeDtypeStruct((m, k), bf16)] * 2,
        compiler_params=_cparams(("parallel",), 2 * tm * k * (4 + 2 + 2)),
        name="modcast",
    )(x2, sc, sh)


_NN = (((1,), (0,)), ((), ()))
_NT = (((1,), (1,)), ((), ()))


def _mm_kernel(*refs, nparts, epilogue, dims):
    acc = None
    for p in range(nparts):
        d = lax.dot_general(refs[p][...], refs[nparts + p][...].astype(bf16), dims, preferred_element_type=f32)
        acc = d if acc is None else acc + d
    epilogue(acc, *refs[2 * nparts:])


def _mm3_kernel(xh_ref, xl_ref, w_ref, o_ref, *, dims):
    wh, wl = _split_bf16(w_ref[...])
    xh = xh_ref[...]
    dg = functools.partial(lax.dot_general, dimension_numbers=dims, preferred_element_type=f32)
    o_ref[...] = dg(xh, wh) + dg(xh, wl) + dg(xl_ref[...], wh)


def _w_spec(kp, tn, k_off, col_off, transposed):
    if transposed:
        assert col_off % SUBLANES == 0 and tn % SUBLANES == 0
        return pl.BlockSpec((pl.Element(tn), pl.Element(kp)),
                            lambda i, j: (pl.multiple_of(col_off + j * tn, SUBLANES), k_off))
    assert col_off % tn == 0 and k_off % kp == 0
    return pl.BlockSpec((kp, tn), lambda i, j: (k_off // kp, col_off // tn + j))


def _store_epilogue(acc, o_ref):
    o_ref[...] = acc.astype(o_ref.dtype)


def _rope_epilogue(acc, c_ref, s_ref, o_ref, *, scale):
    c = c_ref[...]
    s = s_ref[...]
    for k in range(acc.shape[1] // LANES):
        blk = acc[:, k * LANES:(k + 1) * LANES]
        rot = blk * c + pltpu.roll(blk, LANES // 2, 1) * s
        if scale != 1.0:
            rot = rot * scale
        o_ref[:, k * LANES:(k + 1) * LANES] = rot.astype(o_ref.dtype)


def _mm(xs, w, *, col_off, n, tm, tn, out_dtype, epilogue=_store_epilogue, extra=(), extra_specs=(),
        transposed=False):
    m = xs[0].shape[0]
    kp = xs[0].shape[1]
    nparts = len(xs)
    assert all(x.shape == (m, kp) for x in xs) and w.shape[1 if transposed else 0] == nparts * kp
    assert m % tm == 0 and n % tn == 0
    x_specs = [pl.BlockSpec((tm, kp), lambda i, j: (i, 0)) for _ in xs]
    w_specs = [_w_spec(kp, tn, p * kp, col_off, transposed) for p in range(nparts)]
    vmem = 2 * nparts * (tm * kp * 2 + kp * tn * 4) + nparts * kp * tn * 2 + 4 * tm * tn * 4
    return pl.pallas_call(
        functools.partial(_mm_kernel, nparts=nparts, epilogue=epilogue, dims=_NT if transposed else _NN),
        grid=(m // tm, n // tn),
        in_specs=x_specs + w_specs + list(extra_specs),
        out_specs=pl.BlockSpec((tm, tn), lambda i, j: (i, j)),
        out_shape=jax.ShapeDtypeStruct((m, n), out_dtype),
        compiler_params=_cparams(("parallel", "arbitrary"), vmem),
        name="mm",
    )(*xs, *([w] * nparts), *extra)


def _mm3(xh, xl, w, *, col_off, n, tm, tn, transposed=False):
    m, k = xh.shape
    assert m % tm == 0 and n % tn == 0
    xspec = pl.BlockSpec((tm, k), lambda i, j: (i, 0))
    vmem = 2 * (2 * tm * k * 2 + k * tn * 4) + 2 * k * tn * 2 + 4 * tm * tn * 4
    return pl.pallas_call(
        functools.partial(_mm3_kernel, dims=_NT if transposed else _NN),
        grid=(m // tm, n // tn),
        in_specs=[xspec, xspec, _w_spec(k, tn, 0, col_off, transposed)],
        out_specs=pl.BlockSpec((tm, tn), lambda i, j: (i, j)),
        out_shape=jax.ShapeDtypeStruct((m, n), f32),
        compiler_params=_cparams(("parallel", "arbitrary"), vmem),
        name="mm3",
    )(xh, xl, w)


def _res_ln(x, y, gate, g, b):
    z = ALPHA * x + (1.0 + gate) * y
    mu = jnp.mean(z, axis=-1, keepdims=True)
    zc = z - mu
    var = jnp.mean(zc * zc, axis=-1, keepdims=True)
    return zc * lax.rsqrt(var + LN_EPS) * g + b


def _ln_router_kernel(x_ref, y_ref, gate_ref, g_ref, b_ref, sc_ref, sh_ref, wr_ref, br_ref,
                      xo_ref, h_ref, rw_ref, rid_ref):
    xn = _res_ln(x_ref[...], y_ref[...], gate_ref[0], g_ref[...], b_ref[...])
    xo_ref[...] = xn
    h = xn * (1.0 + sc_ref[0]) + sh_ref[0]
    h_ref[...] = h
    logits = _dot3(h, wr_ref[...]) + br_ref[...]
    lane = lax.broadcasted_iota(i32, logits.shape, 1)
    neg = jnp.float32(-jnp.inf)
    is_grp = lane < N_GROUPS
    gl = jnp.where(is_grp, logits, neg)
    gmax = jnp.max(gl, axis=1, keepdims=True)
    gidx = jnp.min(jnp.where(gl == gmax, lane, LANES), axis=1, keepdims=True)
    gsum = jnp.sum(jnp.where(is_grp, jnp.exp(gl - gmax), 0.0), axis=1, keepdims=True)
    grp_p = 1.0 / gsum
    lo = N_GROUPS + EXPERTS_PER_GROUP * gidx
    in_grp = jnp.logical_and(lane >= lo, lane < lo + EXPERTS_PER_GROUP)
    el = jnp.where(in_grp, logits, neg)
    m1 = jnp.max(el, axis=1, keepdims=True)
    i1 = jnp.min(jnp.where(el == m1, lane, LANES), axis=1, keepdims=True)
    el2 = jnp.where(lane == i1, neg, el)
    m2 = jnp.max(el2, axis=1, keepdims=True)
    i2 = jnp.min(jnp.where(el2 == m2, lane, LANES), axis=1, keepdims=True)
    e2 = jnp.exp(m2 - m1)
    w1 = grp_p / (1.0 + e2)
    w2 = grp_p * e2 / (1.0 + e2)
    rw_ref[...] = jnp.where(lane == 0, w1, jnp.where(lane == 1, w2, 0.0))
    rid_ref[...] = jnp.where(lane == 0, i1 - N_GROUPS, jnp.where(lane == 1, i2 - N_GROUPS, 0))


def _ln_router(x2, y2, gate, g, b, sc, sh, wr, br, seq):
    m, d = x2.shape
    tm = 256
    row = pl.BlockSpec((tm, d), lambda i: (i, 0))
    per_batch = pl.BlockSpec((1, 1, d), lambda i: (i * tm // seq, 0, 0))
    vec = pl.BlockSpec((1, d), lambda i: (0, 0))
    small = pl.BlockSpec((tm, LANES), lambda i: (i, 0))
    return pl.pallas_call(
        _ln_router_kernel,
        grid=(m // tm,),
        in_specs=[row, row, per_batch, vec, vec, per_batch, per_batch,
                  pl.BlockSpec((d, LANES), lambda i: (0, 0)), pl.BlockSpec((1, LANES), lambda i: (0, 0))],
        out_specs=[row, row, small, small],
        out_shape=[jax.ShapeDtypeStruct((m, d), f32), jax.ShapeDtypeStruct((m, d), f32),
                   jax.ShapeDtypeStruct((m, LANES), f32), jax.ShapeDtypeStruct((m, LANES), i32)],
        compiler_params=_cparams(("parallel",), 12 * tm * d * 4 + 2 * d * LANES * 4),
        name="ln_router",
    )(x2, y2, gate, g, b, sc, sh, wr, br)


MOE_TM = 256
MOE_HALF = D_EXPERT // 2
DMA_UNROLL = 8


def _row_copy(src_hbm, row, dst_vmem, r, sem):
    return pltpu.make_async_copy(src_hbm.at[pl.ds(row, 1), :], dst_vmem.at[pl.ds(r, 1), :], sem)


def _gather_start(idx_ref, base, src_hbm, dst_vmem, sem, nrows, stride=1):
    def issue(r, c):
        _row_copy(src_hbm, idx_ref[base + r * stride], dst_vmem, r, sem).start()
        return c

    lax.fori_loop(0, nrows, issue, 0, unroll=DMA_UNROLL)


def _gather_wait(src_hbm, dst_vmem, sem, nrows):
    def wait(r, c):
        _row_copy(src_hbm, 0, dst_vmem, r, sem).wait()
        return c

    lax.fori_loop(0, nrows, wait, 0, unroll=DMA_UNROLL)


MOE_NHALF = D_EXPERT // MOE_HALF


def _moe_ffn_kernel(te_ref, tv_ref, tok_ref, h_hbm, wg_hbm, wu_hbm, wd_hbm, o_ref,
                    xbuf, wgb, wub, wdb, xsems, wsems, *, layer):
    i = pl.program_id(0)
    n = pl.num_programs(0)
    slot = i % 2
    valid = tv_ref[i] > 0
    nxt = jnp.minimum(i + 1, n - 1)
    has_next = jnp.logical_and(i + 1 < n, tv_ref[nxt] > 0)
    e = te_ref[i]
    next_e = te_ref[nxt]
    first = jnp.logical_or(i == 0, e != te_ref[jnp.maximum(i - 1, 0)])
    last = jnp.logical_and(has_next, next_e != e)

    def gather(tile, s):
        _gather_start(tok_ref, tile * MOE_TM, h_hbm, xbuf.at[s], xsems.at[s], MOE_TM)

    def weight_copies(expert, hf):
        cs = pl.ds(hf * MOE_HALF, MOE_HALF)
        return (pltpu.make_async_copy(wg_hbm.at[layer, expert, :, cs], wgb.at[hf], wsems.at[hf, 0]),
                pltpu.make_async_copy(wu_hbm.at[layer, expert, :, cs], wub.at[hf], wsems.at[hf, 1]),
                pltpu.make_async_copy(wd_hbm.at[layer, expert, cs, :], wdb.at[hf], wsems.at[hf, 2]))

    @pl.when(jnp.logical_and(i == 0, valid))
    def _prime():
        gather(0, 0)
        for hf in range(MOE_NHALF):
            for cp in weight_copies(e, hf):
                cp.start()

    def gather_next_part(part, nparts):
        rows = MOE_TM // nparts
        for r in range(part * rows, (part + 1) * rows):
            _row_copy(h_hbm, tok_ref[nxt * MOE_TM + r], xbuf.at[1 - slot], r, xsems.at[1 - slot]).start()

    @pl.when(valid)
    def _compute():
        _gather_wait(h_hbm, xbuf.at[slot], xsems.at[slot], MOE_TM)
        xb = xbuf[slot].astype(bf16)
        y = None
        for hf in range(MOE_NHALF):
            @pl.when(first)
            def _arrived(hf=hf):
                for cp in weight_copies(e, hf):
                    cp.wait()

            gather_next_part(2 * hf, 2 * MOE_NHALF)
            gate = jnp.dot(xb, wgb[hf].astype(bf16), preferred_element_type=f32)
            up = jnp.dot(xb, wub[hf].astype(bf16), preferred_element_type=f32)
            hid = gate * _sigmoid(gate) * up
            gather_next_part(2 * hf + 1, 2 * MOE_NHALF)
            part = jnp.dot(hid.astype(bf16), wdb[hf].astype(bf16), preferred_element_type=f32)
            y = part if y is None else y + part

            @pl.when(last)
            def _reload(hf=hf):
                for cp in weight_copies(next_e, hf):
                    cp.start()

        o_ref[...] = y

        @pl.when(jnp.logical_not(has_next))
        def _drain():
            _gather_wait(h_hbm, xbuf.at[1 - slot], xsems.at[1 - slot], MOE_TM)

    @pl.when(jnp.logical_not(valid))
    def _empty():
        o_ref[...] = jnp.zeros_like(o_ref)


def _moe_ffn(h2, w_gate, w_up, w_down, layer, tile_e, tile_valid, row_tok, n_tiles):
    n, d = h2.shape
    tm = MOE_TM
    anyspace = pl.BlockSpec(memory_space=pl.ANY)
    grid_spec = pltpu.PrefetchScalarGridSpec(
        num_scalar_prefetch=3,
        grid=(n_tiles,),
        in_specs=[anyspace, anyspace, anyspace, anyspace],
        out_specs=pl.BlockSpec((tm, d), lambda i, te, tv, tok: (i, 0)),
        scratch_shapes=[pltpu.VMEM((2, tm, d), f32),
                        pltpu.VMEM((MOE_NHALF, d, MOE_HALF), f32),
                        pltpu.VMEM((MOE_NHALF, d, MOE_HALF), f32),
                        pltpu.VMEM((MOE_NHALF, MOE_HALF, d), f32),
                        pltpu.SemaphoreType.DMA((2,)),
                        pltpu.SemaphoreType.DMA((MOE_NHALF, 3))],
    )
    vmem = 3 * d * D_EXPERT * 4 + 3 * d * MOE_HALF * 2 + 2 * tm * d * 4 + tm * d * 2 + 4 * tm * d * 4
    return pl.pallas_call(
        functools.partial(_moe_ffn_kernel, layer=layer),
        grid_spec=grid_spec,
        out_shape=jax.ShapeDtypeStruct((n_tiles * tm, d), f32),
        compiler_params=_cparams(("arbitrary",), vmem),
        name="moe_ffn",
    )(tile_e, tile_valid, row_tok, h2, w_gate, w_up, w_down)


def _moe_combine_kernel(pos_ref, ys_hbm, rw_ref, x_ref, gate_ref, g_ref, b_ref, o_ref, buf, sems):
    i = pl.program_id(0)
    n = pl.num_programs(0)
    tm = x_ref.shape[0]
    slot = i % 2

    def start(tile, s):
        for k in range(2):
            _gather_start(pos_ref, 2 * tile * tm + k, ys_hbm, buf.at[s, k], sems.at[s, k], tm, stride=2)

    @pl.when(i == 0)
    def _first_gather():
        start(0, 0)

    @pl.when(i + 1 < n)
    def _next_gather():
        start(i + 1, 1 - slot)

    for k in range(2):
        _gather_wait(ys_hbm, buf.at[slot, k], sems.at[slot, k], tm)
    rw = rw_ref[...]
    y = rw[:, 0:1] * buf[slot, 0] + rw[:, 1:2] * buf[slot, 1]
    o_ref[...] = _res_ln(x_ref[...], y, gate_ref[0], g_ref[...], b_ref[...])


def _moe_combine(pos, ys, rw, x2, gate, g, b, seq):
    n, d = x2.shape
    tm = 256
    row = lambda i, pos: (i, 0)
    grid_spec = pltpu.PrefetchScalarGridSpec(
        num_scalar_prefetch=1,
        grid=(n // tm,),
        in_specs=[
            pl.BlockSpec(memory_space=pl.ANY),
            pl.BlockSpec((tm, LANES), row),
            pl.BlockSpec((tm, d), row),
            pl.BlockSpec((1, 1, d), lambda i, pos: (i * tm // seq, 0, 0)),
            pl.BlockSpec((1, d), lambda i, pos: (0, 0)),
            pl.BlockSpec((1, d), lambda i, pos: (0, 0)),
        ],
        out_specs=pl.BlockSpec((tm, d), row),
        scratch_shapes=[pltpu.VMEM((2, 2, tm, d), f32), pltpu.SemaphoreType.DMA((2, 2))],
    )
    return pl.pallas_call(
        _moe_combine_kernel,
        grid_spec=grid_spec,
        out_shape=jax.ShapeDtypeStruct((n, d), f32),
        compiler_params=_cparams(("arbitrary",), 12 * tm * d * 4),
        name="moe_combine",
    )(pos, ys, rw, x2, gate, g, b)


def _moe_plan(eid, n_tiles):
    tm = MOE_TM
    flat_e = eid.reshape(-1)
    onehot = (flat_e[:, None] == jnp.arange(N_EXPERTS, dtype=i32)[None, :]).astype(i32)
    csum = jnp.cumsum(onehot, axis=0)
    rank = jnp.sum((csum - onehot) * onehot, axis=1)
    counts = csum[-1]
    padded = ((counts + tm - 1) // tm) * tm
    ends = jnp.cumsum(padded)
    pos = ((ends - padded)[flat_e] + rank).astype(i32)
    tile_start = jnp.arange(n_tiles, dtype=i32) * tm
    tile_valid = (tile_start < ends[-1]).astype(i32)
    tile_e = jnp.searchsorted(ends, tile_start, side="right").astype(i32)
    last_valid = jnp.maximum(ends[-1] // tm - 1, 0)
    tile_e = jnp.where(tile_valid > 0, tile_e, tile_e[last_valid])
    row_tok = jnp.zeros((n_tiles * tm,), i32).at[pos].set(jnp.arange(flat_e.shape[0], dtype=i32) // 2)
    return pos, tile_e, tile_valid, row_tok


def _moe_layer(x2, y2, gate_m, ln_g, ln_b, scale_f, shift_f, gate_f, ln_g2, ln_b2,
               w_grp, b_grp, w_exp, b_exp, w_gate, w_up, w_down, layer, seq):
    n, d = x2.shape
    pad = LANES - N_GROUPS - N_EXPERTS
    wr = jnp.concatenate([w_grp, w_exp, jnp.zeros((d, pad), f32)], axis=1)
    br = jnp.concatenate([b_grp, b_exp, jnp.zeros((pad,), f32)])[None, :]
    x1, h, rw, rid = _ln_router(x2, y2, gate_m, ln_g[None, :], ln_b[None, :], scale_f, shift_f, wr, br, seq)
    n_tiles = (2 * n) // MOE_TM + N_EXPERTS
    pos, tile_e, tile_valid, row_tok = _moe_plan(rid[:, :2], n_tiles)
    ys = _moe_ffn(h, w_gate, w_up, w_down, layer, tile_e, tile_valid, row_tok, n_tiles)
    return _moe_combine(pos, ys, rw, x1, gate_f, ln_g2[None, :], ln_b2[None, :], seq)


def _rope_table_kernel(pos_ref, fr_ref, sg_ref, c_ref, s_ref):
    ang = pos_ref[...].astype(f32) * fr_ref[...]
    c_ref[...] = jnp.cos(ang)
    s_ref[...] = jnp.sin(ang) * sg_ref[...]


def _rope_tables(pos2, head_dim):
    m = pos2.shape[0]
    half = head_dim // 2
    freqs = ROPE_THETA ** (-jnp.arange(half, dtype=f32) / half)
    reps = LANES // head_dim
    fr = jnp.tile(jnp.concatenate([freqs, freqs]), reps)[None, :]
    sg = jnp.tile(jnp.concatenate([-jnp.ones((half,), f32), jnp.ones((half,), f32)]), reps)[None, :]
    tm = min(m, 1024)
    vec = pl.BlockSpec((1, LANES), lambda i: (0, 0))
    out = pl.BlockSpec((tm, LANES), lambda i: (i, 0))
    return pl.pallas_call(
        _rope_table_kernel,
        grid=(m // tm,),
        in_specs=[pl.BlockSpec((tm, 1), lambda i: (i, 0)), vec, vec],
        out_specs=[out, out],
        out_shape=[jax.ShapeDtypeStruct((m, LANES), f32)] * 2,
        compiler_params=_cparams(("parallel",), 8 * tm * LANES * 4),
        name="rope_tables",
    )(pos2, fr, sg)


def _proj_rope(h_hi, w_t, c_tab, s_tab, *, col_off, n, scale, tm, tn):
    tab = pl.BlockSpec((tm, LANES), lambda i, j: (i, 0))
    return _mm([h_hi], w_t, col_off=col_off, n=n, tm=tm, tn=tn, out_dtype=bf16, transposed=True,
               epilogue=functools.partial(_rope_epilogue, scale=scale),
               extra=(c_tab, s_tab), extra_specs=(tab, tab))


IDX_W = IDX_HEADS * IDX_DIM
IDX_RAW = IDX_W + LANES


IDX_K = 4 * IDX_DIM


def _idx_rope_kernel(x_ref, c_ref, s_ref, q_ref, k_ref, w_ref):
    c = c_ref[...]
    s = s_ref[...]
    lane = lax.broadcasted_iota(i32, c.shape, 1)
    first = (lane & (IDX_DIM // 2)) == 0
    low = lane < IDX_DIM

    def rope(blk):
        swapped = jnp.where(first, pltpu.roll(blk, LANES - IDX_DIM // 2, 1), pltpu.roll(blk, IDX_DIM // 2, 1))
        return blk * c + swapped * s

    def hi_lo(x):
        hi = x.astype(bf16).astype(f32)
        return hi, x - hi

    for k in range(IDX_W // LANES):
        hi, lo = hi_lo(rope(x_ref[:, k * LANES:(k + 1) * LANES]))
        hi_sw = pltpu.roll(hi, IDX_DIM, 1)
        lo_sw = pltpu.roll(lo, IDX_DIM, 1)
        base = 2 * k * IDX_K
        q_ref[:, base:base + LANES] = jnp.where(low, hi, hi_sw).astype(bf16)
        q_ref[:, base + LANES:base + IDX_K] = jnp.where(low, lo, 0.0).astype(bf16)
        q_ref[:, base + IDX_K:base + IDX_K + LANES] = jnp.where(low, hi_sw, hi).astype(bf16)
        q_ref[:, base + IDX_K + LANES:base + 2 * IDX_K] = jnp.where(low, lo_sw, 0.0).astype(bf16)
    kw = x_ref[:, IDX_W:IDX_RAW]
    hi, lo = hi_lo(rope(kw))
    k_ref[:, 0:LANES] = jnp.where(low, hi, pltpu.roll(lo, IDX_DIM, 1)).astype(bf16)
    k_ref[:, LANES:IDX_K] = jnp.where(low, hi, 0.0).astype(bf16)
    w_ref[...] = kw


def _idx_rope(raw, c_tab, s_tab):
    m = raw.shape[0]
    tm = min(m, 512)
    tab = pl.BlockSpec((tm, LANES), lambda i: (i, 0))
    return pl.pallas_call(
        _idx_rope_kernel,
        grid=(m // tm,),
        in_specs=[pl.BlockSpec((tm, IDX_RAW), lambda i: (i, 0)), tab, tab],
        out_specs=[pl.BlockSpec((tm, IDX_HEADS * IDX_K), lambda i: (i, 0)),
                   pl.BlockSpec((tm, IDX_K), lambda i: (i, 0)), tab],
        out_shape=[jax.ShapeDtypeStruct((m, IDX_HEADS * IDX_K), bf16), jax.ShapeDtypeStruct((m, IDX_K), bf16),
                   jax.ShapeDtypeStruct((m, LANES), f32)],
        compiler_params=_cparams(("parallel",), 8 * tm * IDX_RAW * 4 + 4 * tm * IDX_HEADS * IDX_K * 2),
        name="idx_rope",
    )(raw, c_tab, s_tab)


INT_MIN = -2 ** 31


DSA_T = 256
DSA_KG = 4
MASKED = -1e30


def _key_group(nkb):
    return DSA_KG if nkb % DSA_KG == 0 else 1


def _dsa_index_kernel(q_ref, k_ref, qw_ref, m_ref, key_scr, *, topk, scale):
    i = pl.program_id(1)
    nkb, kb, tq = key_scr.shape
    qw_t = qw_ref[...].T
    q_chunk = (i * tq + lax.broadcasted_iota(i32, (1, tq), 1)) // CHUNK
    krow = lax.broadcasted_iota(i32, (kb, tq), 0)

    def score_block(j, c):
        kblk = k_ref[pl.ds(pl.multiple_of(j * kb, kb), kb), :]
        score = jnp.zeros((kb, tq), f32)
        for hd in range(IDX_HEADS):
            rel = lax.dot_general(kblk, q_ref[:, hd * IDX_K:(hd + 1) * IDX_K], _NT, preferred_element_type=f32)
            score = score + jnp.maximum(rel, 0.0) * qw_t[IDX_DIM + hd:IDX_DIM + hd + 1, :]
        score = score * scale
        adm = j * kb + krow < (q_chunk + 1) * CHUNK
        bits = pltpu.bitcast(score, i32)
        key = jnp.where(bits < 0, bits ^ 0x7FFFFFFF, bits)
        key_scr[j] = jnp.where(adm, key, INT_MIN)
        return c

    lax.fori_loop(0, i + 1, score_block, 0)

    kg = _key_group(nkb)
    ngroups = (i + kg) // kg

    def fill(j, c):
        key_scr[j] = jnp.full((kb, tq), INT_MIN, i32)
        return c

    lax.fori_loop(i + 1, ngroups * kg, fill, 0)

    def count_ge(cand):
        def grp(g, acc):
            for u in range(kg):
                ge = (key_scr[g * kg + u] >= cand).astype(f32)
                for r in range(kb // SUBLANES):
                    acc = acc + ge[r * SUBLANES:(r + 1) * SUBLANES, :]
            return acc

        acc = lax.fori_loop(0, ngroups, grp, jnp.zeros((SUBLANES, tq), f32))
        return jnp.sum(acc, axis=0, keepdims=True)

    kf = jnp.float32(topk)
    cur = jnp.where(count_ge(jnp.zeros((1, tq), i32)) >= kf, 0, INT_MIN).astype(i32)

    def bisect(it, cur):
        cand = cur + jnp.left_shift(jnp.int32(1), 30 - it)
        return jnp.where(count_ge(cand) >= kf, cand, cur)

    thr = jnp.maximum(lax.fori_loop(0, 31, bisect, cur), INT_MIN + 1)

    def write(j, c):
        m_ref[0, j] = jnp.where(key_scr[j] >= thr, 0.0, MASKED).T
        return c

    def clear(j, c):
        m_ref[0, j] = jnp.full((tq, kb), MASKED, f32)
        return c

    lax.fori_loop(0, i + 1, write, 0)
    lax.fori_loop(i + 1, nkb, clear, 0)


def _dsa_index(iq, ik, iw, batch, seq):
    tq = min(seq, DSA_T)
    nq = seq // tq
    topk = min(TOPK_MAX, seq // 4)
    scale = IDX_DIM ** -0.5 * IDX_HEADS ** -0.5
    return pl.pallas_call(
        functools.partial(_dsa_index_kernel, topk=topk, scale=scale),
        grid=(batch, nq),
        in_specs=[pl.BlockSpec((tq, IDX_HEADS * IDX_K), lambda b, i: (b * nq + i, 0)),
                  pl.BlockSpec((seq, IDX_K), lambda b, i: (b, 0)),
                  pl.BlockSpec((tq, LANES), lambda b, i: (b * nq + i, 0))],
        out_specs=pl.BlockSpec((1, nq, tq, tq), lambda b, i: (b * nq + i, 0, 0, 0)),
        out_shape=jax.ShapeDtypeStruct((batch * nq, nq, tq, tq), f32),
        scratch_shapes=[pltpu.VMEM((nq, tq, tq), i32)],
        compiler_params=_cparams(("parallel", "parallel"),
                                 8 * tq * seq * 4 + 4 * seq * IDX_K * 2 + 4 * tq * IDX_HEADS * IDX_K * 2),
        name="dsa_index",
    )(iq, ik, iw)


ATT_HG = 4


def _dsa_attn_kernel(q_ref, k_ref, v_ref, m_ref, o_ref):
    i = pl.program_id(2)
    _, nkb, tq, kb = m_ref.shape
    kg = _key_group(nkb)

    def group(g, carry):
        rows = pl.ds(pl.multiple_of(g * (kg * kb), kg * kb), kg * kb)
        parts = [m_ref[0, g * kg + u] for u in range(kg)]
        bias = jnp.concatenate(parts, axis=1) if kg > 1 else parts[0]
        out = []
        for h in range(ATT_HG):
            mx, den, acc = carry[h]
            sl = slice(h * HEAD_DIM, (h + 1) * HEAD_DIM)
            logits = lax.dot_general(q_ref[:, sl], k_ref[rows, sl], _NT, preferred_element_type=f32) + bias
            mx_new = jnp.maximum(mx, jnp.max(logits, axis=1, keepdims=True))
            alpha = jnp.exp(mx - mx_new)
            p = jnp.exp(logits - mx_new)
            den = alpha * den + jnp.sum(p, axis=1, keepdims=True)
            acc = alpha * acc + jnp.dot(p.astype(bf16), v_ref[rows, sl], preferred_element_type=f32)
            out.append((mx_new, den, acc))
        return tuple(out)

    init = tuple((jnp.full((tq, 1), MASKED, f32), jnp.zeros((tq, 1), f32), jnp.zeros((tq, HEAD_DIM), f32))
                 for _ in range(ATT_HG))
    final = lax.fori_loop(0, (i + kg) // kg, group, init)
    for h in range(ATT_HG):
        _, den, acc = final[h]
        o_ref[:, h * HEAD_DIM:(h + 1) * HEAD_DIM] = (acc / den).astype(o_ref.dtype)


def _dsa_attn(q, k, v, mask, batch, seq):
    m, w = q.shape
    _, nq, tq, _ = mask.shape
    gw = ATT_HG * HEAD_DIM
    qspec = pl.BlockSpec((tq, gw), lambda b, g, i: (b * nq + i, g))
    kvspec = pl.BlockSpec((seq, gw), lambda b, g, i: (b, g))
    vmem = 2 * (2 * tq * gw * 2 + 2 * seq * gw * 2 + tq * seq * 4) + 8 * tq * DSA_KG * tq * 4
    return pl.pallas_call(
        _dsa_attn_kernel,
        grid=(batch, w // gw, nq),
        in_specs=[qspec, kvspec, kvspec, pl.BlockSpec((1, nq, tq, tq), lambda b, g, i: (b * nq + i, 0, 0, 0))],
        out_specs=qspec,
        out_shape=jax.ShapeDtypeStruct((m, w), bf16),
        compiler_params=_cparams(("parallel", "parallel", "arbitrary"), vmem),
        name="dsa_attn",
    )(q, k, v, mask)


RET_HG = 4


def _retention_kernel(q_ref, k_ref, v_ref, g_ref, gn_ref, lg_ref, o_ref, state):
    c = pl.program_id(2)
    t = q_ref.shape[0]

    @pl.when(c == 0)
    def _init():
        state[...] = jnp.zeros_like(state)

    ri = lax.broadcasted_iota(i32, (t, t), 0)
    ci = lax.broadcasted_iota(i32, (t, t), 1)
    diff = (ri - ci).astype(f32)
    pos = lax.broadcasted_iota(i32, (t, 1), 0).astype(f32)
    for h in range(RET_HG):
        sl = slice(h * HEAD_DIM, (h + 1) * HEAD_DIM)
        lg = lg_ref[:, h * HEAD_DIM:h * HEAD_DIM + 1]
        q = q_ref[:, sl]
        k = k_ref[:, sl]
        v = v_ref[:, sl]
        decay = jnp.where(diff >= 0, jnp.exp(lg * jnp.maximum(diff, 0.0)), 0.0)
        scores = lax.dot_general(q, k, _NT, preferred_element_type=f32) * decay
        inner = jnp.dot(scores.astype(bf16), v, preferred_element_type=f32)
        st = state[h]
        cross = jnp.dot(q, st.astype(bf16), preferred_element_type=f32) * jnp.exp(lg * (pos + 1.0))
        kz = k.astype(f32) * jnp.exp(lg * (t - 1.0 - pos))
        kv = jnp.dot(kz.T.astype(bf16), v, preferred_element_type=f32)
        state[h] = jnp.exp(lg * t) * st + kv
        ret = inner + cross
        mu = jnp.mean(ret, axis=1, keepdims=True)
        rc = ret - mu
        var = jnp.mean(rc * rc, axis=1, keepdims=True)
        gate = g_ref[:, sl]
        out = rc * lax.rsqrt(var + LN_EPS) * gn_ref[:, sl] * (gate * _sigmoid(gate))
        o_ref[:, sl] = out.astype(o_ref.dtype)


def _retention(q, k, v, g, gn_g, batch, seq):
    m, w = q.shape
    heads = w // HEAD_DIM
    t = min(seq, 256)
    nc = seq // t
    gw = RET_HG * HEAD_DIM
    log_g = jnp.log1p(-jnp.exp2(-5.0 - jnp.arange(heads, dtype=f32)))
    lg = jnp.repeat(log_g, HEAD_DIM)[None, :]
    blk = pl.BlockSpec((t, gw), lambda b, gi, c: (b * nc + c, gi))
    vec = pl.BlockSpec((1, gw), lambda b, gi, c: (0, gi))
    return pl.pallas_call(
        _retention_kernel,
        grid=(batch, w // gw, nc),
        in_specs=[blk, blk, blk, blk, vec, vec],
        out_specs=blk,
        out_shape=jax.ShapeDtypeStruct((m, w), bf16),
        scratch_shapes=[pltpu.VMEM((RET_HG, HEAD_DIM, HEAD_DIM), f32)],
        compiler_params=_cparams(("parallel", "parallel", "arbitrary"), 16 * t * gw * 4 + 8 * t * t * 4),
        name="retention",
    )(q, k, v, g, gn_g[None, :], lg)


def _mixer_sparse_retention(h_hi, h_lo, pos2, w_in, w_out, gn_g, batch, seq):
    m, d = h_hi.shape
    gw = d // 2
    tm = min(m, 1024)
    tn = 512
    c128, s128 = _rope_tables(pos2, HEAD_DIM)
    c64, s64 = _rope_tables(pos2, IDX_DIM)
    w_t = jnp.swapaxes(w_in, 0, 1)
    rope = functools.partial(_proj_rope, h_hi, w_t, c_tab=c128, s_tab=s128, n=gw, tm=tm, tn=tn)
    plain = functools.partial(_mm, [h_hi], w_t, n=gw, tm=tm, tn=tn, transposed=True)
    aq = rope(col_off=0, scale=HEAD_DIM ** -0.5)
    ak = rope(col_off=gw, scale=1.0)
    av = plain(col_off=2 * gw, out_dtype=bf16)
    idx_raw = _mm3(h_hi, h_lo, w_t, col_off=3 * gw, n=IDX_RAW, tm=min(m, 512), tn=IDX_RAW // 3, transposed=True)
    iq, ik, iw = _idx_rope(idx_raw, c64, s64)
    mask = _dsa_index(iq, ik, iw, batch, seq)
    a_out = _dsa_attn(aq, ak, av, mask, batch, seq)
    b_off = 3 * gw + IDX_W + IDX_DIM + IDX_HEADS
    bq = rope(col_off=b_off, scale=1.0)
    bk = rope(col_off=b_off + gw, scale=HEAD_DIM ** -0.5)
    bv = plain(col_off=b_off + 2 * gw, out_dtype=bf16)
    bg = plain(col_off=b_off + 3 * gw, out_dtype=f32)
    b_out = _retention(bq, bk, bv, bg, gn_g, batch, seq)
    return _mm([a_out, b_out], w_out, col_off=0, n=d, tm=tm, tn=tn, out_dtype=f32)


MXU_N = 256


def _seg_sum64(x):
    r = lax.broadcasted_iota(i32, (MXU_N, MXU_N), 0) // C_HEAD_DIM
    c = lax.broadcasted_iota(i32, (MXU_N, MXU_N), 1) // C_HEAD_DIM
    ones = (r == c).astype(bf16)
    hi = x.astype(bf16)
    r1 = x - hi.astype(f32)
    mid = r1.astype(bf16)
    lo = (r1 - mid.astype(f32)).astype(bf16)
    cols = []
    for k in range(x.shape[1] // MXU_N):
        sl = slice(k * MXU_N, (k + 1) * MXU_N)
        s = jnp.dot(hi[:, sl], ones, preferred_element_type=f32)
        s += jnp.dot(mid[:, sl], ones, preferred_element_type=f32)
        s += jnp.dot(lo[:, sl], ones, preferred_element_type=f32)
        cols.append(s)
    return jnp.concatenate(cols, axis=1) if len(cols) > 1 else cols[0]


def _neg_softplus_neg(z):
    return jnp.minimum(z, 0.0) - jnp.log(1.0 + jnp.exp(-jnp.abs(z)))


def _rwkv_prep_kernel(p_ref, pp_ref, mu_ref, w0_ref, wup_ref, a0_ref, aup_ref, gup_ref, ka_ref, rk_ref,
                      r_o, w_o, k_o, a_o, v_o, g_o, bon_o, *, seq, gw):
    i = pl.program_id(0)
    tm = p_ref.shape[0]
    p = p_ref[...]
    prev_row = jnp.where((i * tm) % seq == 0, 0.0, pp_ref[SUBLANES - 1:SUBLANES, :])
    row = lax.broadcasted_iota(i32, (tm, 1), 0)
    shifted = jnp.where(row == 0, prev_row, pltpu.roll(p, 1, 0))
    pm = p + (shifted - p) * mu_ref[...]
    r = pm[:, 0:gw]
    k = pm[:, gw:2 * gw]
    v = pm[:, 2 * gw:3 * gw]
    o = 3 * gw
    dw = pm[:, o:o + C_DECAY_RANK]
    da = pm[:, o + C_DECAY_RANK:o + C_DECAY_RANK + C_ICLR_RANK]
    dg = pm[:, o + C_DECAY_RANK + C_ICLR_RANK:]
    w_log = _neg_softplus_neg(w0_ref[...] + _dot3(jnp.tanh(dw), wup_ref[...])) - 0.5
    decay = jnp.exp(-jnp.exp(w_log))
    a = _sigmoid(a0_ref[...] + _dot3(da, aup_ref[...]))
    g = _dot3(_sigmoid(dg), gup_ref[...])
    k2 = k * (1.0 + (a - 1.0) * ka_ref[...])
    r_o[...] = r
    w_o[...] = decay
    k_o[...] = k
    a_o[...] = a
    v_o[...] = v
    g_o[...] = g
    bon_o[...] = _seg_sum64(r * k2 * rk_ref[...]) * v


def _rwkv_prep(pc, mu, w0, w_up, a0, a_up, g_up, k_a, r_k, seq):
    m, cc = pc.shape
    gw = w0.shape[0]
    tm = 128
    nsub = tm // SUBLANES
    vec = pl.BlockSpec((1, gw), lambda i: (0, 0))
    out = pl.BlockSpec((tm, gw), lambda i: (i, 0))
    full = lambda a: pl.BlockSpec(a.shape, lambda i: (0, 0))
    return pl.pallas_call(
        functools.partial(_rwkv_prep_kernel, seq=seq, gw=gw),
        grid=(m // tm,),
        in_specs=[pl.BlockSpec((tm, cc), lambda i: (i, 0)),
                  pl.BlockSpec((SUBLANES, cc), lambda i: (jnp.maximum(i * nsub - 1, 0), 0)),
                  pl.BlockSpec((1, cc), lambda i: (0, 0)),
                  vec, full(w_up), vec, full(a_up), full(g_up), vec, vec],
        out_specs=[out] * 7,
        out_shape=[jax.ShapeDtypeStruct((m, gw), f32)] * 7,
        compiler_params=_cparams(("parallel",), 2 * tm * cc * 4 + 2 * 7 * tm * gw * 4 + 16 * tm * gw * 4),
        name="rwkv_prep",
    )(pc, pc, mu[None, :], w0[None, :], w_up, a0[None, :], a_up, g_up, k_a[None, :], r_k.reshape(1, -1))


def _rwkv_scan_kernel(w_ref, a_ref, k_ref, r_ref, v_ref, kkw_ref, kaw_ref, y_ref, z_ref,
                      w_s, kk_s, ka_s, k_s, r_s):
    c = pl.program_id(0)
    steps = w_ref.shape[0]
    nslab = z_ref.shape[0]

    @pl.when(c == 0)
    def _init():
        z_ref[...] = jnp.zeros_like(z_ref)

    def both_halves(x):
        return jnp.concatenate([x, x], axis=1)

    kkw = both_halves(kkw_ref[...])
    kaw = both_halves(kaw_ref[...])

    def derive(t, carry):
        a = both_halves(a_ref[t])
        kraw = both_halves(k_ref[t])
        kk = kraw * kkw
        kk = kk * lax.rsqrt(jnp.maximum(jnp.sum(kk * kk, axis=0, keepdims=True), 1e-24))
        w_s[t] = both_halves(w_ref[t])
        r_s[t] = both_halves(r_ref[t])
        kk_s[t] = kk
        ka_s[t] = kk * a
        k_s[t] = kraw * (1.0 + (a - 1.0) * kaw)
        return carry

    lax.fori_loop(0, steps, derive, 0, unroll=4)

    def step(t, carry):
        w = w_s[t]
        kk = kk_s[t]
        ka = ka_s[t]
        k = k_s[t]
        r = r_s[t]
        vrows = v_ref[t]
        for s in range(nslab):
            z = z_ref[s]
            sk = jnp.sum(z * kk, axis=0, keepdims=True)
            zn = z * w - ka * sk + k * vrows[s:s + 1, :]
            z_ref[s] = zn
            y_ref[t, s:s + 1, :] = jnp.sum(zn * r, axis=0, keepdims=True)
        return carry

    lax.fori_loop(0, steps, step, 0)


def _to_scan_cols(a, batch, seq):
    heads = a.shape[1] // C_HEAD_DIM
    return a.reshape(batch, seq, heads, C_HEAD_DIM).transpose(1, 3, 0, 2).reshape(seq, C_HEAD_DIM, batch * heads)


def _rwkv_scan(w, a, k, r, v, k_k, k_a, batch, seq):
    heads = w.shape[1] // C_HEAD_DIM
    half = batch * heads
    assert 2 * half == LANES
    nslab = C_HEAD_DIM // 2
    cols = [_to_scan_cols(x, batch, seq) for x in (w, a, k, r)]
    consts = [jnp.tile(p.reshape(heads, C_HEAD_DIM).T, (1, batch)) for p in (k_k, k_a)]
    vr = v.reshape(batch, seq, heads, nslab, 2).transpose(1, 3, 4, 0, 2).reshape(seq, nslab, LANES)
    steps = min(seq, 64)
    col_spec = pl.BlockSpec((steps, C_HEAD_DIM, half), lambda c: (c, 0, 0))
    row_spec = pl.BlockSpec((steps, nslab, LANES), lambda c: (c, 0, 0))
    const_spec = pl.BlockSpec((C_HEAD_DIM, half), lambda c: (0, 0))
    y = pl.pallas_call(
        _rwkv_scan_kernel,
        grid=(seq // steps,),
        in_specs=[col_spec] * 4 + [row_spec, const_spec, const_spec],
        out_specs=row_spec,
        out_shape=jax.ShapeDtypeStruct((seq, nslab, LANES), f32),
        scratch_shapes=[pltpu.VMEM((nslab, C_HEAD_DIM, LANES), f32)]
        + [pltpu.VMEM((steps, C_HEAD_DIM, LANES), f32)] * 5,
        compiler_params=_cparams(("arbitrary",), (2 * 4 + 5) * steps * C_HEAD_DIM * LANES * 4 + 6 * steps * nslab * LANES * 4),
        name="rwkv_scan",
    )(*cols, vr, *consts)
    return y.reshape(seq, nslab, 2, batch, heads).transpose(3, 0, 4, 1, 2).reshape(batch * seq, heads * C_HEAD_DIM)


def _rwkv_post_kernel(y_ref, g_ref, bon_ref, lng_ref, lnb_ref, o_ref):
    y = y_ref[...]
    mu = _seg_sum64(y) * (1.0 / C_HEAD_DIM)
    yc = y - mu
    var = _seg_sum64(yc * yc) * (1.0 / C_HEAD_DIM)
    yn = yc * lax.rsqrt(var + C_EPS) * lng_ref[...] + lnb_ref[...]
    o_ref[...] = ((yn + bon_ref[...]) * g_ref[...]).astype(o_ref.dtype)


def _rwkv_post(y, g, bonus, ln_g, ln_b):
    m, gw = y.shape
    tm = 256
    blk = pl.BlockSpec((tm, gw), lambda i: (i, 0))
    vec = pl.BlockSpec((1, gw), lambda i: (0, 0))
    return pl.pallas_call(
        _rwkv_post_kernel,
        grid=(m // tm,),
        in_specs=[blk, blk, blk, vec, vec],
        out_specs=blk,
        out_shape=jax.ShapeDtypeStruct((m, gw), bf16),
        compiler_params=_cparams(("parallel",), 16 * tm * gw * 4),
        name="rwkv_post",
    )(y, g, bonus, ln_g[None, :], ln_b[None, :])


GELU_C = 0.7978845608028654


def _lru_kernel(px_ref, pg_ref, cw_ref, cb_ref, wa_ref, ba_ref, wx_ref, bx_ref, lam_ref, o_ref,
                tail, hcar, a_s, b_s):
    c = pl.program_id(1)
    t = px_ref.shape[0]

    @pl.when(c == 0)
    def _init():
        tail[...] = jnp.zeros_like(tail)
        hcar[...] = jnp.zeros_like(hcar)

    x = px_ref[...]
    ext = jnp.concatenate([tail[...], x], axis=0)
    xc = cb_ref[...]
    for j in range(D_CONV):
        off = SUBLANES - (D_CONV - 1) + j
        xc = xc + cw_ref[j:j + 1, :] * ext[off:off + t, :]
    tail[...] = x[t - SUBLANES:, :]
    lam = lam_ref[...]
    sp = jnp.maximum(-lam, 0.0) + jnp.log(1.0 + jnp.exp(-jnp.abs(lam)))
    bw = wa_ref.shape[1]
    for n in range(wa_ref.shape[0]):
        sl = slice(n * bw, (n + 1) * bw)
        xb = xc[:, sl]
        rg = _sigmoid(_dot3(xb, wa_ref[n]) + ba_ref[:, sl])
        ig = _sigmoid(_dot3(xb, wx_ref[n]) + bx_ref[:, sl])
        log_a = -LRU_C * rg * sp[:, sl]
        a_s[:, sl] = jnp.exp(log_a)
        th = jnp.tanh(log_a)
        one_minus_a2 = -2.0 * th / (1.0 - th)
        b_s[:, sl] = jnp.sqrt(one_minus_a2) * (ig * xb)

    sub = lax.broadcasted_iota(i32, (SUBLANES, a_s.shape[1]), 0)

    def rows8(g, h):
        r0 = pl.multiple_of(g * SUBLANES, SUBLANES)
        a = a_s[pl.ds(r0, SUBLANES), :]
        b = b_s[pl.ds(r0, SUBLANES), :]
        for s in (1, 2, 4):
            b = a * jnp.where(sub >= s, pltpu.roll(b, s, 0), 0.0) + b
            a = a * jnp.where(sub >= s, pltpu.roll(a, s, 0), 1.0)
        hs = a * h + b
        b_s[pl.ds(r0, SUBLANES), :] = hs
        return hs[SUBLANES - 1:SUBLANES, :]

    hcar[...] = lax.fori_loop(0, t // SUBLANES, rows8, hcar[...])
    gate = pg_ref[...]
    gelu = 0.5 * gate * (1.0 + jnp.tanh(GELU_C * (gate + 0.044715 * (gate * gate * gate))))
    o_ref[...] = (b_s[...] * gelu).astype(o_ref.dtype)


def _lru(px, pg, conv_w, conv_b, w_a, b_a, w_x, b_x, lam, batch, seq):
    m, w = px.shape
    t = min(seq, 256)
    nc = seq // t
    blk = pl.BlockSpec((t, w), lambda b, c: (b * nc + c, 0))
    vec = pl.BlockSpec((1, w), lambda b, c: (0, 0))
    wblk = pl.BlockSpec(w_a.shape, lambda b, c: (0, 0, 0))
    return pl.pallas_call(
        _lru_kernel,
        grid=(batch, nc),
        in_specs=[blk, blk, pl.BlockSpec((D_CONV, w), lambda b, c: (0, 0)), vec, wblk, vec, wblk, vec, vec],
        out_specs=blk,
        out_shape=jax.ShapeDtypeStruct((m, w), bf16),
        scratch_shapes=[pltpu.VMEM((SUBLANES, w), f32), pltpu.VMEM((1, w), f32),
                        pltpu.VMEM((t, w), f32), pltpu.VMEM((t, w), f32)],
        compiler_params=_cparams(("parallel", "arbitrary"), 16 * t * w * 4),
        name="lru",
    )(px, pg, conv_w, conv_b[None, :], w_a, b_a[None, :], w_x, b_x[None, :], lam[None, :])


def _mixer_rwkv_lru(h_hi, w_in, w_out, mu, w0, w_up, a0, a_up, g_up, k_k, k_a, r_k, ln_g, ln_b,
                    conv_w, conv_b, w_a, b_a, w_x, b_x, lam, batch, seq):
    m, d = h_hi.shape
    gw = d // 2
    c_cols = 3 * gw + C_DECAY_RANK + C_ICLR_RANK + C_GATE_RANK
    tm = min(m, 1024)
    tn = 512
    proj = functools.partial(_mm, [h_hi], w_in, tm=tm, tn=tn, out_dtype=f32)
    pc = proj(col_off=0, n=c_cols)
    pg = proj(col_off=c_cols, n=gw)
    px = proj(col_off=c_cols + gw, n=gw)
    r, w, k, a, v, g, bonus = _rwkv_prep(pc, mu, w0, w_up, a0, a_up, g_up, k_a, r_k, seq)
    y = _rwkv_scan(w, a, k, r, v, k_k, k_a, batch, seq)
    c_out = _rwkv_post(y, g, bonus, ln_g, ln_b)
    d_out = _lru(px, pg, conv_w, conv_b, w_a, b_a, w_x, b_x, lam, batch, seq)
    return _mm([c_out, d_out], w_out, col_off=0, n=d, tm=tm, tn=tn, out_dtype=f32)


ADA_ROWS = 16


def _ada_kernel(c_ref, w_ref, b_ref, o_ref):
    c = c_ref[...]
    s = (c * _sigmoid(c)).astype(bf16)
    o_ref[...] = jnp.dot(s, w_ref[...].astype(bf16), preferred_element_type=f32) + b_ref[...]


def _ada(c, ada_w, ada_b):
    batch, d = c.shape
    n = ada_w.shape[1]
    tn = 512
    cp = jnp.zeros((ADA_ROWS, d), f32).at[:batch].set(c)
    out = pl.pallas_call(
        _ada_kernel,
        grid=(n // tn,),
        in_specs=[pl.BlockSpec((ADA_ROWS, d), lambda j: (0, 0)),
                  pl.BlockSpec((d, tn), lambda j: (0, j)),
                  pl.BlockSpec((1, tn), lambda j: (0, j))],
        out_specs=pl.BlockSpec((ADA_ROWS, tn), lambda j: (0, j)),
        out_shape=jax.ShapeDtypeStruct((ADA_ROWS, n), f32),
        compiler_params=_cparams(("parallel",), 3 * d * tn * 4),
        name="ada",
    )(cp, ada_w, ada_b[None, :])
    return out[:batch]


def kernel(x, c, positions, ada_w, ada_b, ada_table, ln_g, ln_b, ab_w_in, ab_w_out, ret_gn_g, cd_w_in, cd_w_out, rwkv_mu, rwkv_w0, rwkv_w_up, rwkv_a0, rwkv_a_up, rwkv_g_up, rwkv_k_k, rwkv_k_a, rwkv_r_k, rwkv_ln_g, rwkv_ln_b, lru_conv_w, lru_conv_b, lru_w_a, lru_b_a, lru_w_x, lru_b_x, lru_lambda, moe_w_grp, moe_b_grp, moe_w_exp, moe_b_exp, moe_w_gate, moe_w_up, moe_w_down):
    batch, seq, d = x.shape
    x2 = x.reshape(batch * seq, d)
    pos2 = positions.reshape(batch * seq, 1)
    ada = _ada(c, ada_w, ada_b).reshape(batch, 6, 1, d)
    for layer in range(DEPTH):
        mod = ada + ada_table[layer][None, :, None, :]
        shift_m, scale_m, gate_m, shift_f, scale_f, gate_f = (mod[:, i] for i in range(6))
        h_hi, h_lo = _modcast(x2, scale_m, shift_m, seq)
        j = layer // 2
        if layer % 2 == 0:
            y = _mixer_sparse_retention(h_hi, h_lo, pos2, ab_w_in[j], ab_w_out[j], ret_gn_g[j], batch, seq)
        else:
            y = _mixer_rwkv_lru(h_hi, cd_w_in[j], cd_w_out[j], rwkv_mu[j], rwkv_w0[j], rwkv_w_up[j],
                                rwkv_a0[j], rwkv_a_up[j], rwkv_g_up[j], rwkv_k_k[j], rwkv_k_a[j],
                                rwkv_r_k[j], rwkv_ln_g[j], rwkv_ln_b[j], lru_conv_w[j], lru_conv_b[j],
                                lru_w_a[j], lru_b_a[j], lru_w_x[j], lru_b_x[j], lru_lambda[j], batch, seq)
        x2 = _moe_layer(x2, y, gate_m, ln_g[layer, 0], ln_b[layer, 0], scale_f, shift_f, gate_f,
                        ln_g[layer, 1], ln_b[layer, 1], moe_w_grp[layer], moe_b_grp[layer],
                        moe_w_exp[layer], moe_b_exp[layer], moe_w_gate, moe_w_up, moe_w_down, layer, seq)
    return x2.reshape(batch, seq, d)
```

```python
import functools

import jax
import jax.numpy as jnp
from jax import lax
from jax.experimental import pallas as pl
from jax.experimental.pallas import tpu as pltpu

f32 = jnp.float32
bf16 = jnp.bfloat16
i32 = jnp.int32

DEPTH = 2
CHUNK = 64
ROPE_THETA = 10000.0
LN_EPS = 1e-5
ALPHA = (2 * DEPTH) ** 0.25
HEAD_DIM = 128
IDX_HEADS = 16
IDX_DIM = 64
TOPK_MAX = 256
C_HEAD_DIM = 64
C_DECAY_RANK = 128
C_ICLR_RANK = 128
C_GATE_RANK = 256
C_EPS = 64e-5
D_BLOCKS = 16
D_CONV = 4
LRU_C = 8.0
N_GROUPS = 4
EXPERTS_PER_GROUP = 8
N_EXPERTS = N_GROUPS * EXPERTS_PER_GROUP
D_EXPERT = 512

LANES = 128
SUBLANES = 8
VMEM_BYTES_V7X = 64 * 1024 * 1024
VMEM_HEADROOM = 8 * 1024 * 1024


def _cparams(semantics, vmem_bytes):
    limit = min(int(vmem_bytes) + VMEM_HEADROOM, VMEM_BYTES_V7X - VMEM_HEADROOM)
    return pltpu.CompilerParams(dimension_semantics=semantics, vmem_limit_bytes=limit)


def _split_bf16(a):
    hi = a.astype(bf16)
    lo = (a - hi.astype(f32)).astype(bf16)
    return hi, lo


def _dot3(a, b, dims=(((1,), (0,)), ((), ()))):
    ah, al = _split_bf16(a)
    bh, bl = _split_bf16(b)
    dg = functools.partial(lax.dot_general, dimension_numbers=dims, preferred_element_type=f32)
    return dg(ah, bh) + dg(ah, bl) + dg(al, bh)


def _sigmoid(x):
    return 1.0 / (1.0 + jnp.exp(-x))


def _modcast_kernel(x_ref, sc_ref, sh_ref, hi_ref, lo_ref):
    h = x_ref[...] * (1.0 + sc_ref[0]) + sh_ref[0]
    hi, lo = _split_bf16(h)
    hi_ref[...] = hi
    lo_ref[...] = lo


def _modcast(x2, sc, sh, seq):
    m, k = x2.shape
    tm = 256
    row = pl.BlockSpec((tm, k), lambda i: (i, 0))
    per_batch = pl.BlockSpec((1, 1, k), lambda i: (i * tm // seq, 0, 0))
    return pl.pallas_call(
        _modcast_kernel,
        grid=(m // tm,),
        in_specs=[row, per_batch, per_batch],
        out_specs=[row, row],
        out_shape=[jax.ShapeDtypeStruct((m, k), bf16)] * 2,
        compiler_params=_cparams(("parallel",), 2 * tm * k * (4 + 2 + 2)),
        name="modcast",
    )(x2, sc, sh)


_NN = (((1,), (0,)), ((), ()))
_NT = (((1,), (1,)), ((), ()))


def _mm_kernel(*refs, nparts, epilogue, dims):
    acc = None
    for p in range(nparts):
        d = lax.dot_general(refs[p][...], refs[nparts + p][...].astype(bf16), dims, preferred_element_type=f32)
        acc = d if acc is None else acc + d
    epilogue(acc, *refs[2 * nparts:])


def _mm3_kernel(xh_ref, xl_ref, w_ref, o_ref, *, dims):
    wh, wl = _split_bf16(w_ref[...])
    xh = xh_ref[...]
    dg = functools.partial(lax.dot_general, dimension_numbers=dims, preferred_element_type=f32)
    o_ref[...] = dg(xh, wh) + dg(xh, wl) + dg(xl_ref[...], wh)


def _w_spec(kp, tn, k_off, col_off, transposed):
    if transposed:
        assert col_off % SUBLANES == 0 and tn % SUBLANES == 0
        return pl.BlockSpec((pl.Element(tn), pl.Element(kp)),
                            lambda i, j: (pl.multiple_of(col_off + j * tn, SUBLANES), k_off))
    assert col_off % tn == 0 and k_off % kp == 0
    return pl.BlockSpec((kp, tn), lambda i, j: (k_off // kp, col_off // tn + j))


def _store_epilogue(acc, o_ref):
    o_ref[...] = acc.astype(o_ref.dtype)


def _rope_epilogue(acc, c_ref, s_ref, o_ref, *, scale):
    c = c_ref[...]
    s = s_ref[...]
    for k in range(acc.shape[1] // LANES):
        blk = acc[:, k * LANES:(k + 1) * LANES]
        rot = blk * c + pltpu.roll(blk, LANES // 2, 1) * s
        if scale != 1.0:
            rot = rot * scale
        o_ref[:, k * LANES:(k + 1) * LANES] = rot.astype(o_ref.dtype)


def _mm(xs, w, *, col_off, n, tm, tn, out_dtype, epilogue=_store_epilogue, extra=(), extra_specs=(),
        transposed=False):
    m = xs[0].shape[0]
    kp = xs[0].shape[1]
    nparts = len(xs)
    assert all(x.shape == (m, kp) for x in xs) and w.shape[1 if transposed else 0] == nparts * kp
    assert m % tm == 0 and n % tn == 0
    x_specs = [pl.BlockSpec((tm, kp), lambda i, j: (i, 0)) for _ in xs]
    w_specs = [_w_spec(kp, tn, p * kp, col_off, transposed) for p in range(nparts)]
    vmem = 2 * nparts * (tm * kp * 2 + kp * tn * 4) + nparts * kp * tn * 2 + 4 * tm * tn * 4
    return pl.pallas_call(
        functools.partial(_mm_kernel, nparts=nparts, epilogue=epilogue, dims=_NT if transposed else _NN),
        grid=(m // tm, n // tn),
        in_specs=x_specs + w_specs + list(extra_specs),
        out_specs=pl.BlockSpec((tm, tn), lambda i, j: (i, j)),
        out_shape=jax.ShapeDtypeStruct((m, n), out_dtype),
        compiler_params=_cparams(("parallel", "arbitrary"), vmem),
        name="mm",
    )(*xs, *([w] * nparts), *extra)


def _mm3(xh, xl, w, *, col_off, n, tm, tn, transposed=False):
    m, k = xh.shape
    assert m % tm == 0 and n % tn == 0
    xspec = pl.BlockSpec((tm, k), lambda i, j: (i, 0))
    vmem = 2 * (2 * tm * k * 2 + k * tn * 4) + 2 * k * tn * 2 + 4 * tm * tn * 4
    return pl.pallas_call(
        functools.partial(_mm3_kernel, dims=_NT if transposed else _NN),
        grid=(m // tm, n // tn),
        in_specs=[xspec, xspec, _w_spec(k, tn, 0, col_off, transposed)],
        out_specs=pl.BlockSpec((tm, tn), lambda i, j: (i, j)),
        out_shape=jax.ShapeDtypeStruct((m, n), f32),
        compiler_params=_cparams(("parallel", "arbitrary"), vmem),
        name="mm3",
    )(xh, xl, w)


def _res_ln(x, y, gate, g, b):
    z = ALPHA * x + (1.0 + gate) * y
    mu = jnp.mean(z, axis=-1, keepdims=True)
    zc = z - mu
    var = jnp.mean(zc * zc, axis=-1, keepdims=True)
    return zc * lax.rsqrt(var + LN_EPS) * g + b


def _ln_router_kernel(x_ref, y_ref, gate_ref, g_ref, b_ref, sc_ref, sh_ref, wr_ref, br_ref,
                      xo_ref, h_ref, rw_ref, rid_ref):
    xn = _res_ln(x_ref[...], y_ref[...], gate_ref[0], g_ref[...], b_ref[...])
    xo_ref[...] = xn
    h = xn * (1.0 + sc_ref[0]) + sh_ref[0]
    h_ref[...] = h
    logits = _dot3(h, wr_ref[...]) + br_ref[...]
    lane = lax.broadcasted_iota(i32, logits.shape, 1)
    neg = jnp.float32(-jnp.inf)
    is_grp = lane < N_GROUPS
    gl = jnp.where(is_grp, logits, neg)
    gmax = jnp.max(gl, axis=1, keepdims=True)
    gidx = jnp.min(jnp.where(gl == gmax, lane, LANES), axis=1, keepdims=True)
    gsum = jnp.sum(jnp.where(is_grp, jnp.exp(gl - gmax), 0.0), axis=1, keepdims=True)
    grp_p = 1.0 / gsum
    lo = N_GROUPS + EXPERTS_PER_GROUP * gidx
    in_grp = jnp.logical_and(lane >= lo, lane < lo + EXPERTS_PER_GROUP)
    el = jnp.where(in_grp, logits, neg)
    m1 = jnp.max(el, axis=1, keepdims=True)
    i1 = jnp.min(jnp.where(el == m1, lane, LANES), axis=1, keepdims=True)
    el2 = jnp.where(lane == i1, neg, el)
    m2 = jnp.max(el2, axis=1, keepdims=True)
    i2 = jnp.min(jnp.where(el2 == m2, lane, LANES), axis=1, keepdims=True)
    e2 = jnp.exp(m2 - m1)
    w1 = grp_p / (1.0 + e2)
    w2 = grp_p * e2 / (1.0 + e2)
    rw_ref[...] = jnp.where(lane == 0, w1, jnp.where(lane == 1, w2, 0.0))
    rid_ref[...] = jnp.where(lane == 0, i1 - N_GROUPS, jnp.where(lane == 1, i2 - N_GROUPS, 0))


def _ln_router(x2, y2, gate, g, b, sc, sh, wr, br, seq):
    m, d = x2.shape
    tm = 256
    row = pl.BlockSpec((tm, d), lambda i: (i, 0))
    per_batch = pl.BlockSpec((1, 1, d), lambda i: (i * tm // seq, 0, 0))
    vec = pl.BlockSpec((1, d), lambda i: (0, 0))
    small = pl.BlockSpec((tm, LANES), lambda i: (i, 0))
    return pl.pallas_call(
        _ln_router_kernel,
        grid=(m // tm,),
        in_specs=[row, row, per_batch, vec, vec, per_batch, per_batch,
                  pl.BlockSpec((d, LANES), lambda i: (0, 0)), pl.BlockSpec((1, LANES), lambda i: (0, 0))],
        out_specs=[row, row, small, small],
        out_shape=[jax.ShapeDtypeStruct((m, d), f32), jax.ShapeDtypeStruct((m, d), f32),
                   jax.ShapeDtypeStruct((m, LANES), f32), jax.ShapeDtypeStruct((m, LANES), i32)],
        compiler_params=_cparams(("parallel",), 12 * tm * d * 4 + 2 * d * LANES * 4),
        name="ln_router",
    )(x2, y2, gate, g, b, sc, sh, wr, br)


MOE_TM = 256
MOE_HALF = D_EXPERT // 2
DMA_UNROLL = 8


def _row_copy(src_hbm, row, dst_vmem, r, sem):
    return pltpu.make_async_copy(src_hbm.at[pl.ds(row, 1), :], dst_vmem.at[pl.ds(r, 1), :], sem)


def _gather_start(idx_ref, base, src_hbm, dst_vmem, sem, nrows, stride=1):
    def issue(r, c):
        _row_copy(src_hbm, idx_ref[base + r * stride], dst_vmem, r, sem).start()
        return c

    lax.fori_loop(0, nrows, issue, 0, unroll=DMA_UNROLL)


def _gather_wait(src_hbm, dst_vmem, sem, nrows):
    def wait(r, c):
        _row_copy(src_hbm, 0, dst_vmem, r, sem).wait()
        return c

    lax.fori_loop(0, nrows, wait, 0, unroll=DMA_UNROLL)


MOE_NHALF = D_EXPERT // MOE_HALF


def _moe_ffn_kernel(te_ref, tv_ref, tok_ref, h_hbm, wg_hbm, wu_hbm, wd_hbm, o_ref,
                    xbuf, wgb, wub, wdb, xsems, wsems, *, layer):
    i = pl.program_id(0)
    n = pl.num_programs(0)
    slot = i % 2
    valid = tv_ref[i] > 0
    nxt = jnp.minimum(i + 1, n - 1)
    has_next = jnp.logical_and(i + 1 < n, tv_ref[nxt] > 0)
    e = te_ref[i]
    next_e = te_ref[nxt]
    first = jnp.logical_or(i == 0, e != te_ref[jnp.maximum(i - 1, 0)])
    last = jnp.logical_and(has_next, next_e != e)

    def gather(tile, s):
        _gather_start(tok_ref, tile * MOE_TM, h_hbm, xbuf.at[s], xsems.at[s], MOE_TM)

    def weight_copies(expert, hf):
        cs = pl.ds(hf * MOE_HALF, MOE_HALF)
        return (pltpu.make_async_copy(wg_hbm.at[layer, expert, :, cs], wgb.at[hf], wsems.at[hf, 0]),
                pltpu.make_async_copy(wu_hbm.at[layer, expert, :, cs], wub.at[hf], wsems.at[hf, 1]),
                pltpu.make_async_copy(wd_hbm.at[layer, expert, cs, :], wdb.at[hf], wsems.at[hf, 2]))

    @pl.when(jnp.logical_and(i == 0, valid))
    def _prime():
        gather(0, 0)
        for hf in range(MOE_NHALF):
            for cp in weight_copies(e, hf):
                cp.start()

    def gather_next_part(part, nparts):
        rows = MOE_TM // nparts
        for r in range(part * rows, (part + 1) * rows):
            _row_copy(h_hbm, tok_ref[nxt * MOE_TM + r], xbuf.at[1 - slot], r, xsems.at[1 - slot]).start()

    @pl.when(valid)
    def _compute():
        _gather_wait(h_hbm, xbuf.at[slot], xsems.at[slot], MOE_TM)
        xb = xbuf[slot].astype(bf16)
        y = None
        for hf in range(MOE_NHALF):
            @pl.when(first)
            def _arrived(hf=hf):
                for cp in weight_copies(e, hf):
                    cp.wait()

            gather_next_part(2 * hf, 2 * MOE_NHALF)
            gate = jnp.dot(xb, wgb[hf].astype(bf16), preferred_element_type=f32)
            up = jnp.dot(xb, wub[hf].astype(bf16), preferred_element_type=f32)
            hid = gate * _sigmoid(gate) * up
            gather_next_part(2 * hf + 1, 2 * MOE_NHALF)
            part = jnp.dot(hid.astype(bf16), wdb[hf].astype(bf16), preferred_element_type=f32)
            y = part if y is None else y + part

            @pl.when(last)
            def _reload(hf=hf):
                for cp in weight_copies(next_e, hf):
                    cp.start()

        o_ref[...] = y

        @pl.when(jnp.logical_not(has_next))
        def _drain():
            _gather_wait(h_hbm, xbuf.at[1 - slot], xsems.at[1 - slot], MOE_TM)

    @pl.when(jnp.logical_not(valid))
    def _empty():
        o_ref[...] = jnp.zeros_like(o_ref)


def _moe_ffn(h2, w_gate, w_up, w_down, layer, tile_e, tile_valid, row_tok, n_tiles):
    n, d = h2.shape
    tm = MOE_TM
    anyspace = pl.BlockSpec(memory_space=pl.ANY)
    grid_spec = pltpu.PrefetchScalarGridSpec(
        num_scalar_prefetch=3,
        grid=(n_tiles,),
        in_specs=[anyspace, anyspace, anyspace, anyspace],
        out_specs=pl.BlockSpec((tm, d), lambda i, te, tv, tok: (i, 0)),
        scratch_shapes=[pltpu.VMEM((2, tm, d), f32),
                        pltpu.VMEM((MOE_NHALF, d, MOE_HALF), f32),
                        pltpu.VMEM((MOE_NHALF, d, MOE_HALF), f32),
                        pltpu.VMEM((MOE_NHALF, MOE_HALF, d), f32),
                        pltpu.SemaphoreType.DMA((2,)),
                        pltpu.SemaphoreType.DMA((MOE_NHALF, 3))],
    )
    vmem = 3 * d * D_EXPERT * 4 + 3 * d * MOE_HALF * 2 + 2 * tm * d * 4 + tm * d * 2 + 4 * tm * d * 4
    return pl.pallas_call(
        functools.partial(_moe_ffn_kernel, layer=layer),
        grid_spec=grid_spec,
        out_shape=jax.ShapeDtypeStruct((n_tiles * tm, d), f32),
        compiler_params=_cparams(("arbitrary",), vmem),
        name="moe_ffn",
    )(tile_e, tile_valid, row_tok, h2, w_gate, w_up, w_down)


def _moe_combine_kernel(pos_ref, ys_hbm, rw_ref, x_ref, gate_ref, g_ref, b_ref, o_ref, buf, sems):
    i = pl.program_id(0)
    n = pl.num_programs(0)
    tm = x_ref.shape[0]
    slot = i % 2

    def start(tile, s):
        for k in range(2):
            _gather_start(pos_ref, 2 * tile * tm + k, ys_hbm, buf.at[s, k], sems.at[s, k], tm, stride=2)

    @pl.when(i == 0)
    def _first_gather():
        start(0, 0)

    @pl.when(i + 1 < n)
    def _next_gather():
        start(i + 1, 1 - slot)

    for k in range(2):
        _gather_wait(ys_hbm, buf.at[slot, k], sems.at[slot, k], tm)
    rw = rw_ref[...]
    y = rw[:, 0:1] * buf[slot, 0] + rw[:, 1:2] * buf[slot, 1]
    o_ref[...] = _res_ln(x_ref[...], y, gate_ref[0], g_ref[...], b_ref[...])


def _moe_combine(pos, ys, rw, x2, gate, g, b, seq):
    n, d = x2.shape
    tm = 256
    row = lambda i, pos: (i, 0)
    grid_spec = pltpu.PrefetchScalarGridSpec(
        num_scalar_prefetch=1,
        grid=(n // tm,),
        in_specs=[
            pl.BlockSpec(memory_space=pl.ANY),
            pl.BlockSpec((tm, LANES), row),
            pl.BlockSpec((tm, d), row),
            pl.BlockSpec((1, 1, d), lambda i, pos: (i * tm // seq, 0, 0)),
            pl.BlockSpec((1, d), lambda i, pos: (0, 0)),
            pl.BlockSpec((1, d), lambda i, pos: (0, 0)),
        ],
        out_specs=pl.BlockSpec((tm, d), row),
        scratch_shapes=[pltpu.VMEM((2, 2, tm, d), f32), pltpu.SemaphoreType.DMA((2, 2))],
    )
    return pl.pallas_call(
        _moe_combine_kernel,
        grid_spec=grid_spec,
        out_shape=jax.ShapeDtypeStruct((n, d), f32),
        compiler_params=_cparams(("arbitrary",), 12 * tm * d * 4),
        name="moe_combine",
    )(pos, ys, rw, x2, gate, g, b)


def _moe_plan(eid, n_tiles):
    tm = MOE_TM
    flat_e = eid.reshape(-1)
    onehot = (flat_e[:, None] == jnp.arange(N_EXPERTS, dtype=i32)[None, :]).astype(i32)
    csum = jnp.cumsum(onehot, axis=0)
    rank = jnp.sum((csum - onehot) * onehot, axis=1)
    counts = csum[-1]
    padded = ((counts + tm - 1) // tm) * tm
    ends = jnp.cumsum(padded)
    pos = ((ends - padded)[flat_e] + rank).astype(i32)
    tile_start = jnp.arange(n_tiles, dtype=i32) * tm
    tile_valid = (tile_start < ends[-1]).astype(i32)
    tile_e = jnp.searchsorted(ends, tile_start, side="right").astype(i32)
    last_valid = jnp.maximum(ends[-1] // tm - 1, 0)
    tile_e = jnp.where(tile_valid > 0, tile_e, tile_e[last_valid])
    row_tok = jnp.zeros((n_tiles * tm,), i32).at[pos].set(jnp.arange(flat_e.shape[0], dtype=i32) // 2)
    return pos, tile_e, tile_valid, row_tok


def _moe_layer(x2, y2, gate_m, ln_g, ln_b, scale_f, shift_f, gate_f, ln_g2, ln_b2,
               w_grp, b_grp, w_exp, b_exp, w_gate, w_up, w_down, layer, seq):
    n, d = x2.shape
    pad = LANES - N_GROUPS - N_EXPERTS
    wr = jnp.concatenate([w_grp, w_exp, jnp.zeros((d, pad), f32)], axis=1)
    br = jnp.concatenate([b_grp, b_exp, jnp.zeros((pad,), f32)])[None, :]
    x1, h, rw, rid = _ln_router(x2, y2, gate_m, ln_g[None, :], ln_b[None, :], scale_f, shift_f, wr, br, seq)
    n_tiles = (2 * n) // MOE_TM + N_EXPERTS
    pos, tile_e, tile_valid, row_tok = _moe_plan(rid[:, :2], n_tiles)
    ys = _moe_ffn(h, w_gate, w_up, w_down, layer, tile_e, tile_valid, row_tok, n_tiles)
    return _moe_combine(pos, ys, rw, x1, gate_f, ln_g2[None, :], ln_b2[None, :], seq)


def _rope_table_kernel(pos_ref, fr_ref, sg_ref, c_ref, s_ref):
    ang = pos_ref[...].astype(f32) * fr_ref[...]
    c_ref[...] = jnp.cos(ang)
    s_ref[...] = jnp.sin(ang) * sg_ref[...]


def _rope_tables(pos2, head_dim):
    m = pos2.shape[0]
    half = head_dim // 2
    freqs = ROPE_THETA ** (-jnp.arange(half, dtype=f32) / half)
    reps = LANES // head_dim
    fr = jnp.tile(jnp.concatenate([freqs, freqs]), reps)[None, :]
    sg = jnp.tile(jnp.concatenate([-jnp.ones((half,), f32), jnp.ones((half,), f32)]), reps)[None, :]
    tm = min(m, 1024)
    vec = pl.BlockSpec((1, LANES), lambda i: (0, 0))
    out = pl.BlockSpec((tm, LANES), lambda i: (i, 0))
    return pl.pallas_call(
        _rope_table_kernel,
        grid=(m // tm,),
        in_specs=[pl.BlockSpec((tm, 1), lambda i: (i, 0)), vec, vec],
        out_specs=[out, out],
        out_shape=[jax.ShapeDtypeStruct((m, LANES), f32)] * 2,
        compiler_params=_cparams(("parallel",), 8 * tm * LANES * 4),
        name="rope_tables",
    )(pos2, fr, sg)


def _proj_rope(h_hi, w_t, c_tab, s_tab, *, col_off, n, scale, tm, tn):
    tab = pl.BlockSpec((tm, LANES), lambda i, j: (i, 0))
    return _mm([h_hi], w_t, col_off=col_off, n=n, tm=tm, tn=tn, out_dtype=bf16, transposed=True,
               epilogue=functools.partial(_rope_epilogue, scale=scale),
               extra=(c_tab, s_tab), extra_specs=(tab, tab))


IDX_W = IDX_HEADS * IDX_DIM
IDX_RAW = IDX_W + LANES


IDX_K = 4 * IDX_DIM


def _idx_rope_kernel(x_ref, c_ref, s_ref, q_ref, k_ref, w_ref):
    c = c_ref[...]
    s = s_ref[...]
    lane = lax.broadcasted_iota(i32, c.shape, 1)
    first = (lane & (IDX_DIM // 2)) == 0
    low = lane < IDX_DIM

    def rope(blk):
        swapped = jnp.where(first, pltpu.roll(blk, LANES - IDX_DIM // 2, 1), pltpu.roll(blk, IDX_DIM // 2, 1))
        return blk * c + swapped * s

    def hi_lo(x):
        hi = x.astype(bf16).astype(f32)
        return hi, x - hi

    for k in range(IDX_W // LANES):
        hi, lo = hi_lo(rope(x_ref[:, k * LANES:(k + 1) * LANES]))
        hi_sw = pltpu.roll(hi, IDX_DIM, 1)
        lo_sw = pltpu.roll(lo, IDX_DIM, 1)
        base = 2 * k * IDX_K
        q_ref[:, base:base + LANES] = jnp.where(low, hi, hi_sw).astype(bf16)
        q_ref[:, base + LANES:base + IDX_K] = jnp.where(low, lo, 0.0).astype(bf16)
        q_ref[:, base + IDX_K:base + IDX_K + LANES] = jnp.where(low, hi_sw, hi).astype(bf16)
        q_ref[:, base + IDX_K + LANES:base + 2 * IDX_K] = jnp.where(low, lo_sw, 0.0).astype(bf16)
    kw = x_ref[:, IDX_W:IDX_RAW]
    hi, lo = hi_lo(rope(kw))
    k_ref[:, 0:LANES] = jnp.where(low, hi, pltpu.roll(lo, IDX_DIM, 1)).astype(bf16)
    k_ref[:, LANES:IDX_K] = jnp.where(low, hi, 0.0).astype(bf16)
    w_ref[...] = kw


def _idx_rope(raw, c_tab, s_tab):
    m = raw.shape[0]
    tm = min(m, 512)
    tab = pl.BlockSpec((tm, LANES), lambda i: (i, 0))
    return pl.pallas_call(
        _idx_rope_kernel,
        grid=(m // tm,),
        in_specs=[pl.BlockSpec((tm, IDX_RAW), lambda i: (i, 0)), tab, tab],
        out_specs=[pl.BlockSpec((tm, IDX_HEADS * IDX_K), lambda i: (i, 0)),
                   pl.BlockSpec((tm, IDX_K), lambda i: (i, 0)), tab],
        out_shape=[jax.ShapeDtypeStruct((m, IDX_HEADS * IDX_K), bf16), jax.ShapeDtypeStruct((m, IDX_K), bf16),
                   jax.ShapeDtypeStruct((m, LANES), f32)],
        compiler_params=_cparams(("parallel",), 8 * tm * IDX_RAW * 4 + 4 * tm * IDX_HEADS * IDX_K * 2),
        name="idx_rope",
    )(raw, c_tab, s_tab)


INT_MIN = -2 ** 31


DSA_T = 256
DSA_KG = 4
MASKED = -1e30


def _key_group(nkb):
    return DSA_KG if nkb % DSA_KG == 0 else 1


def _dsa_index_kernel(q_ref, k_ref, qw_ref, m_ref, key_scr, *, topk, scale):
    i = pl.program_id(1)
    nkb, kb, tq = key_scr.shape
    qw_t = qw_ref[...].T
    q_chunk = (i * tq + lax.broadcasted_iota(i32, (1, tq), 1)) // CHUNK
    krow = lax.broadcasted_iota(i32, (kb, tq), 0)

    def score_block(j, c):
        kblk = k_ref[pl.ds(pl.multiple_of(j * kb, kb), kb), :]
        score = jnp.zeros((kb, tq), f32)
        for hd in range(IDX_HEADS):
            rel = lax.dot_general(kblk, q_ref[:, hd * IDX_K:(hd + 1) * IDX_K], _NT, preferred_element_type=f32)
            score = score + jnp.maximum(rel, 0.0) * qw_t[IDX_DIM + hd:IDX_DIM + hd + 1, :]
        score = score * scale
        adm = j * kb + krow < (q_chunk + 1) * CHUNK
        bits = pltpu.bitcast(score, i32)
        key = jnp.where(bits < 0, bits ^ 0x7FFFFFFF, bits)
        key_scr[j] = jnp.where(adm, key, INT_MIN)
        return c

    lax.fori_loop(0, i + 1, score_block, 0)

    kg = _key_group(nkb)
    ngroups = (i + kg) // kg

    def fill(j, c):
        key_scr[j] = jnp.full((kb, tq), INT_MIN, i32)
        return c

    lax.fori_loop(i + 1, ngroups * kg, fill, 0)

    def count_ge(cand):
        def grp(g, acc):
            for u in range(kg):
                ge = (key_scr[g * kg + u] >= cand).astype(f32)
                for r in range(kb // SUBLANES):
                    acc = acc + ge[r * SUBLANES:(r + 1) * SUBLANES, :]
            return acc

        acc = lax.fori_loop(0, ngroups, grp, jnp.zeros((SUBLANES, tq), f32))
        return jnp.sum(acc, axis=0, keepdims=True)

    kf = jnp.float32(topk)
    cur = jnp.where(count_ge(jnp.zeros((1, tq), i32)) >= kf, 0, INT_MIN).astype(i32)

    def bisect(it, cur):
        cand = cur + jnp.left_shift(jnp.int32(1), 30 - it)
        return jnp.where(count_ge(cand) >= kf, cand, cur)

    thr = jnp.maximum(lax.fori_loop(0, 31, bisect, cur), INT_MIN + 1)

    def write(j, c):
        m_ref[0, j] = jnp.where(key_scr[j] >= thr, 0.0, MASKED).T
        return c

    def clear(j, c):
        m_ref[0, j] = jnp.full((tq, kb), MASKED, f32)
        return c

    lax.fori_loop(0, i + 1, write, 0)
    lax.fori_loop(i + 1, nkb, clear, 0)


def _dsa_index(iq, ik, iw, batch, seq):
    tq = min(seq, DSA_T)
    nq = seq // tq
    topk = min(TOPK_MAX, seq // 4)
    scale = IDX_DIM ** -0.5 * IDX_HEADS ** -0.5
    return pl.pallas_call(
        functools.partial(_dsa_index_kernel, topk=topk, scale=scale),
        grid=(batch, nq),
        in_specs=[pl.BlockSpec((tq, IDX_HEADS * IDX_K), lambda b, i: (b * nq + i, 0)),
                  pl.BlockSpec((seq, IDX_K), lambda b, i: (b, 0)),
                  pl.BlockSpec((tq, LANES), lambda b, i: (b * nq + i, 0))],
        out_specs=pl.BlockSpec((1, nq, tq, tq), lambda b, i: (b * nq + i, 0, 0, 0)),
        out_shape=jax.ShapeDtypeStruct((batch * nq, nq, tq, tq), f32),
        scratch_shapes=[pltpu.VMEM((nq, tq, tq), i32)],
        compiler_params=_cparams(("parallel", "parallel"),
                                 8 * tq * seq * 4 + 4 * seq * IDX_K * 2 + 4 * tq * IDX_HEADS * IDX_K * 2),
        name="dsa_index",
    )(iq, ik, iw)


ATT_HG = 4


def _dsa_attn_kernel(q_ref, k_ref, v_ref, m_ref, o_ref):
    i = pl.program_id(2)
    _, nkb, tq, kb = m_ref.shape
    kg = _key_group(nkb)

    def group(g, carry):
        rows = pl.ds(pl.multiple_of(g * (kg * kb), kg * kb), kg * kb)
        parts = [m_ref[0, g * kg + u] for u in range(kg)]
        bias = jnp.concatenate(parts, axis=1) if kg > 1 else parts[0]
        out = []
        for h in range(ATT_HG):
            mx, den, acc = carry[h]
            sl = slice(h * HEAD_DIM, (h + 1) * HEAD_DIM)
            logits = lax.dot_general(q_ref[:, sl], k_ref[rows, sl], _NT, preferred_element_type=f32) + bias
            mx_new = jnp.maximum(mx, jnp.max(logits, axis=1, keepdims=True))
            alpha = jnp.exp(mx - mx_new)
            p = jnp.exp(logits - mx_new)
            den = alpha * den + jnp.sum(p, axis=1, keepdims=True)
            acc = alpha * acc + jnp.dot(p.astype(bf16), v_ref[rows, sl], preferred_element_type=f32)
            out.append((mx_new, den, acc))
        return tuple(out)

    init = tuple((jnp.full((tq, 1), MASKED, f32), jnp.zeros((tq, 1), f32), jnp.zeros((tq, HEAD_DIM), f32))
                 for _ in range(ATT_HG))
    final = lax.fori_loop(0, (i + kg) // kg, group, init)
    for h in range(ATT_HG):
        _, den, acc = final[h]
        o_ref[:, h * HEAD_DIM:(h + 1) * HEAD_DIM] = (acc / den).astype(o_ref.dtype)


def _dsa_attn(q, k, v, mask, batch, seq):
    m, w = q.shape
    _, nq, tq, _ = mask.shape
    gw = ATT_HG * HEAD_DIM
    qspec = pl.BlockSpec((tq, gw), lambda b, g, i: (b * nq + i, g))
    kvspec = pl.BlockSpec((seq, gw), lambda b, g, i: (b, g))
    vmem = 2 * (2 * tq * gw * 2 + 2 * seq * gw * 2 + tq * seq * 4) + 8 * tq * DSA_KG * tq * 4
    return pl.pallas_call(
        _dsa_attn_kernel,
        grid=(batch, w // gw, nq),
        in_specs=[qspec, kvspec, kvspec, pl.BlockSpec((1, nq, tq, tq), lambda b, g, i: (b * nq + i, 0, 0, 0))],
        out_specs=qspec,
        out_shape=jax.ShapeDtypeStruct((m, w), bf16),
        compiler_params=_cparams(("parallel", "parallel", "arbitrary"), vmem),
        name="dsa_attn",
    )(q, k, v, mask)


RET_HG = 4


def _retention_kernel(q_ref, k_ref, v_ref, g_ref, gn_ref, lg_ref, o_ref, state):
    c = pl.program_id(2)
    t = q_ref.shape[0]

    @pl.when(c == 0)
    def _init():
        state[...] = jnp.zeros_like(state)

    ri = lax.broadcasted_iota(i32, (t, t), 0)
    ci = lax.broadcasted_iota(i32, (t, t), 1)
    diff = (ri - ci).astype(f32)
    pos = lax.broadcasted_iota(i32, (t, 1), 0).astype(f32)
    for h in range(RET_HG):
        sl = slice(h * HEAD_DIM, (h + 1) * HEAD_DIM)
        lg = lg_ref[:, h * HEAD_DIM:h * HEAD_DIM + 1]
        q = q_ref[:, sl]
        k = k_ref[:, sl]
        v = v_ref[:, sl]
        decay = jnp.where(diff >= 0, jnp.exp(lg * jnp.maximum(diff, 0.0)), 0.0)
        scores = lax.dot_general(q, k, _NT, preferred_element_type=f32) * decay
        inner = jnp.dot(scores.astype(bf16), v, preferred_element_type=f32)
        st = state[h]
        cross = jnp.dot(q, st.astype(bf16), preferred_element_type=f32) * jnp.exp(lg * (pos + 1.0))
        kz = k.astype(f32) * jnp.exp(lg * (t - 1.0 - pos))
        kv = jnp.dot(kz.T.astype(bf16), v, preferred_element_type=f32)
        state[h] = jnp.exp(lg * t) * st + kv
        ret = inner + cross
        mu = jnp.mean(ret, axis=1, keepdims=True)
        rc = ret - mu
        var = jnp.mean(rc * rc, axis=1, keepdims=True)
        gate = g_ref[:, sl]
        out = rc * lax.rsqrt(var + LN_EPS) * gn_ref[:, sl] * (gate * _sigmoid(gate))
        o_ref[:, sl] = out.astype(o_ref.dtype)


def _retention(q, k, v, g, gn_g, batch, seq):
    m, w = q.shape
    heads = w // HEAD_DIM
    t = min(seq, 256)
    nc = seq // t
    gw = RET_HG * HEAD_DIM
    log_g = jnp.log1p(-jnp.exp2(-5.0 - jnp.arange(heads, dtype=f32)))
    lg = jnp.repeat(log_g, HEAD_DIM)[None, :]
    blk = pl.BlockSpec((t, gw), lambda b, gi, c: (b * nc + c, gi))
    vec = pl.BlockSpec((1, gw), lambda b, gi, c: (0, gi))
    return pl.pallas_call(
        _retention_kernel,
        grid=(batch, w // gw, nc),
        in_specs=[blk, blk, blk, blk, vec, vec],
        out_specs=blk,
        out_shape=jax.ShapeDtypeStruct((m, w), bf16),
        scratch_shapes=[pltpu.VMEM((RET_HG, HEAD_DIM, HEAD_DIM), f32)],
        compiler_params=_cparams(("parallel", "parallel", "arbitrary"), 16 * t * gw * 4 + 8 * t * t * 4),
        name="retention",
    )(q, k, v, g, gn_g[None, :], lg)


def _mixer_sparse_retention(h_hi, h_lo, pos2, w_in, w_out, gn_g, batch, seq):
    m, d = h_hi.shape
    gw = d // 2
    tm = min(m, 1024)
    tn = 512
    c128, s128 = _rope_tables(pos2, HEAD_DIM)
    c64, s64 = _rope_tables(pos2, IDX_DIM)
    w_t = jnp.swapaxes(w_in, 0, 1)
    rope = functools.partial(_proj_rope, h_hi, w_t, c_tab=c128, s_tab=s128, n=gw, tm=tm, tn=tn)
    plain = functools.partial(_mm, [h_hi], w_t, n=gw, tm=tm, tn=tn, transposed=True)
    aq = rope(col_off=0, scale=HEAD_DIM ** -0.5)
    ak = rope(col_off=gw, scale=1.0)
    av = plain(col_off=2 * gw, out_dtype=bf16)
    idx_raw = _mm3(h_hi, h_lo, w_t, col_off=3 * gw, n=IDX_RAW, tm=min(m, 512), tn=IDX_RAW // 3, transposed=True)
    iq, ik, iw = _idx_rope(idx_raw, c64, s64)
    mask = _dsa_index(iq, ik, iw, batch, seq)
    a_out = _dsa_attn(aq, ak, av, mask, batch, seq)
    b_off = 3 * gw + IDX_W + IDX_DIM + IDX_HEADS
    bq = rope(col_off=b_off, scale=1.0)
    bk = rope(col_off=b_off + gw, scale=HEAD_DIM ** -0.5)
    bv = plain(col_off=b_off + 2 * gw, out_dtype=bf16)
    bg = plain(col_off=b_off + 3 * gw, out_dtype=f32)
    b_out = _retention(bq, bk, bv, bg, gn_g, batch, seq)
    return _mm([a_out, b_out], w_out, col_off=0, n=d, tm=tm, tn=tn, out_dtype=f32)


MXU_N = 256


def _seg_sum64(x):
    r = lax.broadcasted_iota(i32, (MXU_N, MXU_N), 0) // C_HEAD_DIM
    c = lax.broadcasted_iota(i32, (MXU_N, MXU_N), 1) // C_HEAD_DIM
    ones = (r == c).astype(bf16)
    hi = x.astype(bf16)
    r1 = x - hi.astype(f32)
    mid = r1.astype(bf16)
    lo = (r1 - mid.astype(f32)).astype(bf16)
    cols = []
    for k in range(x.shape[1] // MXU_N):
        sl = slice(k * MXU_N, (k + 1) * MXU_N)
        s = jnp.dot(hi[:, sl], ones, preferred_element_type=f32)
        s += jnp.dot(mid[:, sl], ones, preferred_element_type=f32)
        s += jnp.dot(lo[:, sl], ones, preferred_element_type=f32)
        cols.append(s)
    return jnp.concatenate(cols, axis=1) if len(cols) > 1 else cols[0]


def _neg_softplus_neg(z):
    return jnp.minimum(z, 0.0) - jnp.log(1.0 + jnp.exp(-jnp.abs(z)))


def _rwkv_prep_kernel(p_ref, pp_ref, mu_ref, w0_ref, wup_ref, a0_ref, aup_ref, gup_ref, ka_ref, rk_ref,
                      r_o, w_o, k_o, a_o, v_o, g_o, bon_o, *, seq, gw):
    i = pl.program_id(0)
    tm = p_ref.shape[0]
    p = p_ref[...]
    prev_row = jnp.where((i * tm) % seq == 0, 0.0, pp_ref[SUBLANES - 1:SUBLANES, :])
    row = lax.broadcasted_iota(i32, (tm, 1), 0)
    shifted = jnp.where(row == 0, prev_row, pltpu.roll(p, 1, 0))
    pm = p + (shifted - p) * mu_ref[...]
    r = pm[:, 0:gw]
    k = pm[:, gw:2 * gw]
    v = pm[:, 2 * gw:3 * gw]
    o = 3 * gw
    dw = pm[:, o:o + C_DECAY_RANK]
    da = pm[:, o + C_DECAY_RANK:o + C_DECAY_RANK + C_ICLR_RANK]
    dg = pm[:, o + C_DECAY_RANK + C_ICLR_RANK:]
    w_log = _neg_softplus_neg(w0_ref[...] + _dot3(jnp.tanh(dw), wup_ref[...])) - 0.5
    decay = jnp.exp(-jnp.exp(w_log))
    a = _sigmoid(a0_ref[...] + _dot3(da, aup_ref[...]))
    g = _dot3(_sigmoid(dg), gup_ref[...])
    k2 = k * (1.0 + (a - 1.0) * ka_ref[...])
    r_o[...] = r
    w_o[...] = decay
    k_o[...] = k
    a_o[...] = a
    v_o[...] = v
    g_o[...] = g
    bon_o[...] = _seg_sum64(r * k2 * rk_ref[...]) * v


def _rwkv_prep(pc, mu, w0, w_up, a0, a_up, g_up, k_a, r_k, seq):
    m, cc = pc.shape
    gw = w0.shape[0]
    tm = 128
    nsub = tm // SUBLANES
    vec = pl.BlockSpec((1, gw), lambda i: (0, 0))
    out = pl.BlockSpec((tm, gw), lambda i: (i, 0))
    full = lambda a: pl.BlockSpec(a.shape, lambda i: (0, 0))
    return pl.pallas_call(
        functools.partial(_rwkv_prep_kernel, seq=seq, gw=gw),
        grid=(m // tm,),
        in_specs=[pl.BlockSpec((tm, cc), lambda i: (i, 0)),
                  pl.BlockSpec((SUBLANES, cc), lambda i: (jnp.maximum(i * nsub - 1, 0), 0)),
                  pl.BlockSpec((1, cc), lambda i: (0, 0)),
                  vec, full(w_up), vec, full(a_up), full(g_up), vec, vec],
        out_specs=[out] * 7,
        out_shape=[jax.ShapeDtypeStruct((m, gw), f32)] * 7,
        compiler_params=_cparams(("parallel",), 2 * tm * cc * 4 + 2 * 7 * tm * gw * 4 + 16 * tm * gw * 4),
        name="rwkv_prep",
    )(pc, pc, mu[None, :], w0[None, :], w_up, a0[None, :], a_up, g_up, k_a[None, :], r_k.reshape(1, -1))


SCAN_T = 64
SCAN_G = 8


def _rwkv_scan_kernel(w_ref, a_ref, k_ref, r_ref, v_ref, kkw_ref, kaw_ref, y_ref,
                      z_ref, w_s, kk_s, ka_s, k_s, r_s, v_s, y_s):
    c = pl.program_id(0)
    batch, steps, width = w_ref.shape
    nslab = z_ref.shape[0]
    heads = width // C_HEAD_DIM
    half = batch * heads

    @pl.when(c == 0)
    def _init():
        z_ref[...] = jnp.zeros_like(z_ref)

    kkw = kkw_ref[...]
    kaw = kaw_ref[...]

    def by_head(ref, rows, copies):
        per_batch = [ref[b, rows, :].reshape(SCAN_G, heads, C_HEAD_DIM) for b in range(batch)]
        return jnp.concatenate(per_batch * copies, axis=1)

    def key_tiles(ref, rows):
        return jnp.swapaxes(by_head(ref, rows, 2), 1, 2)

    def derive(g, carry):
        rows = pl.ds(pl.multiple_of(g * SCAN_G, SCAN_G), SCAN_G)
        a = key_tiles(a_ref, rows)
        kraw = key_tiles(k_ref, rows)
        kk = kraw * kkw
        kk = kk * lax.rsqrt(jnp.maximum(jnp.sum(kk * kk, axis=1, keepdims=True), 1e-24))
        w_s[rows] = key_tiles(w_ref, rows)
        r_s[rows] = key_tiles(r_ref, rows)
        kk_s[rows] = kk
        ka_s[rows] = kk * a
        k_s[rows] = kraw * (1.0 + (a - 1.0) * kaw)
        vt = jnp.swapaxes(by_head(v_ref, rows, 1), 1, 2)
        v_s[rows] = jnp.concatenate([vt[:, :nslab, :], vt[:, nslab:, :]], axis=2)
        return carry

    lax.fori_loop(0, steps // SCAN_G, derive, 0)

    def step(t, carry):
        w = w_s[t]
        kk = kk_s[t]
        ka = ka_s[t]
        k = k_s[t]
        r = r_s[t]
        vrows = v_s[t]
        for s in range(nslab):
            z = z_ref[s]
            sk = jnp.sum(z * kk, axis=0, keepdims=True)
            zn = z * w - ka * sk + k * vrows[s:s + 1, :]
            z_ref[s] = zn
            y_s[t, s:s + 1, :] = jnp.sum(zn * r, axis=0, keepdims=True)
        return carry

    lax.fori_loop(0, steps, step, 0)

    def restore(g, carry):
        rows = pl.ds(pl.multiple_of(g * SCAN_G, SCAN_G), SCAN_G)
        y = y_s[rows]
        yt = jnp.concatenate([y[:, :, :half], y[:, :, half:]], axis=1)
        yh = jnp.swapaxes(yt, 1, 2)
        for b in range(batch):
            y_ref[b, rows, :] = yh[:, b * heads:(b + 1) * heads, :].reshape(SCAN_G, width)
        return carry

    lax.fori_loop(0, steps // SCAN_G, restore, 0)


def _rwkv_scan(w, a, k, r, v, k_k, k_a, batch, seq):
    width = w.shape[1]
    heads = width // C_HEAD_DIM
    assert 2 * batch * heads == LANES
    nslab = C_HEAD_DIM // 2
    steps = min(seq, SCAN_T)
    consts = [jnp.tile(p.reshape(heads, C_HEAD_DIM).T, (1, 2 * batch)) for p in (k_k, k_a)]
    nat = pl.BlockSpec((batch, steps, width), lambda c: (0, c, 0))
    const_spec = pl.BlockSpec((C_HEAD_DIM, LANES), lambda c: (0, 0))
    tile_bytes = steps * C_HEAD_DIM * LANES * 4
    y = pl.pallas_call(
        _rwkv_scan_kernel,
        grid=(seq // steps,),
        in_specs=[nat] * 5 + [const_spec, const_spec],
        out_specs=nat,
        out_shape=jax.ShapeDtypeStruct((batch, seq, width), f32),
        scratch_shapes=[pltpu.VMEM((nslab, C_HEAD_DIM, LANES), f32)]
        + [pltpu.VMEM((steps, C_HEAD_DIM, LANES), f32)] * 5
        + [pltpu.VMEM((steps, nslab, LANES), f32)] * 2,
        compiler_params=_cparams(("arbitrary",), 6 * tile_bytes + 2 * 6 * batch * steps * width * 4
                                 + 8 * tile_bytes),
        name="rwkv_scan",
    )(*(x.reshape(batch, seq, width) for x in (w, a, k, r, v)), *consts)
    return y.reshape(batch * seq, width)


def _rwkv_post_kernel(y_ref, g_ref, bon_ref, lng_ref, lnb_ref, o_ref):
    y = y_ref[...]
    mu = _seg_sum64(y) * (1.0 / C_HEAD_DIM)
    yc = y - mu
    var = _seg_sum64(yc * yc) * (1.0 / C_HEAD_DIM)
    yn = yc * lax.rsqrt(var + C_EPS) * lng_ref[...] + lnb_ref[...]
    o_ref[...] = ((yn + bon_ref[...]) * g_ref[...]).astype(o_ref.dtype)


def _rwkv_post(y, g, bonus, ln_g, ln_b):
    m, gw = y.shape
    tm = 256
    blk = pl.BlockSpec((tm, gw), lambda i: (i, 0))
    vec = pl.BlockSpec((1, gw), lambda i: (0, 0))
    return pl.pallas_call(
        _rwkv_post_kernel,
        grid=(m // tm,),
        in_specs=[blk, blk, blk, vec, vec],
        out_specs=blk,
        out_shape=jax.ShapeDtypeStruct((m, gw), bf16),
        compiler_params=_cparams(("parallel",), 16 * tm * gw * 4),
        name="rwkv_post",
    )(y, g, bonus, ln_g[None, :], ln_b[None, :])


GELU_C = 0.7978845608028654


def _lru_kernel(px_ref, pg_ref, cw_ref, cb_ref, wa_ref, ba_ref, wx_ref, bx_ref, lam_ref, o_ref,
                tail, hcar, a_s, b_s):
    c = pl.program_id(1)
    t = px_ref.shape[0]

    @pl.when(c == 0)
    def _init():
        tail[...] = jnp.zeros_like(tail)
        hcar[...] = jnp.zeros_like(hcar)

    x = px_ref[...]
    ext = jnp.concatenate([tail[...], x], axis=0)
    xc = cb_ref[...]
    for j in range(D_CONV):
        off = SUBLANES - (D_CONV - 1) + j
        xc = xc + cw_ref[j:j + 1, :] * ext[off:off + t, :]
    tail[...] = x[t - SUBLANES:, :]
    lam = lam_ref[...]
    sp = jnp.maximum(-lam, 0.0) + jnp.log(1.0 + jnp.exp(-jnp.abs(lam)))
    bw = wa_ref.shape[1]
    for n in range(wa_ref.shape[0]):
        sl = slice(n * bw, (n + 1) * bw)
        xb = xc[:, sl]
        rg = _sigmoid(_dot3(xb, wa_ref[n]) + ba_ref[:, sl])
        ig = _sigmoid(_dot3(xb, wx_ref[n]) + bx_ref[:, sl])
        log_a = -LRU_C * rg * sp[:, sl]
        a_s[:, sl] = jnp.exp(log_a)
        th = jnp.tanh(log_a)
        one_minus_a2 = -2.0 * th / (1.0 - th)
        b_s[:, sl] = jnp.sqrt(one_minus_a2) * (ig * xb)

    sub = lax.broadcasted_iota(i32, (SUBLANES, a_s.shape[1]), 0)

    def rows8(g, h):
        r0 = pl.multiple_of(g * SUBLANES, SUBLANES)
        a = a_s[pl.ds(r0, SUBLANES), :]
        b = b_s[pl.ds(r0, SUBLANES), :]
        for s in (1, 2, 4):
            b = a * jnp.where(sub >= s, pltpu.roll(b, s, 0), 0.0) + b
            a = a * jnp.where(sub >= s, pltpu.roll(a, s, 0), 1.0)
        hs = a * h + b
        b_s[pl.ds(r0, SUBLANES), :] = hs
        return hs[SUBLANES - 1:SUBLANES, :]

    hcar[...] = lax.fori_loop(0, t // SUBLANES, rows8, hcar[...])
    gate = pg_ref[...]
    gelu = 0.5 * gate * (1.0 + jnp.tanh(GELU_C * (gate + 0.044715 * (gate * gate * gate))))
    o_ref[...] = (b_s[...] * gelu).astype(o_ref.dtype)


def _lru(px, pg, conv_w, conv_b, w_a, b_a, w_x, b_x, lam, batch, seq):
    m, w = px.shape
    t = min(seq, 256)
    nc = seq // t
    blk = pl.BlockSpec((t, w), lambda b, c: (b * nc + c, 0))
    vec = pl.BlockSpec((1, w), lambda b, c: (0, 0))
    wblk = pl.BlockSpec(w_a.shape, lambda b, c: (0, 0, 0))
    return pl.pallas_call(
        _lru_kernel,
        grid=(batch, nc),
        in_specs=[blk, blk, pl.BlockSpec((D_CONV, w), lambda b, c: (0, 0)), vec, wblk, vec, wblk, vec, vec],
        out_specs=blk,
        out_shape=jax.ShapeDtypeStruct((m, w), bf16),
        scratch_shapes=[pltpu.VMEM((SUBLANES, w), f32), pltpu.VMEM((1, w), f32),
                        pltpu.VMEM((t, w), f32), pltpu.VMEM((t, w), f32)],
        compiler_params=_cparams(("parallel", "arbitrary"), 16 * t * w * 4),
        name="lru",
    )(px, pg, conv_w, conv_b[None, :], w_a, b_a[None, :], w_x, b_x[None, :], lam[None, :])


def _mixer_rwkv_lru(h_hi, w_in, w_out, mu, w0, w_up, a0, a_up, g_up, k_k, k_a, r_k, ln_g, ln_b,
                    conv_w, conv_b, w_a, b_a, w_x, b_x, lam, batch, seq):
    m, d = h_hi.shape
    gw = d // 2
    c_cols = 3 * gw + C_DECAY_RANK + C_ICLR_RANK + C_GATE_RANK
    tm = min(m, 1024)
    tn = 512
    proj = functools.partial(_mm, [h_hi], w_in, tm=tm, tn=tn, out_dtype=f32)
    pc = proj(col_off=0, n=c_cols)
    pg = proj(col_off=c_cols, n=gw)
    px = proj(col_off=c_cols + gw, n=gw)
    r, w, k, a, v, g, bonus = _rwkv_prep(pc, mu, w0, w_up, a0, a_up, g_up, k_a, r_k, seq)
    y = _rwkv_scan(w, a, k, r, v, k_k, k_a, batch, seq)
    c_out = _rwkv_post(y, g, bonus, ln_g, ln_b)
    d_out = _lru(px, pg, conv_w, conv_b, w_a, b_a, w_x, b_x, lam, batch, seq)
    return _mm([c_out, d_out], w_out, col_off=0, n=d, tm=tm, tn=tn, out_dtype=f32)


ADA_ROWS = 16


def _ada_kernel(c_ref, w_ref, b_ref, o_ref):
    c = c_ref[...]
    s = (c * _sigmoid(c)).astype(bf16)
    o_ref[...] = jnp.dot(s, w_ref[...].astype(bf16), preferred_element_type=f32) + b_ref[...]


def _ada(c, ada_w, ada_b):
    batch, d = c.shape
    n = ada_w.shape[1]
    tn = 512
    cp = jnp.zeros((ADA_ROWS, d), f32).at[:batch].set(c)
    out = pl.pallas_call(
        _ada_kernel,
        grid=(n // tn,),
        in_specs=[pl.BlockSpec((ADA_ROWS, d), lambda j: (0, 0)),
                  pl.BlockSpec((d, tn), lambda j: (0, j)),
                  pl.BlockSpec((1, tn), lambda j: (0, j))],
        out_specs=pl.BlockSpec((ADA_ROWS, tn), lambda j: (0, j)),
        out_shape=jax.ShapeDtypeStruct((ADA_ROWS, n), f32),
        compiler_params=_cparams(("parallel",), 3 * d * tn * 4),
        name="ada",
    )(cp, ada_w, ada_b[None, :])
    return out[:batch]


def kernel(x, c, positions, ada_w, ada_b, ada_table, ln_g, ln_b, ab_w_in, ab_w_out, ret_gn_g, cd_w_in, cd_w_out, rwkv_mu, rwkv_w0, rwkv_w_up, rwkv_a0, rwkv_a_up, rwkv_g_up, rwkv_k_k, rwkv_k_a, rwkv_r_k, rwkv_ln_g, rwkv_ln_b, lru_conv_w, lru_conv_b, lru_w_a, lru_b_a, lru_w_x, lru_b_x, lru_lambda, moe_w_grp, moe_b_grp, moe_w_exp, moe_b_exp, moe_w_gate, moe_w_up, moe_w_down):
    batch, seq, d = x.shape
    x2 = x.reshape(batch * seq, d)
    pos2 = positions.reshape(batch * seq, 1)
    ada = _ada(c, ada_w, ada_b).reshape(batch, 6, 1, d)
    for layer in range(DEPTH):
        mod = ada + ada_table[layer][None, :, None, :]
        shift_m, scale_m, gate_m, shift_f, scale_f, gate_f = (mod[:, i] for i in range(6))
        h_hi, h_lo = _modcast(x2, scale_m, shift_m, seq)
        j = layer // 2
        if layer % 2 == 0:
            y = _mixer_sparse_retention(h_hi, h_lo, pos2, ab_w_in[j], ab_w_out[j], ret_gn_g[j], batch, seq)
        else:
            y = _mixer_rwkv_lru(h_hi, cd_w_in[j], cd_w_out[j], rwkv_mu[j], rwkv_w0[j], rwkv_w_up[j],
                                rwkv_a0[j], rwkv_a_up[j], rwkv_g_up[j], rwkv_k_k[j], rwkv_k_a[j],
                                rwkv_r_k[j], rwkv_ln_g[j], rwkv_ln_b[j], lru_conv_w[j], lru_conv_b[j],
                                lru_w_a[j], lru_b_a[j], lru_w_x[j], lru_b_x[j], lru_lambda[j], batch, seq)
        x2 = _moe_layer(x2, y, gate_m, ln_g[layer, 0], ln_b[layer, 0], scale_f, shift_f, gate_f,
                        ln_g[layer, 1], ln_b[layer, 1], moe_w_grp[layer], moe_b_grp[layer],
                        moe_w_exp[layer], moe_b_exp[layer], moe_w_gate, moe_w_up, moe_w_down, layer, seq)
    return x2.reshape(batch, seq, d)
```

```python
import functools

import jax
import jax.numpy as jnp
from jax import lax
from jax.experimental import pallas as pl
from jax.experimental.pallas import tpu as pltpu

f32 = jnp.float32
bf16 = jnp.bfloat16
i32 = jnp.int32

DEPTH = 2
CHUNK = 64
ROPE_THETA = 10000.0
LN_EPS = 1e-5
ALPHA = (2 * DEPTH) ** 0.25
HEAD_DIM = 128
IDX_HEADS = 16
IDX_DIM = 64
TOPK_MAX = 256
C_HEAD_DIM = 64
C_DECAY_RANK = 128
C_ICLR_RANK = 128
C_GATE_RANK = 256
C_EPS = 64e-5
D_BLOCKS = 16
D_CONV = 4
LRU_C = 8.0
N_GROUPS = 4
EXPERTS_PER_GROUP = 8
N_EXPERTS = N_GROUPS * EXPERTS_PER_GROUP
D_EXPERT = 512

LANES = 128
SUBLANES = 8
VMEM_BYTES_V7X = 64 * 1024 * 1024
VMEM_HEADROOM = 8 * 1024 * 1024


def _cparams(semantics, vmem_bytes):
    limit = min(int(vmem_bytes) + VMEM_HEADROOM, VMEM_BYTES_V7X - VMEM_HEADROOM)
    return pltpu.CompilerParams(dimension_semantics=semantics, vmem_limit_bytes=limit)


def _split_bf16(a):
    hi = a.astype(bf16)
    lo = (a - hi.astype(f32)).astype(bf16)
    return hi, lo


def _dot3(a, b, dims=(((1,), (0,)), ((), ()))):
    ah, al = _split_bf16(a)
    bh, bl = _split_bf16(b)
    dg = functools.partial(lax.dot_general, dimension_numbers=dims, preferred_element_type=f32)
    return dg(ah, bh) + dg(ah, bl) + dg(al, bh)


def _sigmoid(x):
    return 1.0 / (1.0 + jnp.exp(-x))


def _modcast_kernel(x_ref, sc_ref, sh_ref, hi_ref, lo_ref):
    h = x_ref[...] * (1.0 + sc_ref[0]) + sh_ref[0]
    hi, lo = _split_bf16(h)
    hi_ref[...] = hi
    lo_ref[...] = lo


def _modcast(x2, sc, sh, seq):
    m, k = x2.shape
    tm = 256
    row = pl.BlockSpec((tm, k), lambda i: (i, 0))
    per_batch = pl.BlockSpec((1, 1, k), lambda i: (i * tm // seq, 0, 0))
    return pl.pallas_call(
        _modcast_kernel,
        grid=(m // tm,),
        in_specs=[row, per_batch, per_batch],
        out_specs=[row, row],
        out_shape=[jax.ShapeDtypeStruct((m, k), bf16)] * 2,
        compiler_params=_cparams(("parallel",), 2 * tm * k * (4 + 2 + 2)),
        name="modcast",
    )(x2, sc, sh)


_NN = (((1,), (0,)), ((), ()))
_NT = (((1,), (1,)), ((), ()))


def _mm_kernel(*refs, nparts, epilogue, dims):
    acc = None
    for p in range(nparts):
        d = lax.dot_general(refs[p][...], refs[nparts + p][...].astype(bf16), dims, preferred_element_type=f32)
        acc = d if acc is None else acc + d
    epilogue(acc, *refs[2 * nparts:])


def _mm3_kernel(xh_ref, xl_ref, w_ref, o_ref, *, dims):
    wh, wl = _split_bf16(w_ref[...])
    xh = xh_ref[...]
    dg = functools.partial(lax.dot_general, dimension_numbers=dims, preferred_element_type=f32)
    o_ref[...] = dg(xh, wh) + dg(xh, wl) + dg(xl_ref[...], wh)


def _w_spec(kp, tn, k_off, col_off, transposed):
    if transposed:
        assert col_off % SUBLANES == 0 and tn % SUBLANES == 0
        return pl.BlockSpec((pl.Element(tn), pl.Element(kp)),
                            lambda i, j: (pl.multiple_of(col_off + j * tn, SUBLANES), k_off))
    assert col_off % tn == 0 and k_off % kp == 0
    return pl.BlockSpec((kp, tn), lambda i, j: (k_off // kp, col_off // tn + j))


def _store_epilogue(acc, o_ref):
    o_ref[...] = acc.astype(o_ref.dtype)


def _rope_epilogue(acc, c_ref, s_ref, o_ref, *, scale):
    c = c_ref[...]
    s = s_ref[...]
    for k in range(acc.shape[1] // LANES):
        blk = acc[:, k * LANES:(k + 1) * LANES]
        rot = blk * c + pltpu.roll(blk, LANES // 2, 1) * s
        if scale != 1.0:
            rot = rot * scale
        o_ref[:, k * LANES:(k + 1) * LANES] = rot.astype(o_ref.dtype)


def _mm(xs, w, *, col_off, n, tm, tn, out_dtype, epilogue=_store_epilogue, extra=(), extra_specs=(),
        transposed=False):
    m = xs[0].shape[0]
    kp = xs[0].shape[1]
    nparts = len(xs)
    assert all(x.shape == (m, kp) for x in xs) and w.shape[1 if transposed else 0] == nparts * kp
    assert m % tm == 0 and n % tn == 0
    x_specs = [pl.BlockSpec((tm, kp), lambda i, j: (i, 0)) for _ in xs]
    w_specs = [_w_spec(kp, tn, p * kp, col_off, transposed) for p in range(nparts)]
    vmem = 2 * nparts * (tm * kp * 2 + kp * tn * 4) + nparts * kp * tn * 2 + 4 * tm * tn * 4
    return pl.pallas_call(
        functools.partial(_mm_kernel, nparts=nparts, epilogue=epilogue, dims=_NT if transposed else _NN),
        grid=(m // tm, n // tn),
        in_specs=x_specs + w_specs + list(extra_specs),
        out_specs=pl.BlockSpec((tm, tn), lambda i, j: (i, j)),
        out_shape=jax.ShapeDtypeStruct((m, n), out_dtype),
        compiler_params=_cparams(("parallel", "arbitrary"), vmem),
        name="mm",
    )(*xs, *([w] * nparts), *extra)


def _mm3(xh, xl, w, *, col_off, n, tm, tn, transposed=False):
    m, k = xh.shape
    assert m % tm == 0 and n % tn == 0
    xspec = pl.BlockSpec((tm, k), lambda i, j: (i, 0))
    vmem = 2 * (2 * tm * k * 2 + k * tn * 4) + 2 * k * tn * 2 + 4 * tm * tn * 4
    return pl.pallas_call(
        functools.partial(_mm3_kernel, dims=_NT if transposed else _NN),
        grid=(m // tm, n // tn),
        in_specs=[xspec, xspec, _w_spec(k, tn, 0, col_off, transposed)],
        out_specs=pl.BlockSpec((tm, tn), lambda i, j: (i, j)),
        out_shape=jax.ShapeDtypeStruct((m, n), f32),
        compiler_params=_cparams(("parallel", "arbitrary"), vmem),
        name="mm3",
    )(xh, xl, w)


def _res_ln(x, y, gate, g, b):
    z = ALPHA * x + (1.0 + gate) * y
    mu = jnp.mean(z, axis=-1, keepdims=True)
    zc = z - mu
    var = jnp.mean(zc * zc, axis=-1, keepdims=True)
    return zc * lax.rsqrt(var + LN_EPS) * g + b


def _ln_router_kernel(x_ref, y_ref, gate_ref, g_ref, b_ref, sc_ref, sh_ref, wr_ref, br_ref,
                      xo_ref, h_ref, rw_ref, rid_ref):
    xn = _res_ln(x_ref[...], y_ref[...], gate_ref[0], g_ref[...], b_ref[...])
    xo_ref[...] = xn
    h = xn * (1.0 + sc_ref[0]) + sh_ref[0]
    h_ref[...] = h
    logits = _dot3(h, wr_ref[...]) + br_ref[...]
    lane = lax.broadcasted_iota(i32, logits.shape, 1)
    neg = jnp.float32(-jnp.inf)
    is_grp = lane < N_GROUPS
    gl = jnp.where(is_grp, logits, neg)
    gmax = jnp.max(gl, axis=1, keepdims=True)
    gidx = jnp.min(jnp.where(gl == gmax, lane, LANES), axis=1, keepdims=True)
    gsum = jnp.sum(jnp.where(is_grp, jnp.exp(gl - gmax), 0.0), axis=1, keepdims=True)
    grp_p = 1.0 / gsum
    lo = N_GROUPS + EXPERTS_PER_GROUP * gidx
    in_grp = jnp.logical_and(lane >= lo, lane < lo + EXPERTS_PER_GROUP)
    el = jnp.where(in_grp, logits, neg)
    m1 = jnp.max(el, axis=1, keepdims=True)
    i1 = jnp.min(jnp.where(el == m1, lane, LANES), axis=1, keepdims=True)
    el2 = jnp.where(lane == i1, neg, el)
    m2 = jnp.max(el2, axis=1, keepdims=True)
    i2 = jnp.min(jnp.where(el2 == m2, lane, LANES), axis=1, keepdims=True)
    e2 = jnp.exp(m2 - m1)
    w1 = grp_p / (1.0 + e2)
    w2 = grp_p * e2 / (1.0 + e2)
    rw_ref[...] = jnp.where(lane == 0, w1, jnp.where(lane == 1, w2, 0.0))
    rid_ref[...] = jnp.where(lane == 0, i1 - N_GROUPS, jnp.where(lane == 1, i2 - N_GROUPS, 0))


def _ln_router(x2, y2, gate, g, b, sc, sh, wr, br, seq):
    m, d = x2.shape
    tm = 256
    row = pl.BlockSpec((tm, d), lambda i: (i, 0))
    per_batch = pl.BlockSpec((1, 1, d), lambda i: (i * tm // seq, 0, 0))
    vec = pl.BlockSpec((1, d), lambda i: (0, 0))
    small = pl.BlockSpec((tm, LANES), lambda i: (i, 0))
    return pl.pallas_call(
        _ln_router_kernel,
        grid=(m // tm,),
        in_specs=[row, row, per_batch, vec, vec, per_batch, per_batch,
                  pl.BlockSpec((d, LANES), lambda i: (0, 0)), pl.BlockSpec((1, LANES), lambda i: (0, 0))],
        out_specs=[row, row, small, small],
        out_shape=[jax.ShapeDtypeStruct((m, d), f32), jax.ShapeDtypeStruct((m, d), f32),
                   jax.ShapeDtypeStruct((m, LANES), f32), jax.ShapeDtypeStruct((m, LANES), i32)],
        compiler_params=_cparams(("parallel",), 12 * tm * d * 4 + 2 * d * LANES * 4),
        name="ln_router",
    )(x2, y2, gate, g, b, sc, sh, wr, br)


MOE_TM = 256
MOE_HALF = D_EXPERT // 2
DMA_UNROLL = 8


def _row_copy(src_hbm, row, dst_vmem, r, sem):
    return pltpu.make_async_copy(src_hbm.at[pl.ds(row, 1), :], dst_vmem.at[pl.ds(r, 1), :], sem)


def _gather_start(idx_ref, base, src_hbm, dst_vmem, sem, nrows, stride=1):
    def issue(r, c):
        _row_copy(src_hbm, idx_ref[base + r * stride], dst_vmem, r, sem).start()
        return c

    lax.fori_loop(0, nrows, issue, 0, unroll=DMA_UNROLL)


def _gather_wait(src_hbm, dst_vmem, sem, nrows):
    def wait(r, c):
        _row_copy(src_hbm, 0, dst_vmem, r, sem).wait()
        return c

    lax.fori_loop(0, nrows, wait, 0, unroll=DMA_UNROLL)


MOE_NHALF = D_EXPERT // MOE_HALF


def _moe_ffn_kernel(te_ref, tv_ref, tok_ref, h_hbm, wg_hbm, wu_hbm, wd_hbm, o_ref,
                    xbuf, wgb, wub, wdb, xsems, wsems, *, layer):
    i = pl.program_id(0)
    n = pl.num_programs(0)
    slot = i % 2
    valid = tv_ref[i] > 0
    nxt = jnp.minimum(i + 1, n - 1)
    has_next = jnp.logical_and(i + 1 < n, tv_ref[nxt] > 0)
    e = te_ref[i]
    next_e = te_ref[nxt]
    first = jnp.logical_or(i == 0, e != te_ref[jnp.maximum(i - 1, 0)])
    last = jnp.logical_and(has_next, next_e != e)

    def gather(tile, s):
        _gather_start(tok_ref, tile * MOE_TM, h_hbm, xbuf.at[s], xsems.at[s], MOE_TM)

    def weight_copies(expert, hf):
        cs = pl.ds(hf * MOE_HALF, MOE_HALF)
        return (pltpu.make_async_copy(wg_hbm.at[layer, expert, :, cs], wgb.at[hf], wsems.at[hf, 0]),
                pltpu.make_async_copy(wu_hbm.at[layer, expert, :, cs], wub.at[hf], wsems.at[hf, 1]),
                pltpu.make_async_copy(wd_hbm.at[layer, expert, cs, :], wdb.at[hf], wsems.at[hf, 2]))

    @pl.when(jnp.logical_and(i == 0, valid))
    def _prime():
        gather(0, 0)
        for hf in range(MOE_NHALF):
            for cp in weight_copies(e, hf):
                cp.start()

    def gather_next_part(part, nparts):
        rows = MOE_TM // nparts
        for r in range(part * rows, (part + 1) * rows):
            _row_copy(h_hbm, tok_ref[nxt * MOE_TM + r], xbuf.at[1 - slot], r, xsems.at[1 - slot]).start()

    @pl.when(valid)
    def _compute():
        _gather_wait(h_hbm, xbuf.at[slot], xsems.at[slot], MOE_TM)
        xb = xbuf[slot].astype(bf16)
        y = None
        for hf in range(MOE_NHALF):
            @pl.when(first)
            def _arrived(hf=hf):
                for cp in weight_copies(e, hf):
                    cp.wait()

            gather_next_part(2 * hf, 2 * MOE_NHALF)
            gate = jnp.dot(xb, wgb[hf].astype(bf16), preferred_element_type=f32)
            up = jnp.dot(xb, wub[hf].astype(bf16), preferred_element_type=f32)
            hid = gate * _sigmoid(gate) * up
            gather_next_part(2 * hf + 1, 2 * MOE_NHALF)
            part = jnp.dot(hid.astype(bf16), wdb[hf].astype(bf16), preferred_element_type=f32)
            y = part if y is None else y + part

            @pl.when(last)
            def _reload(hf=hf):
                for cp in weight_copies(next_e, hf):
                    cp.start()

        o_ref[...] = y

        @pl.when(jnp.logical_not(has_next))
        def _drain():
            _gather_wait(h_hbm, xbuf.at[1 - slot], xsems.at[1 - slot], MOE_TM)

    @pl.when(jnp.logical_not(valid))
    def _empty():
        o_ref[...] = jnp.zeros_like(o_ref)


def _moe_ffn(h2, w_gate, w_up, w_down, layer, tile_e, tile_valid, row_tok, n_tiles):
    n, d = h2.shape
    tm = MOE_TM
    anyspace = pl.BlockSpec(memory_space=pl.ANY)
    grid_spec = pltpu.PrefetchScalarGridSpec(
        num_scalar_prefetch=3,
        grid=(n_tiles,),
        in_specs=[anyspace, anyspace, anyspace, anyspace],
        out_specs=pl.BlockSpec((tm, d), lambda i, te, tv, tok: (i, 0)),
        scratch_shapes=[pltpu.VMEM((2, tm, d), f32),
                        pltpu.VMEM((MOE_NHALF, d, MOE_HALF), f32),
                        pltpu.VMEM((MOE_NHALF, d, MOE_HALF), f32),
                        pltpu.VMEM((MOE_NHALF, MOE_HALF, d), f32),
                        pltpu.SemaphoreType.DMA((2,)),
                        pltpu.SemaphoreType.DMA((MOE_NHALF, 3))],
    )
    vmem = 3 * d * D_EXPERT * 4 + 3 * d * MOE_HALF * 2 + 2 * tm * d * 4 + tm * d * 2 + 4 * tm * d * 4
    return pl.pallas_call(
        functools.partial(_moe_ffn_kernel, layer=layer),
        grid_spec=grid_spec,
        out_shape=jax.ShapeDtypeStruct((n_tiles * tm, d), f32),
        compiler_params=_cparams(("arbitrary",), vmem),
        name="moe_ffn",
    )(tile_e, tile_valid, row_tok, h2, w_gate, w_up, w_down)


def _moe_combine_kernel(pos_ref, ys_hbm, rw_ref, x_ref, gate_ref, g_ref, b_ref, *rest, next_mod):
    if next_mod:
        sc_ref, sh_ref, o_ref, hi_ref, lo_ref, buf, sems = rest
    else:
        o_ref, buf, sems = rest
    i = pl.program_id(0)
    n = pl.num_programs(0)
    tm = x_ref.shape[0]
    slot = i % 2

    def start(tile, s):
        for k in range(2):
            _gather_start(pos_ref, 2 * tile * tm + k, ys_hbm, buf.at[s, k], sems.at[s, k], tm, stride=2)

    @pl.when(i == 0)
    def _first_gather():
        start(0, 0)

    @pl.when(i + 1 < n)
    def _next_gather():
        start(i + 1, 1 - slot)

    for k in range(2):
        _gather_wait(ys_hbm, buf.at[slot, k], sems.at[slot, k], tm)
    rw = rw_ref[...]
    y = rw[:, 0:1] * buf[slot, 0] + rw[:, 1:2] * buf[slot, 1]
    xn = _res_ln(x_ref[...], y, gate_ref[0], g_ref[...], b_ref[...])
    o_ref[...] = xn
    if next_mod:
        hi, lo = _split_bf16(xn * (1.0 + sc_ref[0]) + sh_ref[0])
        hi_ref[...] = hi
        lo_ref[...] = lo


def _moe_combine(pos, ys, rw, x2, gate, g, b, seq, next_mod=None):
    n, d = x2.shape
    tm = 256
    row = pl.BlockSpec((tm, d), lambda i, pos: (i, 0))
    per_batch = pl.BlockSpec((1, 1, d), lambda i, pos: (i * tm // seq, 0, 0))
    vec = pl.BlockSpec((1, d), lambda i, pos: (0, 0))
    extra = () if next_mod is None else tuple(next_mod)
    grid_spec = pltpu.PrefetchScalarGridSpec(
        num_scalar_prefetch=1,
        grid=(n // tm,),
        in_specs=[pl.BlockSpec(memory_space=pl.ANY), pl.BlockSpec((tm, LANES), lambda i, pos: (i, 0)),
                  row, per_batch, vec, vec] + [per_batch] * len(extra),
        out_specs=[row] * (1 + len(extra)) if extra else row,
        scratch_shapes=[pltpu.VMEM((2, 2, tm, d), f32), pltpu.SemaphoreType.DMA((2, 2))],
    )
    x_shape = jax.ShapeDtypeStruct((n, d), f32)
    mod_shape = jax.ShapeDtypeStruct((n, d), bf16)
    return pl.pallas_call(
        functools.partial(_moe_combine_kernel, next_mod=bool(extra)),
        grid_spec=grid_spec,
        out_shape=[x_shape, mod_shape, mod_shape] if extra else x_shape,
        compiler_params=_cparams(("arbitrary",), 14 * tm * d * 4),
        name="moe_combine",
    )(pos, ys, rw, x2, gate, g, b, *extra)


def _moe_plan(eid, n_tiles):
    tm = MOE_TM
    flat_e = eid.reshape(-1)
    onehot = (flat_e[:, None] == jnp.arange(N_EXPERTS, dtype=i32)[None, :]).astype(i32)
    csum = jnp.cumsum(onehot, axis=0)
    rank = jnp.sum((csum - onehot) * onehot, axis=1)
    counts = csum[-1]
    padded = ((counts + tm - 1) // tm) * tm
    ends = jnp.cumsum(padded)
    pos = ((ends - padded)[flat_e] + rank).astype(i32)
    tile_start = jnp.arange(n_tiles, dtype=i32) * tm
    tile_valid = (tile_start < ends[-1]).astype(i32)
    tile_e = jnp.searchsorted(ends, tile_start, side="right").astype(i32)
    last_valid = jnp.maximum(ends[-1] // tm - 1, 0)
    tile_e = jnp.where(tile_valid > 0, tile_e, tile_e[last_valid])
    row_tok = jnp.zeros((n_tiles * tm,), i32).at[pos].set(jnp.arange(flat_e.shape[0], dtype=i32) // 2)
    return pos, tile_e, tile_valid, row_tok


def _moe_layer(x2, y2, gate_m, ln_g, ln_b, scale_f, shift_f, gate_f, ln_g2, ln_b2,
               w_grp, b_grp, w_exp, b_exp, w_gate, w_up, w_down, layer, seq, next_mod=None):
    n, d = x2.shape
    pad = LANES - N_GROUPS - N_EXPERTS
    wr = jnp.concatenate([w_grp, w_exp, jnp.zeros((d, pad), f32)], axis=1)
    br = jnp.concatenate([b_grp, b_exp, jnp.zeros((pad,), f32)])[None, :]
    x1, h, rw, rid = _ln_router(x2, y2, gate_m, ln_g[None, :], ln_b[None, :], scale_f, shift_f, wr, br, seq)
    n_tiles = (2 * n) // MOE_TM + N_EXPERTS
    pos, tile_e, tile_valid, row_tok = _moe_plan(rid[:, :2], n_tiles)
    ys = _moe_ffn(h, w_gate, w_up, w_down, layer, tile_e, tile_valid, row_tok, n_tiles)
    return _moe_combine(pos, ys, rw, x1, gate_f, ln_g2[None, :], ln_b2[None, :], seq, next_mod)


def _rope_table_kernel(pos_ref, fr_ref, sg_ref, c_ref, s_ref):
    ang = pos_ref[...].astype(f32) * fr_ref[...]
    c_ref[...] = jnp.cos(ang)
    s_ref[...] = jnp.sin(ang) * sg_ref[...]


def _rope_tables(pos2, head_dim):
    m = pos2.shape[0]
    half = head_dim // 2
    freqs = ROPE_THETA ** (-jnp.arange(half, dtype=f32) / half)
    reps = LANES // head_dim
    fr = jnp.tile(jnp.concatenate([freqs, freqs]), reps)[None, :]
    sg = jnp.tile(jnp.concatenate([-jnp.ones((half,), f32), jnp.ones((half,), f32)]), reps)[None, :]
    tm = min(m, 1024)
    vec = pl.BlockSpec((1, LANES), lambda i: (0, 0))
    out = pl.BlockSpec((tm, LANES), lambda i: (i, 0))
    return pl.pallas_call(
        _rope_table_kernel,
        grid=(m // tm,),
        in_specs=[pl.BlockSpec((tm, 1), lambda i: (i, 0)), vec, vec],
        out_specs=[out, out],
        out_shape=[jax.ShapeDtypeStruct((m, LANES), f32)] * 2,
        compiler_params=_cparams(("parallel",), 8 * tm * LANES * 4),
        name="rope_tables",
    )(pos2, fr, sg)


def _proj_rope(h_hi, w_t, c_tab, s_tab, *, col_off, n, scale, tm, tn):
    tab = pl.BlockSpec((tm, LANES), lambda i, j: (i, 0))
    return _mm([h_hi], w_t, col_off=col_off, n=n, tm=tm, tn=tn, out_dtype=bf16, transposed=True,
               epilogue=functools.partial(_rope_epilogue, scale=scale),
               extra=(c_tab, s_tab), extra_specs=(tab, tab))


IDX_W = IDX_HEADS * IDX_DIM
IDX_RAW = IDX_W + LANES


IDX_K = 4 * IDX_DIM


def _idx_rope_kernel(x_ref, c_ref, s_ref, q_ref, k_ref, w_ref):
    c = c_ref[...]
    s = s_ref[...]
    lane = lax.broadcasted_iota(i32, c.shape, 1)
    first = (lane & (IDX_DIM // 2)) == 0
    low = lane < IDX_DIM

    def rope(blk):
        swapped = jnp.where(first, pltpu.roll(blk, LANES - IDX_DIM // 2, 1), pltpu.roll(blk, IDX_DIM // 2, 1))
        return blk * c + swapped * s

    def hi_lo(x):
        hi = x.astype(bf16).astype(f32)
        return hi, x - hi

    for k in range(IDX_W // LANES):
        hi, lo = hi_lo(rope(x_ref[:, k * LANES:(k + 1) * LANES]))
        hi_sw = pltpu.roll(hi, IDX_DIM, 1)
        lo_sw = pltpu.roll(lo, IDX_DIM, 1)
        base = 2 * k * IDX_K
        q_ref[:, base:base + LANES] = jnp.where(low, hi, hi_sw).astype(bf16)
        q_ref[:, base + LANES:base + IDX_K] = jnp.where(low, lo, 0.0).astype(bf16)
        q_ref[:, base + IDX_K:base + IDX_K + LANES] = jnp.where(low, hi_sw, hi).astype(bf16)
        q_ref[:, base + IDX_K + LANES:base + 2 * IDX_K] = jnp.where(low, lo_sw, 0.0).astype(bf16)
    kw = x_ref[:, IDX_W:IDX_RAW]
    hi, lo = hi_lo(rope(kw))
    k_ref[:, 0:LANES] = jnp.where(low, hi, pltpu.roll(lo, IDX_DIM, 1)).astype(bf16)
    k_ref[:, LANES:IDX_K] = jnp.where(low, hi, 0.0).astype(bf16)
    w_ref[...] = kw


def _idx_rope(raw, c_tab, s_tab):
    m = raw.shape[0]
    tm = min(m, 512)
    tab = pl.BlockSpec((tm, LANES), lambda i: (i, 0))
    return pl.pallas_call(
        _idx_rope_kernel,
        grid=(m // tm,),
        in_specs=[pl.BlockSpec((tm, IDX_RAW), lambda i: (i, 0)), tab, tab],
        out_specs=[pl.BlockSpec((tm, IDX_HEADS * IDX_K), lambda i: (i, 0)),
                   pl.BlockSpec((tm, IDX_K), lambda i: (i, 0)), tab],
        out_shape=[jax.ShapeDtypeStruct((m, IDX_HEADS * IDX_K), bf16), jax.ShapeDtypeStruct((m, IDX_K), bf16),
                   jax.ShapeDtypeStruct((m, LANES), f32)],
        compiler_params=_cparams(("parallel",), 8 * tm * IDX_RAW * 4 + 4 * tm * IDX_HEADS * IDX_K * 2),
        name="idx_rope",
    )(raw, c_tab, s_tab)


INT_MIN = -2 ** 31


DSA_T = 256
DSA_KG = 4
MASKED = -1e30


def _key_group(nkb):
    return DSA_KG if nkb % DSA_KG == 0 else 1


def _dsa_index_kernel(q_ref, k_ref, qw_ref, m_ref, key_scr, *, topk, scale):
    i = pl.program_id(1)
    nkb, kb, tq = key_scr.shape
    qw_t = qw_ref[...].T
    q_chunk = (i * tq + lax.broadcasted_iota(i32, (1, tq), 1)) // CHUNK
    krow = lax.broadcasted_iota(i32, (kb, tq), 0)

    def score_block(j, c):
        kblk = k_ref[pl.ds(pl.multiple_of(j * kb, kb), kb), :]
        score = jnp.zeros((kb, tq), f32)
        for hd in range(IDX_HEADS):
            rel = lax.dot_general(kblk, q_ref[:, hd * IDX_K:(hd + 1) * IDX_K], _NT, preferred_element_type=f32)
            score = score + jnp.maximum(rel, 0.0) * qw_t[IDX_DIM + hd:IDX_DIM + hd + 1, :]
        score = score * scale
        adm = j * kb + krow < (q_chunk + 1) * CHUNK
        bits = pltpu.bitcast(score, i32)
        key = jnp.where(bits < 0, bits ^ 0x7FFFFFFF, bits)
        key_scr[j] = jnp.where(adm, key, INT_MIN)
        return c

    lax.fori_loop(0, i + 1, score_block, 0)

    kg = _key_group(nkb)
    ngroups = (i + kg) // kg

    def fill(j, c):
        key_scr[j] = jnp.full((kb, tq), INT_MIN, i32)
        return c

    lax.fori_loop(i + 1, ngroups * kg, fill, 0)

    def count_ge(cand):
        def grp(g, acc):
            for u in range(kg):
                ge = (key_scr[g * kg + u] >= cand).astype(f32)
                for r in range(kb // SUBLANES):
                    acc = acc + ge[r * SUBLANES:(r + 1) * SUBLANES, :]
            return acc

        acc = lax.fori_loop(0, ngroups, grp, jnp.zeros((SUBLANES, tq), f32))
        return jnp.sum(acc, axis=0, keepdims=True)

    kf = jnp.float32(topk)
    cur = jnp.where(count_ge(jnp.zeros((1, tq), i32)) >= kf, 0, INT_MIN).astype(i32)

    def bisect(it, cur):
        cand = cur + jnp.left_shift(jnp.int32(1), 30 - it)
        return jnp.where(count_ge(cand) >= kf, cand, cur)

    thr = jnp.maximum(lax.fori_loop(0, 31, bisect, cur), INT_MIN + 1)

    def write(j, c):
        m_ref[0, j] = jnp.where(key_scr[j] >= thr, 0.0, MASKED).T
        return c

    def clear(j, c):
        m_ref[0, j] = jnp.full((tq, kb), MASKED, f32)
        return c

    lax.fori_loop(0, i + 1, write, 0)
    lax.fori_loop(i + 1, nkb, clear, 0)


def _dsa_index(iq, ik, iw, batch, seq):
    tq = min(seq, DSA_T)
    nq = seq // tq
    topk = min(TOPK_MAX, seq // 4)
    scale = IDX_DIM ** -0.5 * IDX_HEADS ** -0.5
    return pl.pallas_call(
        functools.partial(_dsa_index_kernel, topk=topk, scale=scale),
        grid=(batch, nq),
        in_specs=[pl.BlockSpec((tq, IDX_HEADS * IDX_K), lambda b, i: (b * nq + i, 0)),
                  pl.BlockSpec((seq, IDX_K), lambda b, i: (b, 0)),
                  pl.BlockSpec((tq, LANES), lambda b, i: (b * nq + i, 0))],
        out_specs=pl.BlockSpec((1, nq, tq, tq), lambda b, i: (b * nq + i, 0, 0, 0)),
        out_shape=jax.ShapeDtypeStruct((batch * nq, nq, tq, tq), f32),
        scratch_shapes=[pltpu.VMEM((nq, tq, tq), i32)],
        compiler_params=_cparams(("parallel", "parallel"),
                                 8 * tq * seq * 4 + 4 * seq * IDX_K * 2 + 4 * tq * IDX_HEADS * IDX_K * 2),
        name="dsa_index",
    )(iq, ik, iw)


ATT_HG = 4


def _dsa_attn_kernel(q_ref, k_ref, v_ref, m_ref, o_ref):
    i = pl.program_id(2)
    _, nkb, tq, kb = m_ref.shape
    kg = _key_group(nkb)

    def group(g, carry):
        rows = pl.ds(pl.multiple_of(g * (kg * kb), kg * kb), kg * kb)
        parts = [m_ref[0, g * kg + u] for u in range(kg)]
        bias = jnp.concatenate(parts, axis=1) if kg > 1 else parts[0]
        out = []
        for h in range(ATT_HG):
            mx, den, acc = carry[h]
            sl = slice(h * HEAD_DIM, (h + 1) * HEAD_DIM)
            logits = lax.dot_general(q_ref[:, sl], k_ref[rows, sl], _NT, preferred_element_type=f32) + bias
            mx_new = jnp.maximum(mx, jnp.max(logits, axis=1, keepdims=True))
            alpha = jnp.exp(mx - mx_new)
            p = jnp.exp(logits - mx_new)
            den = alpha * den + jnp.sum(p, axis=1, keepdims=True)
            acc = alpha * acc + jnp.dot(p.astype(bf16), v_ref[rows, sl], preferred_element_type=f32)
            out.append((mx_new, den, acc))
        return tuple(out)

    init = tuple((jnp.full((tq, 1), MASKED, f32), jnp.zeros((tq, 1), f32), jnp.zeros((tq, HEAD_DIM), f32))
                 for _ in range(ATT_HG))
    final = lax.fori_loop(0, (i + kg) // kg, group, init)
    for h in range(ATT_HG):
        _, den, acc = final[h]
        o_ref[:, h * HEAD_DIM:(h + 1) * HEAD_DIM] = (acc / den).astype(o_ref.dtype)


def _dsa_attn(q, k, v, mask, batch, seq):
    m, w = q.shape
    _, nq, tq, _ = mask.shape
    gw = ATT_HG * HEAD_DIM
    qspec = pl.BlockSpec((tq, gw), lambda b, g, i: (b * nq + i, g))
    kvspec = pl.BlockSpec((seq, gw), lambda b, g, i: (b, g))
    vmem = 2 * (2 * tq * gw * 2 + 2 * seq * gw * 2 + tq * seq * 4) + 8 * tq * DSA_KG * tq * 4
    return pl.pallas_call(
        _dsa_attn_kernel,
        grid=(batch, w // gw, nq),
        in_specs=[qspec, kvspec, kvspec, pl.BlockSpec((1, nq, tq, tq), lambda b, g, i: (b * nq + i, 0, 0, 0))],
        out_specs=qspec,
        out_shape=jax.ShapeDtypeStruct((m, w), bf16),
        compiler_params=_cparams(("parallel", "parallel", "arbitrary"), vmem),
        name="dsa_attn",
    )(q, k, v, mask)


RET_HG = 4


def _retention_kernel(q_ref, k_ref, v_ref, g_ref, gn_ref, lg_ref, o_ref, state, decay_s):
    c = pl.program_id(2)
    t = q_ref.shape[0]

    @pl.when(c == 0)
    def _init():
        state[...] = jnp.zeros_like(state)
        ri = lax.broadcasted_iota(i32, (t, t), 0)
        ci = lax.broadcasted_iota(i32, (t, t), 1)
        diff = (ri - ci).astype(f32)
        for h in range(RET_HG):
            lg = lg_ref[:, h * HEAD_DIM:h * HEAD_DIM + 1]
            decay_s[h] = jnp.where(diff >= 0, jnp.exp(lg * jnp.maximum(diff, 0.0)), 0.0)

    pos = lax.broadcasted_iota(i32, (t, 1), 0).astype(f32)
    for h in range(RET_HG):
        sl = slice(h * HEAD_DIM, (h + 1) * HEAD_DIM)
        lg = lg_ref[:, h * HEAD_DIM:h * HEAD_DIM + 1]
        q = q_ref[:, sl]
        k = k_ref[:, sl]
        v = v_ref[:, sl]
        scores = lax.dot_general(q, k, _NT, preferred_element_type=f32) * decay_s[h]
        inner = jnp.dot(scores.astype(bf16), v, preferred_element_type=f32)
        st = state[h]
        cross = jnp.dot(q, st.astype(bf16), preferred_element_type=f32) * jnp.exp(lg * (pos + 1.0))
        kz = k.astype(f32) * jnp.exp(lg * (t - 1.0 - pos))
        kv = jnp.dot(kz.T.astype(bf16), v, preferred_element_type=f32)
        state[h] = jnp.exp(lg * t) * st + kv
        ret = inner + cross
        mu = jnp.mean(ret, axis=1, keepdims=True)
        rc = ret - mu
        var = jnp.mean(rc * rc, axis=1, keepdims=True)
        gate = g_ref[:, sl]
        out = rc * lax.rsqrt(var + LN_EPS) * gn_ref[:, sl] * (gate * _sigmoid(gate))
        o_ref[:, sl] = out.astype(o_ref.dtype)


def _retention(q, k, v, g, gn_g, batch, seq):
    m, w = q.shape
    heads = w // HEAD_DIM
    t = min(seq, 256)
    nc = seq // t
    gw = RET_HG * HEAD_DIM
    log_g = jnp.log1p(-jnp.exp2(-5.0 - jnp.arange(heads, dtype=f32)))
    lg = jnp.repeat(log_g, HEAD_DIM)[None, :]
    blk = pl.BlockSpec((t, gw), lambda b, gi, c: (b * nc + c, gi))
    vec = pl.BlockSpec((1, gw), lambda b, gi, c: (0, gi))
    return pl.pallas_call(
        _retention_kernel,
        grid=(batch, w // gw, nc),
        in_specs=[blk, blk, blk, blk, vec, vec],
        out_specs=blk,
        out_shape=jax.ShapeDtypeStruct((m, w), bf16),
        scratch_shapes=[pltpu.VMEM((RET_HG, HEAD_DIM, HEAD_DIM), f32), pltpu.VMEM((RET_HG, t, t), f32)],
        compiler_params=_cparams(("parallel", "parallel", "arbitrary"), 16 * t * gw * 4 + (8 + RET_HG) * t * t * 4),
        name="retention",
    )(q, k, v, g, gn_g[None, :], lg)


def _mixer_sparse_retention(h_hi, h_lo, pos2, w_in, w_out, gn_g, batch, seq):
    m, d = h_hi.shape
    gw = d // 2
    tm = min(m, 1024)
    tn = 512
    c128, s128 = _rope_tables(pos2, HEAD_DIM)
    c64, s64 = _rope_tables(pos2, IDX_DIM)
    w_t = jnp.swapaxes(w_in, 0, 1)
    rope = functools.partial(_proj_rope, h_hi, w_t, c_tab=c128, s_tab=s128, n=gw, tm=tm, tn=tn)
    plain = functools.partial(_mm, [h_hi], w_t, n=gw, tm=tm, tn=tn, transposed=True)
    aq = rope(col_off=0, scale=HEAD_DIM ** -0.5)
    ak = rope(col_off=gw, scale=1.0)
    av = plain(col_off=2 * gw, out_dtype=bf16)
    idx_raw = _mm3(h_hi, h_lo, w_t, col_off=3 * gw, n=IDX_RAW, tm=min(m, 512), tn=IDX_RAW // 3, transposed=True)
    iq, ik, iw = _idx_rope(idx_raw, c64, s64)
    mask = _dsa_index(iq, ik, iw, batch, seq)
    a_out = _dsa_attn(aq, ak, av, mask, batch, seq)
    b_off = 3 * gw + IDX_W + IDX_DIM + IDX_HEADS
    bq = rope(col_off=b_off, scale=1.0)
    bk = rope(col_off=b_off + gw, scale=HEAD_DIM ** -0.5)
    bv = plain(col_off=b_off + 2 * gw, out_dtype=bf16)
    bg = plain(col_off=b_off + 3 * gw, out_dtype=f32)
    b_out = _retention(bq, bk, bv, bg, gn_g, batch, seq)
    return _mm([a_out, b_out], w_out, col_off=0, n=d, tm=tm, tn=tn, out_dtype=f32)


MXU_N = 256


def _seg_sum64(x):
    r = lax.broadcasted_iota(i32, (MXU_N, MXU_N), 0) // C_HEAD_DIM
    c = lax.broadcasted_iota(i32, (MXU_N, MXU_N), 1) // C_HEAD_DIM
    ones = (r == c).astype(bf16)
    hi = x.astype(bf16)
    r1 = x - hi.astype(f32)
    mid = r1.astype(bf16)
    lo = (r1 - mid.astype(f32)).astype(bf16)
    cols = []
    for k in range(x.shape[1] // MXU_N):
        sl = slice(k * MXU_N, (k + 1) * MXU_N)
        s = jnp.dot(hi[:, sl], ones, preferred_element_type=f32)
        s += jnp.dot(mid[:, sl], ones, preferred_element_type=f32)
        s += jnp.dot(lo[:, sl], ones, preferred_element_type=f32)
        cols.append(s)
    return jnp.concatenate(cols, axis=1) if len(cols) > 1 else cols[0]


def _neg_softplus_neg(z):
    return jnp.minimum(z, 0.0) - jnp.log(1.0 + jnp.exp(-jnp.abs(z)))


def _rwkv_prep_kernel(p_ref, pp_ref, mu_ref, w0_ref, wup_ref, a0_ref, aup_ref, gup_ref, ka_ref, rk_ref,
                      r_o, w_o, k_o, a_o, v_o, g_o, bon_o, *, seq, gw):
    i = pl.program_id(0)
    tm = p_ref.shape[0]
    p = p_ref[...]
    prev_row = jnp.where((i * tm) % seq == 0, 0.0, pp_ref[SUBLANES - 1:SUBLANES, :])
    row = lax.broadcasted_iota(i32, (tm, 1), 0)
    shifted = jnp.where(row == 0, prev_row, pltpu.roll(p, 1, 0))
    pm = p + (shifted - p) * mu_ref[...]
    r = pm[:, 0:gw]
    k = pm[:, gw:2 * gw]
    v = pm[:, 2 * gw:3 * gw]
    o = 3 * gw
    dw = pm[:, o:o + C_DECAY_RANK]
    da = pm[:, o + C_DECAY_RANK:o + C_DECAY_RANK + C_ICLR_RANK]
    dg = pm[:, o + C_DECAY_RANK + C_ICLR_RANK:]
    w_log = _neg_softplus_neg(w0_ref[...] + _dot3(jnp.tanh(dw), wup_ref[...])) - 0.5
    decay = jnp.exp(-jnp.exp(w_log))
    a = _sigmoid(a0_ref[...] + _dot3(da, aup_ref[...]))
    g = _dot3(_sigmoid(dg), gup_ref[...])
    k2 = k * (1.0 + (a - 1.0) * ka_ref[...])
    r_o[...] = r
    w_o[...] = decay
    k_o[...] = k
    a_o[...] = a
    v_o[...] = v
    g_o[...] = g
    bon_o[...] = _seg_sum64(r * k2 * rk_ref[...]) * v


def _rwkv_prep(pc, mu, w0, w_up, a0, a_up, g_up, k_a, r_k, seq):
    m, cc = pc.shape
    gw = w0.shape[0]
    tm = 128
    nsub = tm // SUBLANES
    vec = pl.BlockSpec((1, gw), lambda i: (0, 0))
    out = pl.BlockSpec((tm, gw), lambda i: (i, 0))
    full = lambda a: pl.BlockSpec(a.shape, lambda i: (0, 0))
    return pl.pallas_call(
        functools.partial(_rwkv_prep_kernel, seq=seq, gw=gw),
        grid=(m // tm,),
        in_specs=[pl.BlockSpec((tm, cc), lambda i: (i, 0)),
                  pl.BlockSpec((SUBLANES, cc), lambda i: (jnp.maximum(i * nsub - 1, 0), 0)),
                  pl.BlockSpec((1, cc), lambda i: (0, 0)),
                  vec, full(w_up), vec, full(a_up), full(g_up), vec, vec],
        out_specs=[out] * 7,
        out_shape=[jax.ShapeDtypeStruct((m, gw), f32)] * 7,
        compiler_params=_cparams(("parallel",), 2 * tm * cc * 4 + 2 * 7 * tm * gw * 4 + 16 * tm * gw * 4),
        name="rwkv_prep",
    )(pc, pc, mu[None, :], w0[None, :], w_up, a0[None, :], a_up, g_up, k_a[None, :], r_k.reshape(1, -1))


SCAN_T = 64
SCAN_G = 8


def _rwkv_scan_kernel(w_ref, a_ref, k_ref, r_ref, v_ref, kkw_ref, kaw_ref, y_ref,
                      z_ref, w_s, kk_s, ka_s, k_s, r_s, v_s, y_s):
    c = pl.program_id(0)
    batch, steps, width = w_ref.shape
    nslab = z_ref.shape[0]
    heads = width // C_HEAD_DIM
    half = batch * heads

    @pl.when(c == 0)
    def _init():
        z_ref[...] = jnp.zeros_like(z_ref)

    kkw = kkw_ref[...]
    kaw = kaw_ref[...]

    def by_head(ref, rows, copies):
        per_batch = [ref[b, rows, :].reshape(SCAN_G, heads, C_HEAD_DIM) for b in range(batch)]
        return jnp.concatenate(per_batch * copies, axis=1)

    def key_tiles(ref, rows):
        return jnp.swapaxes(by_head(ref, rows, 2), 1, 2)

    def derive(g, carry):
        rows = pl.ds(pl.multiple_of(g * SCAN_G, SCAN_G), SCAN_G)
        a = key_tiles(a_ref, rows)
        kraw = key_tiles(k_ref, rows)
        kk = kraw * kkw
        kk = kk * lax.rsqrt(jnp.maximum(jnp.sum(kk * kk, axis=1, keepdims=True), 1e-24))
        w_s[rows] = key_tiles(w_ref, rows)
        r_s[rows] = key_tiles(r_ref, rows)
        kk_s[rows] = kk
        ka_s[rows] = kk * a
        k_s[rows] = kraw * (1.0 + (a - 1.0) * kaw)
        vt = jnp.swapaxes(by_head(v_ref, rows, 1), 1, 2)
        v_s[rows] = jnp.concatenate([vt[:, :nslab, :], vt[:, nslab:, :]], axis=2)
        return carry

    lax.fori_loop(0, steps // SCAN_G, derive, 0)

    def step(t, carry):
        w = w_s[t]
        kk = kk_s[t]
        ka = ka_s[t]
        k = k_s[t]
        r = r_s[t]
        vrows = v_s[t]
        for s in range(nslab):
            z = z_ref[s]
            sk = jnp.sum(z * kk, axis=0, keepdims=True)
            zn = z * w - ka * sk + k * vrows[s:s + 1, :]
            z_ref[s] = zn
            y_s[t, s:s + 1, :] = jnp.sum(zn * r, axis=0, keepdims=True)
        return carry

    lax.fori_loop(0, steps, step, 0)

    def restore(g, carry):
        rows = pl.ds(pl.multiple_of(g * SCAN_G, SCAN_G), SCAN_G)
        y = y_s[rows]
        yt = jnp.concatenate([y[:, :, :half], y[:, :, half:]], axis=1)
        yh = jnp.swapaxes(yt, 1, 2)
        for b in range(batch):
            y_ref[b, rows, :] = yh[:, b * heads:(b + 1) * heads, :].reshape(SCAN_G, width)
        return carry

    lax.fori_loop(0, steps // SCAN_G, restore, 0)


def _rwkv_scan(w, a, k, r, v, k_k, k_a, batch, seq):
    width = w.shape[1]
    heads = width // C_HEAD_DIM
    assert 2 * batch * heads == LANES
    nslab = C_HEAD_DIM // 2
    steps = min(seq, SCAN_T)
    consts = [jnp.tile(p.reshape(heads, C_HEAD_DIM).T, (1, 2 * batch)) for p in (k_k, k_a)]
    nat = pl.BlockSpec((batch, steps, width), lambda c: (0, c, 0))
    const_spec = pl.BlockSpec((C_HEAD_DIM, LANES), lambda c: (0, 0))
    tile_bytes = steps * C_HEAD_DIM * LANES * 4
    y = pl.pallas_call(
        _rwkv_scan_kernel,
        grid=(seq // steps,),
        in_specs=[nat] * 5 + [const_spec, const_spec],
        out_specs=nat,
        out_shape=jax.ShapeDtypeStruct((batch, seq, width), f32),
        scratch_shapes=[pltpu.VMEM((nslab, C_HEAD_DIM, LANES), f32)]
        + [pltpu.VMEM((steps, C_HEAD_DIM, LANES), f32)] * 5
        + [pltpu.VMEM((steps, nslab, LANES), f32)] * 2,
        compiler_params=_cparams(("arbitrary",), 6 * tile_bytes + 2 * 6 * batch * steps * width * 4
                                 + 8 * tile_bytes),
        name="rwkv_scan",
    )(*(x.reshape(batch, seq, width) for x in (w, a, k, r, v)), *consts)
    return y.reshape(batch * seq, width)


def _rwkv_post_kernel(y_ref, g_ref, bon_ref, lng_ref, lnb_ref, o_ref):
    y = y_ref[...]
    mu = _seg_sum64(y) * (1.0 / C_HEAD_DIM)
    yc = y - mu
    var = _seg_sum64(yc * yc) * (1.0 / C_HEAD_DIM)
    yn = yc * lax.rsqrt(var + C_EPS) * lng_ref[...] + lnb_ref[...]
    o_ref[...] = ((yn + bon_ref[...]) * g_ref[...]).astype(o_ref.dtype)


def _rwkv_post(y, g, bonus, ln_g, ln_b):
    m, gw = y.shape
    tm = 256
    blk = pl.BlockSpec((tm, gw), lambda i: (i, 0))
    vec = pl.BlockSpec((1, gw), lambda i: (0, 0))
    return pl.pallas_call(
        _rwkv_post_kernel,
        grid=(m // tm,),
        in_specs=[blk, blk, blk, vec, vec],
        out_specs=blk,
        out_shape=jax.ShapeDtypeStruct((m, gw), bf16),
        compiler_params=_cparams(("parallel",), 16 * tm * gw * 4),
        name="rwkv_post",
    )(y, g, bonus, ln_g[None, :], ln_b[None, :])


GELU_C = 0.7978845608028654


def _lru_kernel(px_ref, pg_ref, cw_ref, cb_ref, wa_ref, ba_ref, wx_ref, bx_ref, lam_ref, o_ref,
                tail, hcar, a_s, b_s):
    c = pl.program_id(1)
    t = px_ref.shape[0]

    @pl.when(c == 0)
    def _init():
        tail[...] = jnp.zeros_like(tail)
        hcar[...] = jnp.zeros_like(hcar)

    x = px_ref[...]
    ext = jnp.concatenate([tail[...], x], axis=0)
    xc = cb_ref[...]
    for j in range(D_CONV):
        off = SUBLANES - (D_CONV - 1) + j
        xc = xc + cw_ref[j:j + 1, :] * ext[off:off + t, :]
    tail[...] = x[t - SUBLANES:, :]
    lam = lam_ref[...]
    sp = jnp.maximum(-lam, 0.0) + jnp.log(1.0 + jnp.exp(-jnp.abs(lam)))
    bw = wa_ref.shape[1]
    for n in range(wa_ref.shape[0]):
        sl = slice(n * bw, (n + 1) * bw)
        xb = xc[:, sl]
        rg = _sigmoid(_dot3(xb, wa_ref[n]) + ba_ref[:, sl])
        ig = _sigmoid(_dot3(xb, wx_ref[n]) + bx_ref[:, sl])
        log_a = -LRU_C * rg * sp[:, sl]
        a_s[:, sl] = jnp.exp(log_a)
        th = jnp.tanh(log_a)
        one_minus_a2 = -2.0 * th / (1.0 - th)
        b_s[:, sl] = jnp.sqrt(one_minus_a2) * (ig * xb)

    sub = lax.broadcasted_iota(i32, (SUBLANES, a_s.shape[1]), 0)

    def rows8(g, h):
        r0 = pl.multiple_of(g * SUBLANES, SUBLANES)
        a = a_s[pl.ds(r0, SUBLANES), :]
        b = b_s[pl.ds(r0, SUBLANES), :]
        for s in (1, 2, 4):
            b = a * jnp.where(sub >= s, pltpu.roll(b, s, 0), 0.0) + b
            a = a * jnp.where(sub >= s, pltpu.roll(a, s, 0), 1.0)
        hs = a * h + b
        b_s[pl.ds(r0, SUBLANES), :] = hs
        return hs[SUBLANES - 1:SUBLANES, :]

    hcar[...] = lax.fori_loop(0, t // SUBLANES, rows8, hcar[...])
    gate = pg_ref[...]
    gelu = 0.5 * gate * (1.0 + jnp.tanh(GELU_C * (gate + 0.044715 * (gate * gate * gate))))
    o_ref[...] = (b_s[...] * gelu).astype(o_ref.dtype)


def _lru(px, pg, conv_w, conv_b, w_a, b_a, w_x, b_x, lam, batch, seq):
    m, w = px.shape
    t = min(seq, 256)
    nc = seq // t
    blk = pl.BlockSpec((t, w), lambda b, c: (b * nc + c, 0))
    vec = pl.BlockSpec((1, w), lambda b, c: (0, 0))
    wblk = pl.BlockSpec(w_a.shape, lambda b, c: (0, 0, 0))
    return pl.pallas_call(
        _lru_kernel,
        grid=(batch, nc),
        in_specs=[blk, blk, pl.BlockSpec((D_CONV, w), lambda b, c: (0, 0)), vec, wblk, vec, wblk, vec, vec],
        out_specs=blk,
        out_shape=jax.ShapeDtypeStruct((m, w), bf16),
        scratch_shapes=[pltpu.VMEM((SUBLANES, w), f32), pltpu.VMEM((1, w), f32),
                        pltpu.VMEM((t, w), f32), pltpu.VMEM((t, w), f32)],
        compiler_params=_cparams(("parallel", "arbitrary"), 16 * t * w * 4),
        name="lru",
    )(px, pg, conv_w, conv_b[None, :], w_a, b_a[None, :], w_x, b_x[None, :], lam[None, :])


def _mixer_rwkv_lru(h_hi, w_in, w_out, mu, w0, w_up, a0, a_up, g_up, k_k, k_a, r_k, ln_g, ln_b,
                    conv_w, conv_b, w_a, b_a, w_x, b_x, lam, batch, seq):
    m, d = h_hi.shape
    gw = d // 2
    c_cols = 3 * gw + C_DECAY_RANK + C_ICLR_RANK + C_GATE_RANK
    tm = min(m, 1024)
    tn = 512
    proj = functools.partial(_mm, [h_hi], w_in, tm=tm, tn=tn, out_dtype=f32)
    pc = proj(col_off=0, n=c_cols)
    pg = proj(col_off=c_cols, n=gw)
    px = proj(col_off=c_cols + gw, n=gw)
    r, w, k, a, v, g, bonus = _rwkv_prep(pc, mu, w0, w_up, a0, a_up, g_up, k_a, r_k, seq)
    y = _rwkv_scan(w, a, k, r, v, k_k, k_a, batch, seq)
    c_out = _rwkv_post(y, g, bonus, ln_g, ln_b)
    d_out = _lru(px, pg, conv_w, conv_b, w_a, b_a, w_x, b_x, lam, batch, seq)
    return _mm([c_out, d_out], w_out, col_off=0, n=d, tm=tm, tn=tn, out_dtype=f32)


ADA_ROWS = 16


def _ada_kernel(c_ref, w_ref, b_ref, o_ref):
    c = c_ref[...]
    s = (c * _sigmoid(c)).astype(bf16)
    o_ref[...] = jnp.dot(s, w_ref[...].astype(bf16), preferred_element_type=f32) + b_ref[...]


def _ada(c, ada_w, ada_b):
    batch, d = c.shape
    n = ada_w.shape[1]
    tn = 512
    cp = jnp.zeros((ADA_ROWS, d), f32).at[:batch].set(c)
    out = pl.pallas_call(
        _ada_kernel,
        grid=(n // tn,),
        in_specs=[pl.BlockSpec((ADA_ROWS, d), lambda j: (0, 0)),
                  pl.BlockSpec((d, tn), lambda j: (0, j)),
                  pl.BlockSpec((1, tn), lambda j: (0, j))],
        out_specs=pl.BlockSpec((ADA_ROWS, tn), lambda j: (0, j)),
        out_shape=jax.ShapeDtypeStruct((ADA_ROWS, n), f32),
        compiler_params=_cparams(("parallel",), 3 * d * tn * 4),
        name="ada",
    )(cp, ada_w, ada_b[None, :])
    return out[:batch]


def kernel(x, c, positions, ada_w, ada_b, ada_table, ln_g, ln_b, ab_w_in, ab_w_out, ret_gn_g, cd_w_in, cd_w_out, rwkv_mu, rwkv_w0, rwkv_w_up, rwkv_a0, rwkv_a_up, rwkv_g_up, rwkv_k_k, rwkv_k_a, rwkv_r_k, rwkv_ln_g, rwkv_ln_b, lru_conv_w, lru_conv_b, lru_w_a, lru_b_a, lru_w_x, lru_b_x, lru_lambda, moe_w_grp, moe_b_grp, moe_w_exp, moe_b_exp, moe_w_gate, moe_w_up, moe_w_down):
    batch, seq, d = x.shape
    x2 = x.reshape(batch * seq, d)
    pos2 = positions.reshape(batch * seq, 1)
    ada = _ada(c, ada_w, ada_b).reshape(batch, 6, 1, d)
    mods = [ada + ada_table[layer][None, :, None, :] for layer in range(DEPTH)]
    h_hi, h_lo = _modcast(x2, mods[0][:, 1], mods[0][:, 0], seq)
    for layer in range(DEPTH):
        shift_m, scale_m, gate_m, shift_f, scale_f, gate_f = (mods[layer][:, i] for i in range(6))
        j = layer // 2
        if layer % 2 == 0:
            y = _mixer_sparse_retention(h_hi, h_lo, pos2, ab_w_in[j], ab_w_out[j], ret_gn_g[j], batch, seq)
        else:
            y = _mixer_rwkv_lru(h_hi, cd_w_in[j], cd_w_out[j], rwkv_mu[j], rwkv_w0[j], rwkv_w_up[j],
                                rwkv_a0[j], rwkv_a_up[j], rwkv_g_up[j], rwkv_k_k[j], rwkv_k_a[j],
                                rwkv_r_k[j], rwkv_ln_g[j], rwkv_ln_b[j], lru_conv_w[j], lru_conv_b[j],
                                lru_w_a[j], lru_b_a[j], lru_w_x[j], lru_b_x[j], lru_lambda[j], batch, seq)
        last = layer + 1 == DEPTH
        next_mod = None if last else (mods[layer + 1][:, 1], mods[layer + 1][:, 0])
        out = _moe_layer(x2, y, gate_m, ln_g[layer, 0], ln_b[layer, 0], scale_f, shift_f, gate_f,
                         ln_g[layer, 1], ln_b[layer, 1], moe_w_grp[layer], moe_b_grp[layer],
                         moe_w_exp[layer], moe_b_exp[layer], moe_w_gate, moe_w_up, moe_w_down, layer, seq,
                         next_mod)
        x2, h_hi, h_lo = (out, None, None) if last else out
    return x2.reshape(batch, seq, d)
```

```python
import functools

import jax
import jax.numpy as jnp
from jax import lax
from jax.experimental import pallas as pl
from jax.experimental.pallas import tpu as pltpu

f32 = jnp.float32
bf16 = jnp.bfloat16
i32 = jnp.int32

DEPTH = 2
CHUNK = 64
ROPE_THETA = 10000.0
LN_EPS = 1e-5
ALPHA = (2 * DEPTH) ** 0.25
HEAD_DIM = 128
IDX_HEADS = 16
IDX_DIM = 64
TOPK_MAX = 256
C_HEAD_DIM = 64
C_DECAY_RANK = 128
C_ICLR_RANK = 128
C_GATE_RANK = 256
C_EPS = 64e-5
D_BLOCKS = 16
D_CONV = 4
LRU_C = 8.0
N_GROUPS = 4
EXPERTS_PER_GROUP = 8
N_EXPERTS = N_GROUPS * EXPERTS_PER_GROUP
D_EXPERT = 512

LANES = 128
SUBLANES = 8
VMEM_BYTES_V7X = 64 * 1024 * 1024
VMEM_HEADROOM = 8 * 1024 * 1024


def _cparams(semantics, vmem_bytes):
    limit = min(int(vmem_bytes) + VMEM_HEADROOM, VMEM_BYTES_V7X - VMEM_HEADROOM)
    return pltpu.CompilerParams(dimension_semantics=semantics, vmem_limit_bytes=limit)


def _split_bf16(a):
    hi = a.astype(bf16)
    lo = (a - hi.astype(f32)).astype(bf16)
    return hi, lo


def _dot3(a, b, dims=(((1,), (0,)), ((), ()))):
    ah, al = _split_bf16(a)
    bh, bl = _split_bf16(b)
    dg = functools.partial(lax.dot_general, dimension_numbers=dims, preferred_element_type=f32)
    return dg(ah, bh) + dg(ah, bl) + dg(al, bh)


def _sigmoid(x):
    return 1.0 / (1.0 + jnp.exp(-x))


def _modcast_kernel(x_ref, sc_ref, sh_ref, hi_ref, lo_ref):
    h = x_ref[...] * (1.0 + sc_ref[0]) + sh_ref[0]
    hi, lo = _split_bf16(h)
    hi_ref[...] = hi
    lo_ref[...] = lo


def _modcast(x2, sc, sh, seq):
    m, k = x2.shape
    tm = 256
    row = pl.BlockSpec((tm, k), lambda i: (i, 0))
    per_batch = pl.BlockSpec((1, 1, k), lambda i: (i * tm // seq, 0, 0))
    return pl.pallas_call(
        _modcast_kernel,
        grid=(m // tm,),
        in_specs=[row, per_batch, per_batch],
        out_specs=[row, row],
        out_shape=[jax.ShapeDtypeStruct((m, k), bf16)] * 2,
        compiler_params=_cparams(("parallel",), 2 * tm * k * (4 + 2 + 2)),
        name="modcast",
    )(x2, sc, sh)


_NN = (((1,), (0,)), ((), ()))
_NT = (((1,), (1,)), ((), ()))


def _mm_kernel(*refs, nparts, epilogue, dims):
    acc = None
    for p in range(nparts):
        d = lax.dot_general(refs[p][...], refs[nparts + p][...].astype(bf16), dims, preferred_element_type=f32)
        acc = d if acc is None else acc + d
    epilogue(acc, *refs[2 * nparts:])


def _mm3_kernel(xh_ref, xl_ref, w_ref, o_ref, *, dims):
    wh, wl = _split_bf16(w_ref[...])
    xh = xh_ref[...]
    dg = functools.partial(lax.dot_general, dimension_numbers=dims, preferred_element_type=f32)
    o_ref[...] = dg(xh, wh) + dg(xh, wl) + dg(xl_ref[...], wh)


def _w_spec(kp, tn, k_off, col_off, transposed):
    if transposed:
        assert col_off % SUBLANES == 0 and tn % SUBLANES == 0
        return pl.BlockSpec((pl.Element(tn), pl.Element(kp)),
                            lambda i, j: (pl.multiple_of(col_off + j * tn, SUBLANES), k_off))
    assert col_off % tn == 0 and k_off % kp == 0
    return pl.BlockSpec((kp, tn), lambda i, j: (k_off // kp, col_off // tn + j))


def _store_epilogue(acc, o_ref):
    o_ref[...] = acc.astype(o_ref.dtype)


def _rope_epilogue(acc, c_ref, s_ref, o_ref, *, scale):
    c = c_ref[...]
    s = s_ref[...]
    for k in range(acc.shape[1] // LANES):
        blk = acc[:, k * LANES:(k + 1) * LANES]
        rot = blk * c + pltpu.roll(blk, LANES // 2, 1) * s
        if scale != 1.0:
            rot = rot * scale
        o_ref[:, k * LANES:(k + 1) * LANES] = rot.astype(o_ref.dtype)


def _mm(xs, w, *, col_off, n, tm, tn, out_dtype, epilogue=_store_epilogue, extra=(), extra_specs=(),
        transposed=False):
    m = xs[0].shape[0]
    kp = xs[0].shape[1]
    nparts = len(xs)
    assert all(x.shape == (m, kp) for x in xs) and w.shape[1 if transposed else 0] == nparts * kp
    assert m % tm == 0 and n % tn == 0
    x_specs = [pl.BlockSpec((tm, kp), lambda i, j: (i, 0)) for _ in xs]
    w_specs = [_w_spec(kp, tn, p * kp, col_off, transposed) for p in range(nparts)]
    vmem = 2 * nparts * (tm * kp * 2 + kp * tn * 4) + nparts * kp * tn * 2 + 4 * tm * tn * 4
    return pl.pallas_call(
        functools.partial(_mm_kernel, nparts=nparts, epilogue=epilogue, dims=_NT if transposed else _NN),
        grid=(m // tm, n // tn),
        in_specs=x_specs + w_specs + list(extra_specs),
        out_specs=pl.BlockSpec((tm, tn), lambda i, j: (i, j)),
        out_shape=jax.ShapeDtypeStruct((m, n), out_dtype),
        compiler_params=_cparams(("parallel", "arbitrary"), vmem),
        name="mm",
    )(*xs, *([w] * nparts), *extra)


def _mm3(xh, xl, w, *, col_off, n, tm, tn, transposed=False):
    m, k = xh.shape
    assert m % tm == 0 and n % tn == 0
    xspec = pl.BlockSpec((tm, k), lambda i, j: (i, 0))
    vmem = 2 * (2 * tm * k * 2 + k * tn * 4) + 2 * k * tn * 2 + 4 * tm * tn * 4
    return pl.pallas_call(
        functools.partial(_mm3_kernel, dims=_NT if transposed else _NN),
        grid=(m // tm, n // tn),
        in_specs=[xspec, xspec, _w_spec(k, tn, 0, col_off, transposed)],
        out_specs=pl.BlockSpec((tm, tn), lambda i, j: (i, j)),
        out_shape=jax.ShapeDtypeStruct((m, n), f32),
        compiler_params=_cparams(("parallel", "arbitrary"), vmem),
        name="mm3",
    )(xh, xl, w)


def _res_ln(x, y, gate, g, b):
    z = ALPHA * x + (1.0 + gate) * y
    mu = jnp.mean(z, axis=-1, keepdims=True)
    zc = z - mu
    var = jnp.mean(zc * zc, axis=-1, keepdims=True)
    return zc * lax.rsqrt(var + LN_EPS) * g + b


def _ln_router_kernel(x_ref, y_ref, gate_ref, g_ref, b_ref, sc_ref, sh_ref, wr_ref, br_ref,
                      xo_ref, h_ref, rw_ref, rid_ref):
    xn = _res_ln(x_ref[...], y_ref[...], gate_ref[0], g_ref[...], b_ref[...])
    xo_ref[...] = xn
    h = xn * (1.0 + sc_ref[0]) + sh_ref[0]
    h_ref[...] = h
    logits = _dot3(h, wr_ref[...]) + br_ref[...]
    lane = lax.broadcasted_iota(i32, logits.shape, 1)
    neg = jnp.float32(-jnp.inf)
    is_grp = lane < N_GROUPS
    gl = jnp.where(is_grp, logits, neg)
    gmax = jnp.max(gl, axis=1, keepdims=True)
    gidx = jnp.min(jnp.where(gl == gmax, lane, LANES), axis=1, keepdims=True)
    gsum = jnp.sum(jnp.where(is_grp, jnp.exp(gl - gmax), 0.0), axis=1, keepdims=True)
    grp_p = 1.0 / gsum
    lo = N_GROUPS + EXPERTS_PER_GROUP * gidx
    in_grp = jnp.logical_and(lane >= lo, lane < lo + EXPERTS_PER_GROUP)
    el = jnp.where(in_grp, logits, neg)
    m1 = jnp.max(el, axis=1, keepdims=True)
    i1 = jnp.min(jnp.where(el == m1, lane, LANES), axis=1, keepdims=True)
    el2 = jnp.where(lane == i1, neg, el)
    m2 = jnp.max(el2, axis=1, keepdims=True)
    i2 = jnp.min(jnp.where(el2 == m2, lane, LANES), axis=1, keepdims=True)
    e2 = jnp.exp(m2 - m1)
    w1 = grp_p / (1.0 + e2)
    w2 = grp_p * e2 / (1.0 + e2)
    rw_ref[...] = jnp.where(lane == 0, w1, jnp.where(lane == 1, w2, 0.0))
    rid_ref[...] = jnp.where(lane == 0, i1 - N_GROUPS, jnp.where(lane == 1, i2 - N_GROUPS, 0))


def _ln_router(x2, y2, gate, g, b, sc, sh, wr, br, seq):
    m, d = x2.shape
    tm = 256
    row = pl.BlockSpec((tm, d), lambda i: (i, 0))
    per_batch = pl.BlockSpec((1, 1, d), lambda i: (i * tm // seq, 0, 0))
    vec = pl.BlockSpec((1, d), lambda i: (0, 0))
    small = pl.BlockSpec((tm, LANES), lambda i: (i, 0))
    return pl.pallas_call(
        _ln_router_kernel,
        grid=(m // tm,),
        in_specs=[row, row, per_batch, vec, vec, per_batch, per_batch,
                  pl.BlockSpec((d, LANES), lambda i: (0, 0)), pl.BlockSpec((1, LANES), lambda i: (0, 0))],
        out_specs=[row, row, small, small],
        out_shape=[jax.ShapeDtypeStruct((m, d), f32), jax.ShapeDtypeStruct((m, d), f32),
                   jax.ShapeDtypeStruct((m, LANES), f32), jax.ShapeDtypeStruct((m, LANES), i32)],
        compiler_params=_cparams(("parallel",), 12 * tm * d * 4 + 2 * d * LANES * 4),
        name="ln_router",
    )(x2, y2, gate, g, b, sc, sh, wr, br)


MOE_TM = 256
MOE_HALF = D_EXPERT // 2
DMA_UNROLL = 8


def _row_copy(src_hbm, row, dst_vmem, r, sem):
    return pltpu.make_async_copy(src_hbm.at[pl.ds(row, 1), :], dst_vmem.at[pl.ds(r, 1), :], sem)


def _gather_start(idx_ref, base, src_hbm, dst_vmem, sem, nrows, stride=1):
    def issue(r, c):
        _row_copy(src_hbm, idx_ref[base + r * stride], dst_vmem, r, sem).start()
        return c

    lax.fori_loop(0, nrows, issue, 0, unroll=DMA_UNROLL)


def _gather_wait(src_hbm, dst_vmem, sem, nrows):
    def wait(r, c):
        _row_copy(src_hbm, 0, dst_vmem, r, sem).wait()
        return c

    lax.fori_loop(0, nrows, wait, 0, unroll=DMA_UNROLL)


MOE_NHALF = D_EXPERT // MOE_HALF


def _moe_ffn_kernel(te_ref, tv_ref, tok_ref, h_hbm, wg_hbm, wu_hbm, wd_hbm, o_ref,
                    xbuf, wgb, wub, wdb, xsems, wsems, *, layer):
    i = pl.program_id(0)
    n = pl.num_programs(0)
    slot = i % 2
    valid = tv_ref[i] > 0
    nxt = jnp.minimum(i + 1, n - 1)
    has_next = jnp.logical_and(i + 1 < n, tv_ref[nxt] > 0)
    e = te_ref[i]
    next_e = te_ref[nxt]
    first = jnp.logical_or(i == 0, e != te_ref[jnp.maximum(i - 1, 0)])
    last = jnp.logical_and(has_next, next_e != e)

    def gather(tile, s):
        _gather_start(tok_ref, tile * MOE_TM, h_hbm, xbuf.at[s], xsems.at[s], MOE_TM)

    def weight_copies(expert, hf):
        cs = pl.ds(hf * MOE_HALF, MOE_HALF)
        return (pltpu.make_async_copy(wg_hbm.at[layer, expert, :, cs], wgb.at[hf], wsems.at[hf, 0]),
                pltpu.make_async_copy(wu_hbm.at[layer, expert, :, cs], wub.at[hf], wsems.at[hf, 1]),
                pltpu.make_async_copy(wd_hbm.at[layer, expert, cs, :], wdb.at[hf], wsems.at[hf, 2]))

    @pl.when(jnp.logical_and(i == 0, valid))
    def _prime():
        gather(0, 0)
        for hf in range(MOE_NHALF):
            for cp in weight_copies(e, hf):
                cp.start()

    def gather_next_part(part, nparts):
        rows = MOE_TM // nparts
        for r in range(part * rows, (part + 1) * rows):
            _row_copy(h_hbm, tok_ref[nxt * MOE_TM + r], xbuf.at[1 - slot], r, xsems.at[1 - slot]).start()

    @pl.when(valid)
    def _compute():
        _gather_wait(h_hbm, xbuf.at[slot], xsems.at[slot], MOE_TM)
        xb = xbuf[slot].astype(bf16)
        y = None
        for hf in range(MOE_NHALF):
            @pl.when(first)
            def _arrived(hf=hf):
                for cp in weight_copies(e, hf):
                    cp.wait()

            gather_next_part(2 * hf, 2 * MOE_NHALF)
            gate = jnp.dot(xb, wgb[hf].astype(bf16), preferred_element_type=f32)
            up = jnp.dot(xb, wub[hf].astype(bf16), preferred_element_type=f32)
            hid = gate * _sigmoid(gate) * up
            gather_next_part(2 * hf + 1, 2 * MOE_NHALF)
            part = jnp.dot(hid.astype(bf16), wdb[hf].astype(bf16), preferred_element_type=f32)
            y = part if y is None else y + part

            @pl.when(last)
            def _reload(hf=hf):
                for cp in weight_copies(next_e, hf):
                    cp.start()

        o_ref[...] = y

        @pl.when(jnp.logical_not(has_next))
        def _drain():
            _gather_wait(h_hbm, xbuf.at[1 - slot], xsems.at[1 - slot], MOE_TM)

    @pl.when(jnp.logical_not(valid))
    def _empty():
        o_ref[...] = jnp.zeros_like(o_ref)


def _moe_ffn(h2, w_gate, w_up, w_down, layer, tile_e, tile_valid, row_tok, n_tiles):
    n, d = h2.shape
    tm = MOE_TM
    anyspace = pl.BlockSpec(memory_space=pl.ANY)
    grid_spec = pltpu.PrefetchScalarGridSpec(
        num_scalar_prefetch=3,
        grid=(n_tiles,),
        in_specs=[anyspace, anyspace, anyspace, anyspace],
        out_specs=pl.BlockSpec((tm, d), lambda i, te, tv, tok: (i, 0)),
        scratch_shapes=[pltpu.VMEM((2, tm, d), f32),
                        pltpu.VMEM((MOE_NHALF, d, MOE_HALF), f32),
                        pltpu.VMEM((MOE_NHALF, d, MOE_HALF), f32),
                        pltpu.VMEM((MOE_NHALF, MOE_HALF, d), f32),
                        pltpu.SemaphoreType.DMA((2,)),
                        pltpu.SemaphoreType.DMA((MOE_NHALF, 3))],
    )
    vmem = 3 * d * D_EXPERT * 4 + 3 * d * MOE_HALF * 2 + 2 * tm * d * 4 + tm * d * 2 + 4 * tm * d * 4
    return pl.pallas_call(
        functools.partial(_moe_ffn_kernel, layer=layer),
        grid_spec=grid_spec,
        out_shape=jax.ShapeDtypeStruct((n_tiles * tm, d), f32),
        compiler_params=_cparams(("arbitrary",), vmem),
        name="moe_ffn",
    )(tile_e, tile_valid, row_tok, h2, w_gate, w_up, w_down)


def _moe_combine_kernel(pos_ref, ys_hbm, rw_ref, x_ref, gate_ref, g_ref, b_ref, *rest, next_mod):
    if next_mod:
        sc_ref, sh_ref, o_ref, hi_ref, lo_ref, buf, sems = rest
    else:
        o_ref, buf, sems = rest
    i = pl.program_id(0)
    n = pl.num_programs(0)
    tm = x_ref.shape[0]
    slot = i % 2

    def start(tile, s):
        for k in range(2):
            _gather_start(pos_ref, 2 * tile * tm + k, ys_hbm, buf.at[s, k], sems.at[s, k], tm, stride=2)

    @pl.when(i == 0)
    def _first_gather():
        start(0, 0)

    @pl.when(i + 1 < n)
    def _next_gather():
        start(i + 1, 1 - slot)

    for k in range(2):
        _gather_wait(ys_hbm, buf.at[slot, k], sems.at[slot, k], tm)
    rw = rw_ref[...]
    y = rw[:, 0:1] * buf[slot, 0] + rw[:, 1:2] * buf[slot, 1]
    xn = _res_ln(x_ref[...], y, gate_ref[0], g_ref[...], b_ref[...])
    o_ref[...] = xn
    if next_mod:
        hi, lo = _split_bf16(xn * (1.0 + sc_ref[0]) + sh_ref[0])
        hi_ref[...] = hi
        lo_ref[...] = lo


def _moe_combine(pos, ys, rw, x2, gate, g, b, seq, next_mod=None):
    n, d = x2.shape
    tm = 256
    row = pl.BlockSpec((tm, d), lambda i, pos: (i, 0))
    per_batch = pl.BlockSpec((1, 1, d), lambda i, pos: (i * tm // seq, 0, 0))
    vec = pl.BlockSpec((1, d), lambda i, pos: (0, 0))
    extra = () if next_mod is None else tuple(next_mod)
    grid_spec = pltpu.PrefetchScalarGridSpec(
        num_scalar_prefetch=1,
        grid=(n // tm,),
        in_specs=[pl.BlockSpec(memory_space=pl.ANY), pl.BlockSpec((tm, LANES), lambda i, pos: (i, 0)),
                  row, per_batch, vec, vec] + [per_batch] * len(extra),
        out_specs=[row] * (1 + len(extra)) if extra else row,
        scratch_shapes=[pltpu.VMEM((2, 2, tm, d), f32), pltpu.SemaphoreType.DMA((2, 2))],
    )
    x_shape = jax.ShapeDtypeStruct((n, d), f32)
    mod_shape = jax.ShapeDtypeStruct((n, d), bf16)
    return pl.pallas_call(
        functools.partial(_moe_combine_kernel, next_mod=bool(extra)),
        grid_spec=grid_spec,
        out_shape=[x_shape, mod_shape, mod_shape] if extra else x_shape,
        compiler_params=_cparams(("arbitrary",), 14 * tm * d * 4),
        name="moe_combine",
    )(pos, ys, rw, x2, gate, g, b, *extra)


def _moe_plan(eid, n_tiles):
    tm = MOE_TM
    flat_e = eid.reshape(-1)
    onehot = (flat_e[:, None] == jnp.arange(N_EXPERTS, dtype=i32)[None, :]).astype(i32)
    csum = jnp.cumsum(onehot, axis=0)
    rank = jnp.sum((csum - onehot) * onehot, axis=1)
    counts = csum[-1]
    padded = ((counts + tm - 1) // tm) * tm
    ends = jnp.cumsum(padded)
    pos = ((ends - padded)[flat_e] + rank).astype(i32)
    tile_start = jnp.arange(n_tiles, dtype=i32) * tm
    tile_valid = (tile_start < ends[-1]).astype(i32)
    tile_e = jnp.searchsorted(ends, tile_start, side="right").astype(i32)
    last_valid = jnp.maximum(ends[-1] // tm - 1, 0)
    tile_e = jnp.where(tile_valid > 0, tile_e, tile_e[last_valid])
    row_tok = jnp.zeros((n_tiles * tm,), i32).at[pos].set(jnp.arange(flat_e.shape[0], dtype=i32) // 2)
    return pos, tile_e, tile_valid, row_tok


def _moe_layer(x2, y2, gate_m, ln_g, ln_b, scale_f, shift_f, gate_f, ln_g2, ln_b2,
               w_grp, b_grp, w_exp, b_exp, w_gate, w_up, w_down, layer, seq, next_mod=None):
    n, d = x2.shape
    pad = LANES - N_GROUPS - N_EXPERTS
    wr = jnp.concatenate([w_grp, w_exp, jnp.zeros((d, pad), f32)], axis=1)
    br = jnp.concatenate([b_grp, b_exp, jnp.zeros((pad,), f32)])[None, :]
    x1, h, rw, rid = _ln_router(x2, y2, gate_m, ln_g[None, :], ln_b[None, :], scale_f, shift_f, wr, br, seq)
    n_tiles = (2 * n) // MOE_TM + N_EXPERTS
    pos, tile_e, tile_valid, row_tok = _moe_plan(rid[:, :2], n_tiles)
    ys = _moe_ffn(h, w_gate, w_up, w_down, layer, tile_e, tile_valid, row_tok, n_tiles)
    return _moe_combine(pos, ys, rw, x1, gate_f, ln_g2[None, :], ln_b2[None, :], seq, next_mod)


def _rope_table_kernel(pos_ref, fr_ref, sg_ref, c_ref, s_ref):
    ang = pos_ref[...].astype(f32) * fr_ref[...]
    c_ref[...] = jnp.cos(ang)
    s_ref[...] = jnp.sin(ang) * sg_ref[...]


def _rope_tables(pos2, head_dim):
    m = pos2.shape[0]
    half = head_dim // 2
    freqs = ROPE_THETA ** (-jnp.arange(half, dtype=f32) / half)
    reps = LANES // head_dim
    fr = jnp.tile(jnp.concatenate([freqs, freqs]), reps)[None, :]
    sg = jnp.tile(jnp.concatenate([-jnp.ones((half,), f32), jnp.ones((half,), f32)]), reps)[None, :]
    tm = min(m, 1024)
    vec = pl.BlockSpec((1, LANES), lambda i: (0, 0))
    out = pl.BlockSpec((tm, LANES), lambda i: (i, 0))
    return pl.pallas_call(
        _rope_table_kernel,
        grid=(m // tm,),
        in_specs=[pl.BlockSpec((tm, 1), lambda i: (i, 0)), vec, vec],
        out_specs=[out, out],
        out_shape=[jax.ShapeDtypeStruct((m, LANES), f32)] * 2,
        compiler_params=_cparams(("parallel",), 8 * tm * LANES * 4),
        name="rope_tables",
    )(pos2, fr, sg)


def _proj_rope(h_hi, w_t, c_tab, s_tab, *, col_off, n, scale, tm, tn):
    tab = pl.BlockSpec((tm, LANES), lambda i, j: (i, 0))
    return _mm([h_hi], w_t, col_off=col_off, n=n, tm=tm, tn=tn, out_dtype=bf16, transposed=True,
               epilogue=functools.partial(_rope_epilogue, scale=scale),
               extra=(c_tab, s_tab), extra_specs=(tab, tab))


IDX_W = IDX_HEADS * IDX_DIM
IDX_RAW = IDX_W + LANES


IDX_K = 4 * IDX_DIM


def _idx_rope_kernel(x_ref, c_ref, s_ref, q_ref, k_ref, w_ref):
    c = c_ref[...]
    s = s_ref[...]
    lane = lax.broadcasted_iota(i32, c.shape, 1)
    first = (lane & (IDX_DIM // 2)) == 0
    low = lane < IDX_DIM

    def rope(blk):
        swapped = jnp.where(first, pltpu.roll(blk, LANES - IDX_DIM // 2, 1), pltpu.roll(blk, IDX_DIM // 2, 1))
        return blk * c + swapped * s

    def hi_lo(x):
        hi = x.astype(bf16).astype(f32)
        return hi, x - hi

    for k in range(IDX_W // LANES):
        hi, lo = hi_lo(rope(x_ref[:, k * LANES:(k + 1) * LANES]))
        hi_sw = pltpu.roll(hi, IDX_DIM, 1)
        lo_sw = pltpu.roll(lo, IDX_DIM, 1)
        base = 2 * k * IDX_K
        q_ref[:, base:base + LANES] = jnp.where(low, hi, hi_sw).astype(bf16)
        q_ref[:, base + LANES:base + IDX_K] = jnp.where(low, lo, 0.0).astype(bf16)
        q_ref[:, base + IDX_K:base + IDX_K + LANES] = jnp.where(low, hi_sw, hi).astype(bf16)
        q_ref[:, base + IDX_K + LANES:base + 2 * IDX_K] = jnp.where(low, lo_sw, 0.0).astype(bf16)
    kw = x_ref[:, IDX_W:IDX_RAW]
    hi, lo = hi_lo(rope(kw))
    k_ref[:, 0:LANES] = jnp.where(low, hi, pltpu.roll(lo, IDX_DIM, 1)).astype(bf16)
    k_ref[:, LANES:IDX_K] = jnp.where(low, hi, 0.0).astype(bf16)
    w_ref[...] = kw


def _idx_rope(raw, c_tab, s_tab):
    m = raw.shape[0]
    tm = min(m, 512)
    tab = pl.BlockSpec((tm, LANES), lambda i: (i, 0))
    return pl.pallas_call(
        _idx_rope_kernel,
        grid=(m // tm,),
        in_specs=[pl.BlockSpec((tm, IDX_RAW), lambda i: (i, 0)), tab, tab],
        out_specs=[pl.BlockSpec((tm, IDX_HEADS * IDX_K), lambda i: (i, 0)),
                   pl.BlockSpec((tm, IDX_K), lambda i: (i, 0)), tab],
        out_shape=[jax.ShapeDtypeStruct((m, IDX_HEADS * IDX_K), bf16), jax.ShapeDtypeStruct((m, IDX_K), bf16),
                   jax.ShapeDtypeStruct((m, LANES), f32)],
        compiler_params=_cparams(("parallel",), 8 * tm * IDX_RAW * 4 + 4 * tm * IDX_HEADS * IDX_K * 2),
        name="idx_rope",
    )(raw, c_tab, s_tab)


INT_MIN = -2 ** 31


DSA_T = 256
DSA_KG = 4
MASKED = -1e30


def _key_group(nkb):
    return DSA_KG if nkb % DSA_KG == 0 else 1


def _dsa_index_kernel(q_ref, k_ref, qw_ref, m_ref, key_scr, *, topk, scale):
    i = pl.program_id(1)
    nkb, kb, tq = key_scr.shape
    qw_t = qw_ref[...].T
    q_chunk = (i * tq + lax.broadcasted_iota(i32, (1, tq), 1)) // CHUNK
    krow = lax.broadcasted_iota(i32, (kb, tq), 0)

    def score_block(j, c):
        kblk = k_ref[pl.ds(pl.multiple_of(j * kb, kb), kb), :]
        score = jnp.zeros((kb, tq), f32)
        for hd in range(IDX_HEADS):
            rel = lax.dot_general(kblk, q_ref[:, hd * IDX_K:(hd + 1) * IDX_K], _NT, preferred_element_type=f32)
            score = score + jnp.maximum(rel, 0.0) * qw_t[IDX_DIM + hd:IDX_DIM + hd + 1, :]
        score = score * scale
        adm = j * kb + krow < (q_chunk + 1) * CHUNK
        bits = pltpu.bitcast(score, i32)
        key = jnp.where(bits < 0, bits ^ 0x7FFFFFFF, bits)
        key_scr[j] = jnp.where(adm, key, INT_MIN)
        return c

    lax.fori_loop(0, i + 1, score_block, 0)

    kg = _key_group(nkb)
    ngroups = (i + kg) // kg

    def fill(j, c):
        key_scr[j] = jnp.full((kb, tq), INT_MIN, i32)
        return c

    lax.fori_loop(i + 1, ngroups * kg, fill, 0)

    def count_ge(cand):
        def grp(g, acc):
            for u in range(kg):
                ge = (key_scr[g * kg + u] >= cand).astype(f32)
                for r in range(kb // SUBLANES):
                    acc = acc + ge[r * SUBLANES:(r + 1) * SUBLANES, :]
            return acc

        acc = lax.fori_loop(0, ngroups, grp, jnp.zeros((SUBLANES, tq), f32))
        return jnp.sum(acc, axis=0, keepdims=True)

    kf = jnp.float32(topk)
    cur = jnp.where(count_ge(jnp.zeros((1, tq), i32)) >= kf, 0, INT_MIN).astype(i32)

    def bisect(it, cur):
        cand = cur + jnp.left_shift(jnp.int32(1), 30 - it)
        return jnp.where(count_ge(cand) >= kf, cand, cur)

    thr = jnp.maximum(lax.fori_loop(0, 31, bisect, cur), INT_MIN + 1)

    def write(j, c):
        m_ref[0, j] = jnp.where(key_scr[j] >= thr, 0.0, MASKED).T
        return c

    def clear(j, c):
        m_ref[0, j] = jnp.full((tq, kb), MASKED, f32)
        return c

    lax.fori_loop(0, i + 1, write, 0)
    lax.fori_loop(i + 1, nkb, clear, 0)


def _dsa_index(iq, ik, iw, batch, seq):
    tq = min(seq, DSA_T)
    nq = seq // tq
    topk = min(TOPK_MAX, seq // 4)
    scale = IDX_DIM ** -0.5 * IDX_HEADS ** -0.5
    return pl.pallas_call(
        functools.partial(_dsa_index_kernel, topk=topk, scale=scale),
        grid=(batch, nq),
        in_specs=[pl.BlockSpec((tq, IDX_HEADS * IDX_K), lambda b, i: (b * nq + i, 0)),
                  pl.BlockSpec((seq, IDX_K), lambda b, i: (b, 0)),
                  pl.BlockSpec((tq, LANES), lambda b, i: (b * nq + i, 0))],
        out_specs=pl.BlockSpec((1, nq, tq, tq), lambda b, i: (b * nq + i, 0, 0, 0)),
        out_shape=jax.ShapeDtypeStruct((batch * nq, nq, tq, tq), f32),
        scratch_shapes=[pltpu.VMEM((nq, tq, tq), i32)],
        compiler_params=_cparams(("parallel", "parallel"),
                                 8 * tq * seq * 4 + 4 * seq * IDX_K * 2 + 4 * tq * IDX_HEADS * IDX_K * 2),
        name="dsa_index",
    )(iq, ik, iw)


ATT_HG = 4


def _dsa_attn_kernel(q_ref, k_ref, v_ref, m_ref, o_ref):
    i = pl.program_id(2)
    _, nkb, tq, kb = m_ref.shape
    kg = _key_group(nkb)

    def group(g, carry):
        rows = pl.ds(pl.multiple_of(g * (kg * kb), kg * kb), kg * kb)
        parts = [m_ref[0, g * kg + u] for u in range(kg)]
        bias = jnp.concatenate(parts, axis=1) if kg > 1 else parts[0]
        out = []
        for h in range(ATT_HG):
            mx, den, acc = carry[h]
            sl = slice(h * HEAD_DIM, (h + 1) * HEAD_DIM)
            logits = lax.dot_general(q_ref[:, sl], k_ref[rows, sl], _NT, preferred_element_type=f32) + bias
            mx_new = jnp.maximum(mx, jnp.max(logits, axis=1, keepdims=True))
            alpha = jnp.exp(mx - mx_new)
            p = jnp.exp(logits - mx_new)
            den = alpha * den + jnp.sum(p, axis=1, keepdims=True)
            acc = alpha * acc + jnp.dot(p.astype(bf16), v_ref[rows, sl], preferred_element_type=f32)
            out.append((mx_new, den, acc))
        return tuple(out)

    init = tuple((jnp.full((tq, 1), MASKED, f32), jnp.zeros((tq, 1), f32), jnp.zeros((tq, HEAD_DIM), f32))
                 for _ in range(ATT_HG))
    final = lax.fori_loop(0, (i + kg) // kg, group, init)
    for h in range(ATT_HG):
        _, den, acc = final[h]
        o_ref[:, h * HEAD_DIM:(h + 1) * HEAD_DIM] = (acc / den).astype(o_ref.dtype)


def _dsa_attn(q, k, v, mask, batch, seq):
    m, w = q.shape
    _, nq, tq, _ = mask.shape
    gw = ATT_HG * HEAD_DIM
    qspec = pl.BlockSpec((tq, gw), lambda b, g, i: (b * nq + i, g))
    kvspec = pl.BlockSpec((seq, gw), lambda b, g, i: (b, g))
    vmem = 2 * (2 * tq * gw * 2 + 2 * seq * gw * 2 + tq * seq * 4) + 8 * tq * DSA_KG * tq * 4
    return pl.pallas_call(
        _dsa_attn_kernel,
        grid=(batch, w // gw, nq),
        in_specs=[qspec, kvspec, kvspec, pl.BlockSpec((1, nq, tq, tq), lambda b, g, i: (b * nq + i, 0, 0, 0))],
        out_specs=qspec,
        out_shape=jax.ShapeDtypeStruct((m, w), bf16),
        compiler_params=_cparams(("parallel", "parallel", "arbitrary"), vmem),
        name="dsa_attn",
    )(q, k, v, mask)


RET_HG = 4


def _retention_kernel(q_ref, k_ref, v_ref, g_ref, gn_ref, lg_ref, o_ref, state):
    c = pl.program_id(2)
    t = q_ref.shape[0]

    @pl.when(c == 0)
    def _init():
        state[...] = jnp.zeros_like(state)

    ri = lax.broadcasted_iota(i32, (t, t), 0)
    ci = lax.broadcasted_iota(i32, (t, t), 1)
    diff = (ri - ci).astype(f32)
    pos = lax.broadcasted_iota(i32, (t, 1), 0).astype(f32)
    for h in range(RET_HG):
        sl = slice(h * HEAD_DIM, (h + 1) * HEAD_DIM)
        lg = lg_ref[:, h * HEAD_DIM:h * HEAD_DIM + 1]
        q = q_ref[:, sl]
        k = k_ref[:, sl]
        v = v_ref[:, sl]
        decay = jnp.where(diff >= 0, jnp.exp(lg * jnp.maximum(diff, 0.0)), 0.0)
        scores = lax.dot_general(q, k, _NT, preferred_element_type=f32) * decay
        inner = jnp.dot(scores.astype(bf16), v, preferred_element_type=f32)
        st = state[h]
        cross = jnp.dot(q, st.astype(bf16), preferred_element_type=f32) * jnp.exp(lg * (pos + 1.0))
        kz = k.astype(f32) * jnp.exp(lg * (t - 1.0 - pos))
        kv = jnp.dot(kz.T.astype(bf16), v, preferred_element_type=f32)
        state[h] = jnp.exp(lg * t) * st + kv
        ret = inner + cross
        mu = jnp.mean(ret, axis=1, keepdims=True)
        rc = ret - mu
        var = jnp.mean(rc * rc, axis=1, keepdims=True)
        gate = g_ref[:, sl]
        out = rc * lax.rsqrt(var + LN_EPS) * gn_ref[:, sl] * (gate * _sigmoid(gate))
        o_ref[:, sl] = out.astype(o_ref.dtype)


def _retention(q, k, v, g, gn_g, batch, seq):
    m, w = q.shape
    heads = w // HEAD_DIM
    t = min(seq, 256)
    nc = seq // t
    gw = RET_HG * HEAD_DIM
    log_g = jnp.log1p(-jnp.exp2(-5.0 - jnp.arange(heads, dtype=f32)))
    lg = jnp.repeat(log_g, HEAD_DIM)[None, :]
    blk = pl.BlockSpec((t, gw), lambda b, gi, c: (b * nc + c, gi))
    vec = pl.BlockSpec((1, gw), lambda b, gi, c: (0, gi))
    return pl.pallas_call(
        _retention_kernel,
        grid=(batch, w // gw, nc),
        in_specs=[blk, blk, blk, blk, vec, vec],
        out_specs=blk,
        out_shape=jax.ShapeDtypeStruct((m, w), bf16),
        scratch_shapes=[pltpu.VMEM((RET_HG, HEAD_DIM, HEAD_DIM), f32)],
        compiler_params=_cparams(("parallel", "parallel", "arbitrary"), 16 * t * gw * 4 + 8 * t * t * 4),
        name="retention",
    )(q, k, v, g, gn_g[None, :], lg)


def _mixer_sparse_retention(h_hi, h_lo, pos2, w_in, w_out, gn_g, batch, seq):
    m, d = h_hi.shape
    gw = d // 2
    tm = min(m, 1024)
    tn = 512
    c128, s128 = _rope_tables(pos2, HEAD_DIM)
    c64, s64 = _rope_tables(pos2, IDX_DIM)
    w_t = jnp.swapaxes(w_in, 0, 1)
    rope = functools.partial(_proj_rope, h_hi, w_t, c_tab=c128, s_tab=s128, n=gw, tm=tm, tn=tn)
    plain = functools.partial(_mm, [h_hi], w_t, n=gw, tm=tm, tn=tn, transposed=True)
    aq = rope(col_off=0, scale=HEAD_DIM ** -0.5)
    ak = rope(col_off=gw, scale=1.0)
    av = plain(col_off=2 * gw, out_dtype=bf16)
    idx_raw = _mm3(h_hi, h_lo, w_t, col_off=3 * gw, n=IDX_RAW, tm=min(m, 512), tn=IDX_RAW // 3, transposed=True)
    iq, ik, iw = _idx_rope(idx_raw, c64, s64)
    mask = _dsa_index(iq, ik, iw, batch, seq)
    a_out = _dsa_attn(aq, ak, av, mask, batch, seq)
    b_off = 3 * gw + IDX_W + IDX_DIM + IDX_HEADS
    bq = rope(col_off=b_off, scale=1.0)
    bk = rope(col_off=b_off + gw, scale=HEAD_DIM ** -0.5)
    bv = plain(col_off=b_off + 2 * gw, out_dtype=bf16)
    bg = plain(col_off=b_off + 3 * gw, out_dtype=f32)
    b_out = _retention(bq, bk, bv, bg, gn_g, batch, seq)
    return _mm([a_out, b_out], w_out, col_off=0, n=d, tm=tm, tn=tn, out_dtype=f32)


MXU_N = 256


def _seg_sum64(x):
    r = lax.broadcasted_iota(i32, (MXU_N, MXU_N), 0) // C_HEAD_DIM
    c = lax.broadcasted_iota(i32, (MXU_N, MXU_N), 1) // C_HEAD_DIM
    ones = (r == c).astype(bf16)
    hi = x.astype(bf16)
    r1 = x - hi.astype(f32)
    mid = r1.astype(bf16)
    lo = (r1 - mid.astype(f32)).astype(bf16)
    cols = []
    for k in range(x.shape[1] // MXU_N):
        sl = slice(k * MXU_N, (k + 1) * MXU_N)
        s = jnp.dot(hi[:, sl], ones, preferred_element_type=f32)
        s += jnp.dot(mid[:, sl], ones, preferred_element_type=f32)
        s += jnp.dot(lo[:, sl], ones, preferred_element_type=f32)
        cols.append(s)
    return jnp.concatenate(cols, axis=1) if len(cols) > 1 else cols[0]


def _neg_softplus_neg(z):
    return jnp.minimum(z, 0.0) - jnp.log(1.0 + jnp.exp(-jnp.abs(z)))


def _rwkv_prep_kernel(p_ref, pp_ref, mu_ref, w0_ref, wup_ref, a0_ref, aup_ref, gup_ref, ka_ref, rk_ref,
                      r_o, w_o, k_o, a_o, v_o, g_o, bon_o, *, seq, gw):
    i = pl.program_id(0)
    tm = p_ref.shape[0]
    p = p_ref[...]
    prev_row = jnp.where((i * tm) % seq == 0, 0.0, pp_ref[SUBLANES - 1:SUBLANES, :])
    row = lax.broadcasted_iota(i32, (tm, 1), 0)
    shifted = jnp.where(row == 0, prev_row, pltpu.roll(p, 1, 0))
    pm = p + (shifted - p) * mu_ref[...]
    r = pm[:, 0:gw]
    k = pm[:, gw:2 * gw]
    v = pm[:, 2 * gw:3 * gw]
    o = 3 * gw
    dw = pm[:, o:o + C_DECAY_RANK]
    da = pm[:, o + C_DECAY_RANK:o + C_DECAY_RANK + C_ICLR_RANK]
    dg = pm[:, o + C_DECAY_RANK + C_ICLR_RANK:]
    w_log = _neg_softplus_neg(w0_ref[...] + _dot3(jnp.tanh(dw), wup_ref[...])) - 0.5
    decay = jnp.exp(-jnp.exp(w_log))
    a = _sigmoid(a0_ref[...] + _dot3(da, aup_ref[...]))
    g = _dot3(_sigmoid(dg), gup_ref[...])
    k2 = k * (1.0 + (a - 1.0) * ka_ref[...])
    r_o[...] = r
    w_o[...] = decay
    k_o[...] = k
    a_o[...] = a
    v_o[...] = v
    g_o[...] = g
    bon_o[...] = _seg_sum64(r * k2 * rk_ref[...]) * v


def _rwkv_prep(pc, mu, w0, w_up, a0, a_up, g_up, k_a, r_k, seq):
    m, cc = pc.shape
    gw = w0.shape[0]
    tm = 128
    nsub = tm // SUBLANES
    vec = pl.BlockSpec((1, gw), lambda i: (0, 0))
    out = pl.BlockSpec((tm, gw), lambda i: (i, 0))
    full = lambda a: pl.BlockSpec(a.shape, lambda i: (0, 0))
    return pl.pallas_call(
        functools.partial(_rwkv_prep_kernel, seq=seq, gw=gw),
        grid=(m // tm,),
        in_specs=[pl.BlockSpec((tm, cc), lambda i: (i, 0)),
                  pl.BlockSpec((SUBLANES, cc), lambda i: (jnp.maximum(i * nsub - 1, 0), 0)),
                  pl.BlockSpec((1, cc), lambda i: (0, 0)),
                  vec, full(w_up), vec, full(a_up), full(g_up), vec, vec],
        out_specs=[out] * 7,
        out_shape=[jax.ShapeDtypeStruct((m, gw), f32)] * 7,
        compiler_params=_cparams(("parallel",), 2 * tm * cc * 4 + 2 * 7 * tm * gw * 4 + 16 * tm * gw * 4),
        name="rwkv_prep",
    )(pc, pc, mu[None, :], w0[None, :], w_up, a0[None, :], a_up, g_up, k_a[None, :], r_k.reshape(1, -1))


SCAN_T = 64
SCAN_G = 16


def _rwkv_scan_kernel(w_ref, a_ref, k_ref, r_ref, v_ref, kkw_ref, kaw_ref, y_ref,
                      z_ref, w_s, kk_s, ka_s, k_s, r_s, v_s, y_s):
    c = pl.program_id(0)
    batch, steps, width = w_ref.shape
    nslab = z_ref.shape[0]
    heads = width // C_HEAD_DIM
    half = batch * heads

    @pl.when(c == 0)
    def _init():
        z_ref[...] = jnp.zeros_like(z_ref)

    kkw = kkw_ref[...]
    kaw = kaw_ref[...]

    def by_head(ref, rows, copies):
        per_batch = [ref[b, rows, :].reshape(SCAN_G, heads, C_HEAD_DIM) for b in range(batch)]
        return jnp.concatenate(per_batch * copies, axis=1)

    def key_tiles(ref, rows):
        return jnp.swapaxes(by_head(ref, rows, 2), 1, 2)

    def derive(g, carry):
        rows = pl.ds(pl.multiple_of(g * SCAN_G, SCAN_G), SCAN_G)
        a = key_tiles(a_ref, rows)
        kraw = key_tiles(k_ref, rows)
        kk = kraw * kkw
        kk = kk * lax.rsqrt(jnp.maximum(jnp.sum(kk * kk, axis=1, keepdims=True), 1e-24))
        w_s[rows] = key_tiles(w_ref, rows)
        r_s[rows] = key_tiles(r_ref, rows)
        kk_s[rows] = kk
        ka_s[rows] = kk * a
        k_s[rows] = kraw * (1.0 + (a - 1.0) * kaw)
        vt = jnp.swapaxes(by_head(v_ref, rows, 1), 1, 2)
        v_s[rows] = jnp.concatenate([vt[:, :nslab, :], vt[:, nslab:, :]], axis=2)
        return carry

    lax.fori_loop(0, steps // SCAN_G, derive, 0)

    def step(t, carry):
        w = w_s[t]
        kk = kk_s[t]
        ka = ka_s[t]
        k = k_s[t]
        r = r_s[t]
        vrows = v_s[t]
        for s in range(nslab):
            z = z_ref[s]
            sk = jnp.sum(z * kk, axis=0, keepdims=True)
            zn = z * w - ka * sk + k * vrows[s:s + 1, :]
            z_ref[s] = zn
            y_s[t, s:s + 1, :] = jnp.sum(zn * r, axis=0, keepdims=True)
        return carry

    lax.fori_loop(0, steps, step, 0)

    def restore(g, carry):
        rows = pl.ds(pl.multiple_of(g * SCAN_G, SCAN_G), SCAN_G)
        y = y_s[rows]
        yt = jnp.concatenate([y[:, :, :half], y[:, :, half:]], axis=1)
        yh = jnp.swapaxes(yt, 1, 2)
        for b in range(batch):
            y_ref[b, rows, :] = yh[:, b * heads:(b + 1) * heads, :].reshape(SCAN_G, width)
        return carry

    lax.fori_loop(0, steps // SCAN_G, restore, 0)


def _rwkv_scan(w, a, k, r, v, k_k, k_a, batch, seq):
    width = w.shape[1]
    heads = width // C_HEAD_DIM
    assert 2 * batch * heads == LANES
    nslab = C_HEAD_DIM // 2
    steps = min(seq, SCAN_T)
    consts = [jnp.tile(p.reshape(heads, C_HEAD_DIM).T, (1, 2 * batch)) for p in (k_k, k_a)]
    nat = pl.BlockSpec((batch, steps, width), lambda c: (0, c, 0))
    const_spec = pl.BlockSpec((C_HEAD_DIM, LANES), lambda c: (0, 0))
    tile_bytes = steps * C_HEAD_DIM * LANES * 4
    y = pl.pallas_call(
        _rwkv_scan_kernel,
        grid=(seq // steps,),
        in_specs=[nat] * 5 + [const_spec, const_spec],
        out_specs=nat,
        out_shape=jax.ShapeDtypeStruct((batch, seq, width), f32),
        scratch_shapes=[pltpu.VMEM((nslab, C_HEAD_DIM, LANES), f32)]
        + [pltpu.VMEM((steps, C_HEAD_DIM, LANES), f32)] * 5
        + [pltpu.VMEM((steps, nslab, LANES), f32)] * 2,
        compiler_params=_cparams(("arbitrary",), 6 * tile_bytes + 2 * 6 * batch * steps * width * 4
                                 + 8 * tile_bytes),
        name="rwkv_scan",
    )(*(x.reshape(batch, seq, width) for x in (w, a, k, r, v)), *consts)
    return y.reshape(batch * seq, width)


def _rwkv_post_kernel(y_ref, g_ref, bon_ref, lng_ref, lnb_ref, o_ref):
    y = y_ref[...]
    mu = _seg_sum64(y) * (1.0 / C_HEAD_DIM)
    yc = y - mu
    var = _seg_sum64(yc * yc) * (1.0 / C_HEAD_DIM)
    yn = yc * lax.rsqrt(var + C_EPS) * lng_ref[...] + lnb_ref[...]
    o_ref[...] = ((yn + bon_ref[...]) * g_ref[...]).astype(o_ref.dtype)


def _rwkv_post(y, g, bonus, ln_g, ln_b):
    m, gw = y.shape
    tm = 256
    blk = pl.BlockSpec((tm, gw), lambda i: (i, 0))
    vec = pl.BlockSpec((1, gw), lambda i: (0, 0))
    return pl.pallas_call(
        _rwkv_post_kernel,
        grid=(m // tm,),
        in_specs=[blk, blk, blk, vec, vec],
        out_specs=blk,
        out_shape=jax.ShapeDtypeStruct((m, gw), bf16),
        compiler_params=_cparams(("parallel",), 16 * tm * gw * 4),
        name="rwkv_post",
    )(y, g, bonus, ln_g[None, :], ln_b[None, :])


GELU_C = 0.7978845608028654


def _lru_kernel(px_ref, pg_ref, cw_ref, cb_ref, wa_ref, ba_ref, wx_ref, bx_ref, lam_ref, o_ref,
                tail, hcar, a_s, b_s):
    c = pl.program_id(1)
    t = px_ref.shape[0]

    @pl.when(c == 0)
    def _init():
        tail[...] = jnp.zeros_like(tail)
        hcar[...] = jnp.zeros_like(hcar)

    x = px_ref[...]
    ext = jnp.concatenate([tail[...], x], axis=0)
    xc = cb_ref[...]
    for j in range(D_CONV):
        off = SUBLANES - (D_CONV - 1) + j
        xc = xc + cw_ref[j:j + 1, :] * ext[off:off + t, :]
    tail[...] = x[t - SUBLANES:, :]
    lam = lam_ref[...]
    sp = jnp.maximum(-lam, 0.0) + jnp.log(1.0 + jnp.exp(-jnp.abs(lam)))
    bw = wa_ref.shape[1]
    for n in range(wa_ref.shape[0]):
        sl = slice(n * bw, (n + 1) * bw)
        xb = xc[:, sl]
        rg = _sigmoid(_dot3(xb, wa_ref[n]) + ba_ref[:, sl])
        ig = _sigmoid(_dot3(xb, wx_ref[n]) + bx_ref[:, sl])
        log_a = -LRU_C * rg * sp[:, sl]
        a_s[:, sl] = jnp.exp(log_a)
        th = jnp.tanh(log_a)
        one_minus_a2 = -2.0 * th / (1.0 - th)
        b_s[:, sl] = jnp.sqrt(one_minus_a2) * (ig * xb)

    sub = lax.broadcasted_iota(i32, (SUBLANES, a_s.shape[1]), 0)

    def rows8(g, h):
        r0 = pl.multiple_of(g * SUBLANES, SUBLANES)
        a = a_s[pl.ds(r0, SUBLANES), :]
        b = b_s[pl.ds(r0, SUBLANES), :]
        for s in (1, 2, 4):
            b = a * jnp.where(sub >= s, pltpu.roll(b, s, 0), 0.0) + b
            a = a * jnp.where(sub >= s, pltpu.roll(a, s, 0), 1.0)
        hs = a * h + b
        b_s[pl.ds(r0, SUBLANES), :] = hs
        return hs[SUBLANES - 1:SUBLANES, :]

    hcar[...] = lax.fori_loop(0, t // SUBLANES, rows8, hcar[...])
    gate = pg_ref[...]
    gelu = 0.5 * gate * (1.0 + jnp.tanh(GELU_C * (gate + 0.044715 * (gate * gate * gate))))
    o_ref[...] = (b_s[...] * gelu).astype(o_ref.dtype)


def _lru(px, pg, conv_w, conv_b, w_a, b_a, w_x, b_x, lam, batch, seq):
    m, w = px.shape
    t = min(seq, 256)
    nc = seq // t
    blk = pl.BlockSpec((t, w), lambda b, c: (b * nc + c, 0))
    vec = pl.BlockSpec((1, w), lambda b, c: (0, 0))
    wblk = pl.BlockSpec(w_a.shape, lambda b, c: (0, 0, 0))
    return pl.pallas_call(
        _lru_kernel,
        grid=(batch, nc),
        in_specs=[blk, blk, pl.BlockSpec((D_CONV, w), lambda b, c: (0, 0)), vec, wblk, vec, wblk, vec, vec],
        out_specs=blk,
        out_shape=jax.ShapeDtypeStruct((m, w), bf16),
        scratch_shapes=[pltpu.VMEM((SUBLANES, w), f32), pltpu.VMEM((1, w), f32),
                        pltpu.VMEM((t, w), f32), pltpu.VMEM((t, w), f32)],
        compiler_params=_cparams(("parallel", "arbitrary"), 16 * t * w * 4),
        name="lru",
    )(px, pg, conv_w, conv_b[None, :], w_a, b_a[None, :], w_x, b_x[None, :], lam[None, :])


def _mixer_rwkv_lru(h_hi, w_in, w_out, mu, w0, w_up, a0, a_up, g_up, k_k, k_a, r_k, ln_g, ln_b,
                    conv_w, conv_b, w_a, b_a, w_x, b_x, lam, batch, seq):
    m, d = h_hi.shape
    gw = d // 2
    c_cols = 3 * gw + C_DECAY_RANK + C_ICLR_RANK + C_GATE_RANK
    tm = min(m, 1024)
    tn = 512
    proj = functools.partial(_mm, [h_hi], w_in, tm=tm, tn=tn, out_dtype=f32)
    pc = proj(col_off=0, n=c_cols)
    pg = proj(col_off=c_cols, n=gw)
    px = proj(col_off=c_cols + gw, n=gw)
    r, w, k, a, v, g, bonus = _rwkv_prep(pc, mu, w0, w_up, a0, a_up, g_up, k_a, r_k, seq)
    y = _rwkv_scan(w, a, k, r, v, k_k, k_a, batch, seq)
    c_out = _rwkv_post(y, g, bonus, ln_g, ln_b)
    d_out = _lru(px, pg, conv_w, conv_b, w_a, b_a, w_x, b_x, lam, batch, seq)
    return _mm([c_out, d_out], w_out, col_off=0, n=d, tm=tm, tn=tn, out_dtype=f32)


ADA_ROWS = 16


def _ada_kernel(c_ref, w_ref, b_ref, o_ref):
    c = c_ref[...]
    s = (c * _sigmoid(c)).astype(bf16)
    o_ref[...] = jnp.dot(s, w_ref[...].astype(bf16), preferred_element_type=f32) + b_ref[...]


def _ada(c, ada_w, ada_b):
    batch, d = c.shape
    n = ada_w.shape[1]
    tn = 512
    cp = jnp.zeros((ADA_ROWS, d), f32).at[:batch].set(c)
    out = pl.pallas_call(
        _ada_kernel,
        grid=(n // tn,),
        in_specs=[pl.BlockSpec((ADA_ROWS, d), lambda j: (0, 0)),
                  pl.BlockSpec((d, tn), lambda j: (0, j)),
                  pl.BlockSpec((1, tn), lambda j: (0, j))],
        out_specs=pl.BlockSpec((ADA_ROWS, tn), lambda j: (0, j)),
        out_shape=jax.ShapeDtypeStruct((ADA_ROWS, n), f32),
        compiler_params=_cparams(("parallel",), 3 * d * tn * 4),
        name="ada",
    )(cp, ada_w, ada_b[None, :])
    return out[:batch]


def kernel(x, c, positions, ada_w, ada_b, ada_table, ln_g, ln_b, ab_w_in, ab_w_out, ret_gn_g, cd_w_in, cd_w_out, rwkv_mu, rwkv_w0, rwkv_w_up, rwkv_a0, rwkv_a_up, rwkv_g_up, rwkv_k_k, rwkv_k_a, rwkv_r_k, rwkv_ln_g, rwkv_ln_b, lru_conv_w, lru_conv_b, lru_w_a, lru_b_a, lru_w_x, lru_b_x, lru_lambda, moe_w_grp, moe_b_grp, moe_w_exp, moe_b_exp, moe_w_gate, moe_w_up, moe_w_down):
    batch, seq, d = x.shape
    x2 = x.reshape(batch * seq, d)
    pos2 = positions.reshape(batch * seq, 1)
    ada = _ada(c, ada_w, ada_b).reshape(batch, 6, 1, d)
    mods = [ada + ada_table[layer][None, :, None, :] for layer in range(DEPTH)]
    h_hi, h_lo = _modcast(x2, mods[0][:, 1], mods[0][:, 0], seq)
    for layer in range(DEPTH):
        shift_m, scale_m, gate_m, shift_f, scale_f, gate_f = (mods[layer][:, i] for i in range(6))
        j = layer // 2
        if layer % 2 == 0:
            y = _mixer_sparse_retention(h_hi, h_lo, pos2, ab_w_in[j], ab_w_out[j], ret_gn_g[j], batch, seq)
        else:
            y = _mixer_rwkv_lru(h_hi, cd_w_in[j], cd_w_out[j], rwkv_mu[j], rwkv_w0[j], rwkv_w_up[j],
                                rwkv_a0[j], rwkv_a_up[j], rwkv_g_up[j], rwkv_k_k[j], rwkv_k_a[j],
                                rwkv_r_k[j], rwkv_ln_g[j], rwkv_ln_b[j], lru_conv_w[j], lru_conv_b[j],
                                lru_w_a[j], lru_b_a[j], lru_w_x[j], lru_b_x[j], lru_lambda[j], batch, seq)
        last = layer + 1 == DEPTH
        next_mod = None if last else (mods[layer + 1][:, 1], mods[layer + 1][:, 0])
        out = _moe_layer(x2, y, gate_m, ln_g[layer, 0], ln_b[layer, 0], scale_f, shift_f, gate_f,
                         ln_g[layer, 1], ln_b[layer, 1], moe_w_grp[layer], moe_b_grp[layer],
                         moe_w_exp[layer], moe_b_exp[layer], moe_w_gate, moe_w_up, moe_w_down, layer, seq,
                         next_mod)
        x2, h_hi, h_lo = (out, None, None) if last else out
    return x2.reshape(batch, seq, d)
```

```python
import functools

import jax
import jax.numpy as jnp
from jax import lax
from jax.experimental import pallas as pl
from jax.experimental.pallas import tpu as pltpu

f32 = jnp.float32
bf16 = jnp.bfloat16
i32 = jnp.int32

DEPTH = 2
CHUNK = 64
ROPE_THETA = 10000.0
LN_EPS = 1e-5
ALPHA = (2 * DEPTH) ** 0.25
HEAD_DIM = 128
IDX_HEADS = 16
IDX_DIM = 64
TOPK_MAX = 256
C_HEAD_DIM = 64
C_DECAY_RANK = 128
C_ICLR_RANK = 128
C_GATE_RANK = 256
C_EPS = 64e-5
D_BLOCKS = 16
D_CONV = 4
LRU_C = 8.0
N_GROUPS = 4
EXPERTS_PER_GROUP = 8
N_EXPERTS = N_GROUPS * EXPERTS_PER_GROUP
D_EXPERT = 512

LANES = 128
SUBLANES = 8
VMEM_BYTES_V7X = 64 * 1024 * 1024
VMEM_HEADROOM = 8 * 1024 * 1024


def _cparams(semantics, vmem_bytes):
    limit = min(int(vmem_bytes) + VMEM_HEADROOM, VMEM_BYTES_V7X - VMEM_HEADROOM)
    return pltpu.CompilerParams(dimension_semantics=semantics, vmem_limit_bytes=limit)


def _split_bf16(a):
    hi = a.astype(bf16)
    lo = (a - hi.astype(f32)).astype(bf16)
    return hi, lo


def _dot3(a, b, dims=(((1,), (0,)), ((), ()))):
    ah, al = _split_bf16(a)
    bh, bl = _split_bf16(b)
    dg = functools.partial(lax.dot_general, dimension_numbers=dims, preferred_element_type=f32)
    return dg(ah, bh) + dg(ah, bl) + dg(al, bh)


def _sigmoid(x):
    return 1.0 / (1.0 + jnp.exp(-x))


def _modcast_kernel(x_ref, sc_ref, sh_ref, hi_ref, lo_ref):
    h = x_ref[...] * (1.0 + sc_ref[0]) + sh_ref[0]
    hi, lo = _split_bf16(h)
    hi_ref[...] = hi
    lo_ref[...] = lo


def _modcast(x2, sc, sh, seq):
    m, k = x2.shape
    tm = 256
    row = pl.BlockSpec((tm, k), lambda i: (i, 0))
    per_batch = pl.BlockSpec((1, 1, k), lambda i: (i * tm // seq, 0, 0))
    return pl.pallas_call(
        _modcast_kernel,
        grid=(m // tm,),
        in_specs=[row, per_batch, per_batch],
        out_specs=[row, row],
        out_shape=[jax.ShapeDtypeStruct((m, k), bf16)] * 2,
        compiler_params=_cparams(("parallel",), 2 * tm * k * (4 + 2 + 2)),
        name="modcast",
    )(x2, sc, sh)


_NN = (((1,), (0,)), ((), ()))
_NT = (((1,), (1,)), ((), ()))


def _mm_kernel(*refs, nparts, epilogue, dims):
    acc = None
    for p in range(nparts):
        d = lax.dot_general(refs[p][...], refs[nparts + p][...].astype(bf16), dims, preferred_element_type=f32)
        acc = d if acc is None else acc + d
    epilogue(acc, *refs[2 * nparts:])


def _mm3_kernel(xh_ref, xl_ref, w_ref, o_ref, *, dims):
    wh, wl = _split_bf16(w_ref[...])
    xh = xh_ref[...]
    dg = functools.partial(lax.dot_general, dimension_numbers=dims, preferred_element_type=f32)
    o_ref[...] = dg(xh, wh) + dg(xh, wl) + dg(xl_ref[...], wh)


def _w_spec(kp, tn, k_off, col_off, transposed):
    if transposed:
        assert col_off % SUBLANES == 0 and tn % SUBLANES == 0
        return pl.BlockSpec((pl.Element(tn), pl.Element(kp)),
                            lambda i, j: (pl.multiple_of(col_off + j * tn, SUBLANES), k_off))
    assert col_off % tn == 0 and k_off % kp == 0
    return pl.BlockSpec((kp, tn), lambda i, j: (k_off // kp, col_off // tn + j))


def _store_epilogue(acc, o_ref):
    o_ref[...] = acc.astype(o_ref.dtype)


def _rope_epilogue(acc, c_ref, s_ref, o_ref, *, scale):
    c = c_ref[...]
    s = s_ref[...]
    for k in range(acc.shape[1] // LANES):
        blk = acc[:, k * LANES:(k + 1) * LANES]
        rot = blk * c + pltpu.roll(blk, LANES // 2, 1) * s
        if scale != 1.0:
            rot = rot * scale
        o_ref[:, k * LANES:(k + 1) * LANES] = rot.astype(o_ref.dtype)


def _mm(xs, w, *, col_off, n, tm, tn, out_dtype, epilogue=_store_epilogue, extra=(), extra_specs=(),
        transposed=False):
    m = xs[0].shape[0]
    kp = xs[0].shape[1]
    nparts = len(xs)
    assert all(x.shape == (m, kp) for x in xs) and w.shape[1 if transposed else 0] == nparts * kp
    assert m % tm == 0 and n % tn == 0
    x_specs = [pl.BlockSpec((tm, kp), lambda i, j: (i, 0)) for _ in xs]
    w_specs = [_w_spec(kp, tn, p * kp, col_off, transposed) for p in range(nparts)]
    vmem = 2 * nparts * (tm * kp * 2 + kp * tn * 4) + nparts * kp * tn * 2 + 4 * tm * tn * 4
    return pl.pallas_call(
        functools.partial(_mm_kernel, nparts=nparts, epilogue=epilogue, dims=_NT if transposed else _NN),
        grid=(m // tm, n // tn),
        in_specs=x_specs + w_specs + list(extra_specs),
        out_specs=pl.BlockSpec((tm, tn), lambda i, j: (i, j)),
        out_shape=jax.ShapeDtypeStruct((m, n), out_dtype),
        compiler_params=_cparams(("parallel", "arbitrary"), vmem),
        name="mm",
    )(*xs, *([w] * nparts), *extra)


def _mm3(xh, xl, w, *, col_off, n, tm, tn, transposed=False):
    m, k = xh.shape
    assert m % tm == 0 and n % tn == 0
    xspec = pl.BlockSpec((tm, k), lambda i, j: (i, 0))
    vmem = 2 * (2 * tm * k * 2 + k * tn * 4) + 2 * k * tn * 2 + 4 * tm * tn * 4
    return pl.pallas_call(
        functools.partial(_mm3_kernel, dims=_NT if transposed else _NN),
        grid=(m // tm, n // tn),
        in_specs=[xspec, xspec, _w_spec(k, tn, 0, col_off, transposed)],
        out_specs=pl.BlockSpec((tm, tn), lambda i, j: (i, j)),
        out_shape=jax.ShapeDtypeStruct((m, n), f32),
        compiler_params=_cparams(("parallel", "arbitrary"), vmem),
        name="mm3",
    )(xh, xl, w)


def _res_ln(x, y, gate, g, b):
    z = ALPHA * x + (1.0 + gate) * y
    mu = jnp.mean(z, axis=-1, keepdims=True)
    zc = z - mu
    var = jnp.mean(zc * zc, axis=-1, keepdims=True)
    return zc * lax.rsqrt(var + LN_EPS) * g + b


def _ln_router_kernel(x_ref, y_ref, gate_ref, g_ref, b_ref, sc_ref, sh_ref, wr_ref, br_ref,
                      xo_ref, h_ref, rw_ref, rid_ref):
    xn = _res_ln(x_ref[...], y_ref[...], gate_ref[0], g_ref[...], b_ref[...])
    xo_ref[...] = xn
    h = xn * (1.0 + sc_ref[0]) + sh_ref[0]
    h_ref[...] = h
    logits = _dot3(h, wr_ref[...]) + br_ref[...]
    lane = lax.broadcasted_iota(i32, logits.shape, 1)
    neg = jnp.float32(-jnp.inf)
    is_grp = lane < N_GROUPS
    gl = jnp.where(is_grp, logits, neg)
    gmax = jnp.max(gl, axis=1, keepdims=True)
    gidx = jnp.min(jnp.where(gl == gmax, lane, LANES), axis=1, keepdims=True)
    gsum = jnp.sum(jnp.where(is_grp, jnp.exp(gl - gmax), 0.0), axis=1, keepdims=True)
    grp_p = 1.0 / gsum
    lo = N_GROUPS + EXPERTS_PER_GROUP * gidx
    in_grp = jnp.logical_and(lane >= lo, lane < lo + EXPERTS_PER_GROUP)
    el = jnp.where(in_grp, logits, neg)
    m1 = jnp.max(el, axis=1, keepdims=True)
    i1 = jnp.min(jnp.where(el == m1, lane, LANES), axis=1, keepdims=True)
    el2 = jnp.where(lane == i1, neg, el)
    m2 = jnp.max(el2, axis=1, keepdims=True)
    i2 = jnp.min(jnp.where(el2 == m2, lane, LANES), axis=1, keepdims=True)
    e2 = jnp.exp(m2 - m1)
    w1 = grp_p / (1.0 + e2)
    w2 = grp_p * e2 / (1.0 + e2)
    rw_ref[...] = jnp.where(lane == 0, w1, jnp.where(lane == 1, w2, 0.0))
    rid_ref[...] = jnp.where(lane == 0, i1 - N_GROUPS, jnp.where(lane == 1, i2 - N_GROUPS, 0))


def _ln_router(x2, y2, gate, g, b, sc, sh, wr, br, seq):
    m, d = x2.shape
    tm = 256
    row = pl.BlockSpec((tm, d), lambda i: (i, 0))
    per_batch = pl.BlockSpec((1, 1, d), lambda i: (i * tm // seq, 0, 0))
    vec = pl.BlockSpec((1, d), lambda i: (0, 0))
    small = pl.BlockSpec((tm, LANES), lambda i: (i, 0))
    return pl.pallas_call(
        _ln_router_kernel,
        grid=(m // tm,),
        in_specs=[row, row, per_batch, vec, vec, per_batch, per_batch,
                  pl.BlockSpec((d, LANES), lambda i: (0, 0)), pl.BlockSpec((1, LANES), lambda i: (0, 0))],
        out_specs=[row, row, small, small],
        out_shape=[jax.ShapeDtypeStruct((m, d), f32), jax.ShapeDtypeStruct((m, d), f32),
                   jax.ShapeDtypeStruct((m, LANES), f32), jax.ShapeDtypeStruct((m, LANES), i32)],
        compiler_params=_cparams(("parallel",), 12 * tm * d * 4 + 2 * d * LANES * 4),
        name="ln_router",
    )(x2, y2, gate, g, b, sc, sh, wr, br)


MOE_TM = 256
MOE_HALF = D_EXPERT // 2
DMA_UNROLL = 8


def _row_copy(src_hbm, row, dst_vmem, r, sem):
    return pltpu.make_async_copy(src_hbm.at[pl.ds(row, 1), :], dst_vmem.at[pl.ds(r, 1), :], sem)


def _gather_start(idx_ref, base, src_hbm, dst_vmem, sem, nrows, stride=1):
    def issue(r, c):
        _row_copy(src_hbm, idx_ref[base + r * stride], dst_vmem, r, sem).start()
        return c

    lax.fori_loop(0, nrows, issue, 0, unroll=DMA_UNROLL)


def _gather_wait(src_hbm, dst_vmem, sem, nrows):
    def wait(r, c):
        _row_copy(src_hbm, 0, dst_vmem, r, sem).wait()
        return c

    lax.fori_loop(0, nrows, wait, 0, unroll=DMA_UNROLL)


MOE_NHALF = D_EXPERT // MOE_HALF


def _moe_ffn_kernel(te_ref, tv_ref, tok_ref, h_hbm, wg_hbm, wu_hbm, wd_hbm, o_ref,
                    xbuf, wgb, wub, wdb, xsems, wsems, *, layer):
    i = pl.program_id(0)
    n = pl.num_programs(0)
    slot = i % 2
    valid = tv_ref[i] > 0
    nxt = jnp.minimum(i + 1, n - 1)
    has_next = jnp.logical_and(i + 1 < n, tv_ref[nxt] > 0)
    e = te_ref[i]
    next_e = te_ref[nxt]
    first = jnp.logical_or(i == 0, e != te_ref[jnp.maximum(i - 1, 0)])
    last = jnp.logical_and(has_next, next_e != e)

    def gather(tile, s):
        _gather_start(tok_ref, tile * MOE_TM, h_hbm, xbuf.at[s], xsems.at[s], MOE_TM)

    def weight_copies(expert, hf):
        cs = pl.ds(hf * MOE_HALF, MOE_HALF)
        return (pltpu.make_async_copy(wg_hbm.at[layer, expert, :, cs], wgb.at[hf], wsems.at[hf, 0]),
                pltpu.make_async_copy(wu_hbm.at[layer, expert, :, cs], wub.at[hf], wsems.at[hf, 1]),
                pltpu.make_async_copy(wd_hbm.at[layer, expert, cs, :], wdb.at[hf], wsems.at[hf, 2]))

    @pl.when(jnp.logical_and(i == 0, valid))
    def _prime():
        gather(0, 0)
        for hf in range(MOE_NHALF):
            for cp in weight_copies(e, hf):
                cp.start()

    def gather_next_part(part, nparts):
        rows = MOE_TM // nparts
        for r in range(part * rows, (part + 1) * rows):
            _row_copy(h_hbm, tok_ref[nxt * MOE_TM + r], xbuf.at[1 - slot], r, xsems.at[1 - slot]).start()

    @pl.when(valid)
    def _compute():
        _gather_wait(h_hbm, xbuf.at[slot], xsems.at[slot], MOE_TM)
        xb = xbuf[slot].astype(bf16)
        y = None
        for hf in range(MOE_NHALF):
            @pl.when(first)
            def _arrived(hf=hf):
                for cp in weight_copies(e, hf):
                    cp.wait()

            gather_next_part(2 * hf, 2 * MOE_NHALF)
            gate = jnp.dot(xb, wgb[hf].astype(bf16), preferred_element_type=f32)
            up = jnp.dot(xb, wub[hf].astype(bf16), preferred_element_type=f32)
            hid = gate * _sigmoid(gate) * up
            gather_next_part(2 * hf + 1, 2 * MOE_NHALF)
            part = jnp.dot(hid.astype(bf16), wdb[hf].astype(bf16), preferred_element_type=f32)
            y = part if y is None else y + part

            @pl.when(last)
            def _reload(hf=hf):
                for cp in weight_copies(next_e, hf):
                    cp.start()

        o_ref[...] = y

        @pl.when(jnp.logical_not(has_next))
        def _drain():
            _gather_wait(h_hbm, xbuf.at[1 - slot], xsems.at[1 - slot], MOE_TM)

    @pl.when(jnp.logical_not(valid))
    def _empty():
        o_ref[...] = jnp.zeros_like(o_ref)


def _moe_ffn(h2, w_gate, w_up, w_down, layer, tile_e, tile_valid, row_tok, n_tiles):
    n, d = h2.shape
    tm = MOE_TM
    anyspace = pl.BlockSpec(memory_space=pl.ANY)
    grid_spec = pltpu.PrefetchScalarGridSpec(
        num_scalar_prefetch=3,
        grid=(n_tiles,),
        in_specs=[anyspace, anyspace, anyspace, anyspace],
        out_specs=pl.BlockSpec((tm, d), lambda i, te, tv, tok: (i, 0)),
        scratch_shapes=[pltpu.VMEM((2, tm, d), f32),
                        pltpu.VMEM((MOE_NHALF, d, MOE_HALF), f32),
                        pltpu.VMEM((MOE_NHALF, d, MOE_HALF), f32),
                        pltpu.VMEM((MOE_NHALF, MOE_HALF, d), f32),
                        pltpu.SemaphoreType.DMA((2,)),
                        pltpu.SemaphoreType.DMA((MOE_NHALF, 3))],
    )
    vmem = 3 * d * D_EXPERT * 4 + 3 * d * MOE_HALF * 2 + 2 * tm * d * 4 + tm * d * 2 + 4 * tm * d * 4
    return pl.pallas_call(
        functools.partial(_moe_ffn_kernel, layer=layer),
        grid_spec=grid_spec,
        out_shape=jax.ShapeDtypeStruct((n_tiles * tm, d), f32),
        compiler_params=_cparams(("arbitrary",), vmem),
        name="moe_ffn",
    )(tile_e, tile_valid, row_tok, h2, w_gate, w_up, w_down)


def _moe_combine_kernel(pos_ref, ys_hbm, rw_ref, x_ref, gate_ref, g_ref, b_ref, *rest, next_mod):
    if next_mod:
        sc_ref, sh_ref, o_ref, hi_ref, lo_ref, buf, sems = rest
    else:
        o_ref, buf, sems = rest
    i = pl.program_id(0)
    n = pl.num_programs(0)
    tm = x_ref.shape[0]
    slot = i % 2

    def start(tile, s):
        for k in range(2):
            _gather_start(pos_ref, 2 * tile * tm + k, ys_hbm, buf.at[s, k], sems.at[s, k], tm, stride=2)

    @pl.when(i == 0)
    def _first_gather():
        start(0, 0)

    @pl.when(i + 1 < n)
    def _next_gather():
        start(i + 1, 1 - slot)

    for k in range(2):
        _gather_wait(ys_hbm, buf.at[slot, k], sems.at[slot, k], tm)
    rw = rw_ref[...]
    y = rw[:, 0:1] * buf[slot, 0] + rw[:, 1:2] * buf[slot, 1]
    xn = _res_ln(x_ref[...], y, gate_ref[0], g_ref[...], b_ref[...])
    o_ref[...] = xn
    if next_mod:
        hi, lo = _split_bf16(xn * (1.0 + sc_ref[0]) + sh_ref[0])
        hi_ref[...] = hi
        lo_ref[...] = lo


def _moe_combine(pos, ys, rw, x2, gate, g, b, seq, next_mod=None):
    n, d = x2.shape
    tm = 256
    row = pl.BlockSpec((tm, d), lambda i, pos: (i, 0))
    per_batch = pl.BlockSpec((1, 1, d), lambda i, pos: (i * tm // seq, 0, 0))
    vec = pl.BlockSpec((1, d), lambda i, pos: (0, 0))
    extra = () if next_mod is None else tuple(next_mod)
    grid_spec = pltpu.PrefetchScalarGridSpec(
        num_scalar_prefetch=1,
        grid=(n // tm,),
        in_specs=[pl.BlockSpec(memory_space=pl.ANY), pl.BlockSpec((tm, LANES), lambda i, pos: (i, 0)),
                  row, per_batch, vec, vec] + [per_batch] * len(extra),
        out_specs=[row] * (1 + len(extra)) if extra else row,
        scratch_shapes=[pltpu.VMEM((2, 2, tm, d), f32), pltpu.SemaphoreType.DMA((2, 2))],
    )
    x_shape = jax.ShapeDtypeStruct((n, d), f32)
    mod_shape = jax.ShapeDtypeStruct((n, d), bf16)
    return pl.pallas_call(
        functools.partial(_moe_combine_kernel, next_mod=bool(extra)),
        grid_spec=grid_spec,
        out_shape=[x_shape, mod_shape, mod_shape] if extra else x_shape,
        compiler_params=_cparams(("arbitrary",), 14 * tm * d * 4),
        name="moe_combine",
    )(pos, ys, rw, x2, gate, g, b, *extra)


def _moe_plan(eid, n_tiles):
    tm = MOE_TM
    flat_e = eid.reshape(-1)
    onehot = (flat_e[:, None] == jnp.arange(N_EXPERTS, dtype=i32)[None, :]).astype(i32)
    csum = jnp.cumsum(onehot, axis=0)
    rank = jnp.sum((csum - onehot) * onehot, axis=1)
    counts = csum[-1]
    padded = ((counts + tm - 1) // tm) * tm
    ends = jnp.cumsum(padded)
    pos = ((ends - padded)[flat_e] + rank).astype(i32)
    tile_start = jnp.arange(n_tiles, dtype=i32) * tm
    tile_valid = (tile_start < ends[-1]).astype(i32)
    tile_e = jnp.searchsorted(ends, tile_start, side="right").astype(i32)
    last_valid = jnp.maximum(ends[-1] // tm - 1, 0)
    tile_e = jnp.where(tile_valid > 0, tile_e, tile_e[last_valid])
    row_tok = jnp.zeros((n_tiles * tm,), i32).at[pos].set(jnp.arange(flat_e.shape[0], dtype=i32) // 2)
    return pos, tile_e, tile_valid, row_tok


def _moe_layer(x2, y2, gate_m, ln_g, ln_b, scale_f, shift_f, gate_f, ln_g2, ln_b2,
               w_grp, b_grp, w_exp, b_exp, w_gate, w_up, w_down, layer, seq, next_mod=None):
    n, d = x2.shape
    pad = LANES - N_GROUPS - N_EXPERTS
    wr = jnp.concatenate([w_grp, w_exp, jnp.zeros((d, pad), f32)], axis=1)
    br = jnp.concatenate([b_grp, b_exp, jnp.zeros((pad,), f32)])[None, :]
    x1, h, rw, rid = _ln_router(x2, y2, gate_m, ln_g[None, :], ln_b[None, :], scale_f, shift_f, wr, br, seq)
    n_tiles = (2 * n) // MOE_TM + N_EXPERTS
    pos, tile_e, tile_valid, row_tok = _moe_plan(rid[:, :2], n_tiles)
    ys = _moe_ffn(h, w_gate, w_up, w_down, layer, tile_e, tile_valid, row_tok, n_tiles)
    return _moe_combine(pos, ys, rw, x1, gate_f, ln_g2[None, :], ln_b2[None, :], seq, next_mod)


def _rope_table_kernel(pos_ref, fr_ref, sg_ref, c_ref, s_ref):
    ang = pos_ref[...].astype(f32) * fr_ref[...]
    c_ref[...] = jnp.cos(ang)
    s_ref[...] = jnp.sin(ang) * sg_ref[...]


def _rope_tables(pos2, head_dim):
    m = pos2.shape[0]
    half = head_dim // 2
    freqs = ROPE_THETA ** (-jnp.arange(half, dtype=f32) / half)
    reps = LANES // head_dim
    fr = jnp.tile(jnp.concatenate([freqs, freqs]), reps)[None, :]
    sg = jnp.tile(jnp.concatenate([-jnp.ones((half,), f32), jnp.ones((half,), f32)]), reps)[None, :]
    tm = min(m, 1024)
    vec = pl.BlockSpec((1, LANES), lambda i: (0, 0))
    out = pl.BlockSpec((tm, LANES), lambda i: (i, 0))
    return pl.pallas_call(
        _rope_table_kernel,
        grid=(m // tm,),
        in_specs=[pl.BlockSpec((tm, 1), lambda i: (i, 0)), vec, vec],
        out_specs=[out, out],
        out_shape=[jax.ShapeDtypeStruct((m, LANES), f32)] * 2,
        compiler_params=_cparams(("parallel",), 8 * tm * LANES * 4),
        name="rope_tables",
    )(pos2, fr, sg)


def _proj_rope(h_hi, w_t, c_tab, s_tab, *, col_off, n, scale, tm, tn):
    tab = pl.BlockSpec((tm, LANES), lambda i, j: (i, 0))
    return _mm([h_hi], w_t, col_off=col_off, n=n, tm=tm, tn=tn, out_dtype=bf16, transposed=True,
               epilogue=functools.partial(_rope_epilogue, scale=scale),
               extra=(c_tab, s_tab), extra_specs=(tab, tab))


IDX_W = IDX_HEADS * IDX_DIM
IDX_RAW = IDX_W + LANES


IDX_K = 4 * IDX_DIM


def _idx_rope_kernel(x_ref, c_ref, s_ref, q_ref, k_ref, w_ref):
    c = c_ref[...]
    s = s_ref[...]
    lane = lax.broadcasted_iota(i32, c.shape, 1)
    first = (lane & (IDX_DIM // 2)) == 0
    low = lane < IDX_DIM

    def rope(blk):
        swapped = jnp.where(first, pltpu.roll(blk, LANES - IDX_DIM // 2, 1), pltpu.roll(blk, IDX_DIM // 2, 1))
        return blk * c + swapped * s

    def hi_lo(x):
        hi = x.astype(bf16).astype(f32)
        return hi, x - hi

    for k in range(IDX_W // LANES):
        hi, lo = hi_lo(rope(x_ref[:, k * LANES:(k + 1) * LANES]))
        hi_sw = pltpu.roll(hi, IDX_DIM, 1)
        lo_sw = pltpu.roll(lo, IDX_DIM, 1)
        base = 2 * k * IDX_K
        q_ref[:, base:base + LANES] = jnp.where(low, hi, hi_sw).astype(bf16)
        q_ref[:, base + LANES:base + IDX_K] = jnp.where(low, lo, 0.0).astype(bf16)
        q_ref[:, base + IDX_K:base + IDX_K + LANES] = jnp.where(low, hi_sw, hi).astype(bf16)
        q_ref[:, base + IDX_K + LANES:base + 2 * IDX_K] = jnp.where(low, lo_sw, 0.0).astype(bf16)
    kw = x_ref[:, IDX_W:IDX_RAW]
    hi, lo = hi_lo(rope(kw))
    k_ref[:, 0:LANES] = jnp.where(low, hi, pltpu.roll(lo, IDX_DIM, 1)).astype(bf16)
    k_ref[:, LANES:IDX_K] = jnp.where(low, hi, 0.0).astype(bf16)
    w_ref[...] = kw


def _idx_rope(raw, c_tab, s_tab):
    m = raw.shape[0]
    tm = min(m, 512)
    tab = pl.BlockSpec((tm, LANES), lambda i: (i, 0))
    return pl.pallas_call(
        _idx_rope_kernel,
        grid=(m // tm,),
        in_specs=[pl.BlockSpec((tm, IDX_RAW), lambda i: (i, 0)), tab, tab],
        out_specs=[pl.BlockSpec((tm, IDX_HEADS * IDX_K), lambda i: (i, 0)),
                   pl.BlockSpec((tm, IDX_K), lambda i: (i, 0)), tab],
        out_shape=[jax.ShapeDtypeStruct((m, IDX_HEADS * IDX_K), bf16), jax.ShapeDtypeStruct((m, IDX_K), bf16),
                   jax.ShapeDtypeStruct((m, LANES), f32)],
        compiler_params=_cparams(("parallel",), 8 * tm * IDX_RAW * 4 + 4 * tm * IDX_HEADS * IDX_K * 2),
        name="idx_rope",
    )(raw, c_tab, s_tab)


INT_MIN = -2 ** 31


DSA_T = 256
DSA_KG = 4
MASKED = -1e30


def _key_group(nkb):
    return DSA_KG if nkb % DSA_KG == 0 else 1


def _dsa_index_kernel(q_ref, k_ref, qw_ref, m_ref, key_scr, cut_s, *, topk, scale):
    i = pl.program_id(1)
    nkb, kb, tq = key_scr.shape
    qw_t = qw_ref[...].T
    q_chunk = (i * tq + lax.broadcasted_iota(i32, (1, tq), 1)) // CHUNK
    krow = lax.broadcasted_iota(i32, (kb, tq), 0)

    def score_block(j, c):
        kblk = k_ref[pl.ds(pl.multiple_of(j * kb, kb), kb), :]
        score = jnp.zeros((kb, tq), f32)
        for hd in range(IDX_HEADS):
            rel = lax.dot_general(kblk, q_ref[:, hd * IDX_K:(hd + 1) * IDX_K], _NT, preferred_element_type=f32)
            score = score + jnp.maximum(rel, 0.0) * qw_t[IDX_DIM + hd:IDX_DIM + hd + 1, :]
        score = score * scale
        adm = j * kb + krow < (q_chunk + 1) * CHUNK
        bits = pltpu.bitcast(score, i32)
        key = jnp.where(bits < 0, bits ^ 0x7FFFFFFF, bits)
        key_scr[j] = jnp.where(adm, key, INT_MIN)
        return c

    lax.fori_loop(0, i + 1, score_block, 0)

    kg = _key_group(nkb)
    ngroups = (i + kg) // kg

    def fill(j, c):
        key_scr[j] = jnp.full((kb, tq), INT_MIN, i32)
        return c

    lax.fori_loop(i + 1, ngroups * kg, fill, 0)

    def count_ge(cand):
        def grp(g, acc):
            for u in range(kg):
                ge = (key_scr[g * kg + u] >= cand).astype(f32)
                for r in range(kb // SUBLANES):
                    acc = acc + ge[r * SUBLANES:(r + 1) * SUBLANES, :]
            return acc

        acc = lax.fori_loop(0, ngroups, grp, jnp.zeros((SUBLANES, tq), f32))
        return jnp.sum(acc, axis=0, keepdims=True)

    kf = jnp.float32(topk)
    cur = jnp.where(count_ge(jnp.zeros((1, tq), i32)) >= kf, 0, INT_MIN).astype(i32)

    def bisect(it, cur):
        cand = cur + jnp.left_shift(jnp.int32(1), 30 - it)
        return jnp.where(count_ge(cand) >= kf, cand, cur)

    thr = jnp.maximum(lax.fori_loop(0, 31, bisect, cur), INT_MIN + 1)

    cut_s[...] = jnp.full((1, tq), nkb * kb, i32)
    surplus = count_ge(thr) - kf

    @pl.when(jnp.max(surplus) > 0.0)
    def _break_ties():
        def count_eq_below(cand):
            def grp(g, acc):
                for u in range(kg):
                    j = g * kg + u
                    hit = jnp.logical_and(key_scr[j] == thr, j * kb + krow < cand).astype(f32)
                    for r in range(kb // SUBLANES):
                        acc = acc + hit[r * SUBLANES:(r + 1) * SUBLANES, :]
                return acc

            acc = lax.fori_loop(0, ngroups, grp, jnp.zeros((SUBLANES, tq), f32))
            return jnp.sum(acc, axis=0, keepdims=True)

        need = count_eq_below(jnp.full((1, tq), nkb * kb, i32)) - surplus
        nbits = (nkb * kb).bit_length()

        def narrow(it, cur):
            cand = cur + jnp.left_shift(jnp.int32(1), nbits - 1 - it)
            return jnp.where(count_eq_below(cand) < need, cand, cur)

        cut_s[...] = lax.fori_loop(0, nbits, narrow, jnp.zeros((1, tq), i32)) + 1

    cut = cut_s[...]

    def write(j, c):
        key = key_scr[j]
        keep = jnp.logical_or(key > thr, jnp.logical_and(key == thr, j * kb + krow < cut))
        m_ref[0, j] = jnp.where(keep, 0.0, MASKED).T
        return c

    def clear(j, c):
        m_ref[0, j] = jnp.full((tq, kb), MASKED, f32)
        return c

    lax.fori_loop(0, i + 1, write, 0)
    lax.fori_loop(i + 1, nkb, clear, 0)


def _dsa_index(iq, ik, iw, batch, seq):
    tq = min(seq, DSA_T)
    nq = seq // tq
    topk = min(TOPK_MAX, seq // 4)
    scale = IDX_DIM ** -0.5 * IDX_HEADS ** -0.5
    return pl.pallas_call(
        functools.partial(_dsa_index_kernel, topk=topk, scale=scale),
        grid=(batch, nq),
        in_specs=[pl.BlockSpec((tq, IDX_HEADS * IDX_K), lambda b, i: (b * nq + i, 0)),
                  pl.BlockSpec((seq, IDX_K), lambda b, i: (b, 0)),
                  pl.BlockSpec((tq, LANES), lambda b, i: (b * nq + i, 0))],
        out_specs=pl.BlockSpec((1, nq, tq, tq), lambda b, i: (b * nq + i, 0, 0, 0)),
        out_shape=jax.ShapeDtypeStruct((batch * nq, nq, tq, tq), f32),
        scratch_shapes=[pltpu.VMEM((nq, tq, tq), i32), pltpu.VMEM((1, tq), i32)],
        compiler_params=_cparams(("parallel", "parallel"),
                                 8 * tq * seq * 4 + 4 * seq * IDX_K * 2 + 4 * tq * IDX_HEADS * IDX_K * 2),
        name="dsa_index",
    )(iq, ik, iw)


ATT_HG = 4


def _dsa_attn_kernel(q_ref, k_ref, v_ref, m_ref, o_ref):
    i = pl.program_id(2)
    _, nkb, tq, kb = m_ref.shape
    kg = _key_group(nkb)

    def group(g, carry):
        rows = pl.ds(pl.multiple_of(g * (kg * kb), kg * kb), kg * kb)
        parts = [m_ref[0, g * kg + u] for u in range(kg)]
        bias = jnp.concatenate(parts, axis=1) if kg > 1 else parts[0]
        out = []
        for h in range(ATT_HG):
            mx, den, acc = carry[h]
            sl = slice(h * HEAD_DIM, (h + 1) * HEAD_DIM)
            logits = lax.dot_general(q_ref[:, sl], k_ref[rows, sl], _NT, preferred_element_type=f32) + bias
            mx_new = jnp.maximum(mx, jnp.max(logits, axis=1, keepdims=True))
            alpha = jnp.exp(mx - mx_new)
            p = jnp.exp(logits - mx_new)
            den = alpha * den + jnp.sum(p, axis=1, keepdims=True)
            acc = alpha * acc + jnp.dot(p.astype(bf16), v_ref[rows, sl], preferred_element_type=f32)
            out.append((mx_new, den, acc))
        return tuple(out)

    init = tuple((jnp.full((tq, 1), MASKED, f32), jnp.zeros((tq, 1), f32), jnp.zeros((tq, HEAD_DIM), f32))
                 for _ in range(ATT_HG))
    final = lax.fori_loop(0, (i + kg) // kg, group, init)
    for h in range(ATT_HG):
        _, den, acc = final[h]
        o_ref[:, h * HEAD_DIM:(h + 1) * HEAD_DIM] = (acc / den).astype(o_ref.dtype)


def _dsa_attn(q, k, v, mask, batch, seq):
    m, w = q.shape
    _, nq, tq, _ = mask.shape
    gw = ATT_HG * HEAD_DIM
    qspec = pl.BlockSpec((tq, gw), lambda b, g, i: (b * nq + i, g))
    kvspec = pl.BlockSpec((seq, gw), lambda b, g, i: (b, g))
    vmem = 2 * (2 * tq * gw * 2 + 2 * seq * gw * 2 + tq * seq * 4) + 8 * tq * DSA_KG * tq * 4
    return pl.pallas_call(
        _dsa_attn_kernel,
        grid=(batch, w // gw, nq),
        in_specs=[qspec, kvspec, kvspec, pl.BlockSpec((1, nq, tq, tq), lambda b, g, i: (b * nq + i, 0, 0, 0))],
        out_specs=qspec,
        out_shape=jax.ShapeDtypeStruct((m, w), bf16),
        compiler_params=_cparams(("parallel", "parallel", "arbitrary"), vmem),
        name="dsa_attn",
    )(q, k, v, mask)


RET_HG = 4


def _retention_kernel(q_ref, k_ref, v_ref, g_ref, gn_ref, lg_ref, o_ref, state):
    c = pl.program_id(2)
    t = q_ref.shape[0]

    @pl.when(c == 0)
    def _init():
        state[...] = jnp.zeros_like(state)

    ri = lax.broadcasted_iota(i32, (t, t), 0)
    ci = lax.broadcasted_iota(i32, (t, t), 1)
    diff = (ri - ci).astype(f32)
    pos = lax.broadcasted_iota(i32, (t, 1), 0).astype(f32)
    for h in range(RET_HG):
        sl = slice(h * HEAD_DIM, (h + 1) * HEAD_DIM)
        lg = lg_ref[:, h * HEAD_DIM:h * HEAD_DIM + 1]
        q = q_ref[:, sl]
        k = k_ref[:, sl]
        v = v_ref[:, sl]
        decay = jnp.where(diff >= 0, jnp.exp(lg * jnp.maximum(diff, 0.0)), 0.0)
        scores = lax.dot_general(q, k, _NT, preferred_element_type=f32) * decay
        inner = jnp.dot(scores.astype(bf16), v, preferred_element_type=f32)
        st = state[h]
        cross = jnp.dot(q, st.astype(bf16), preferred_element_type=f32) * jnp.exp(lg * (pos + 1.0))
        kz = k.astype(f32) * jnp.exp(lg * (t - 1.0 - pos))
        kv = jnp.dot(kz.T.astype(bf16), v, preferred_element_type=f32)
        state[h] = jnp.exp(lg * t) * st + kv
        ret = inner + cross
        mu = jnp.mean(ret, axis=1, keepdims=True)
        rc = ret - mu
        var = jnp.mean(rc * rc, axis=1, keepdims=True)
        gate = g_ref[:, sl]
        out = rc * lax.rsqrt(var + LN_EPS) * gn_ref[:, sl] * (gate * _sigmoid(gate))
        o_ref[:, sl] = out.astype(o_ref.dtype)


def _retention(q, k, v, g, gn_g, batch, seq):
    m, w = q.shape
    heads = w // HEAD_DIM
    t = min(seq, 256)
    nc = seq // t
    gw = RET_HG * HEAD_DIM
    log_g = jnp.log1p(-jnp.exp2(-5.0 - jnp.arange(heads, dtype=f32)))
    lg = jnp.repeat(log_g, HEAD_DIM)[None, :]
    blk = pl.BlockSpec((t, gw), lambda b, gi, c: (b * nc + c, gi))
    vec = pl.BlockSpec((1, gw), lambda b, gi, c: (0, gi))
    return pl.pallas_call(
        _retention_kernel,
        grid=(batch, w // gw, nc),
        in_specs=[blk, blk, blk, blk, vec, vec],
        out_specs=blk,
        out_shape=jax.ShapeDtypeStruct((m, w), bf16),
        scratch_shapes=[pltpu.VMEM((RET_HG, HEAD_DIM, HEAD_DIM), f32)],
        compiler_params=_cparams(("parallel", "parallel", "arbitrary"), 16 * t * gw * 4 + 8 * t * t * 4),
        name="retention",
    )(q, k, v, g, gn_g[None, :], lg)


def _mixer_sparse_retention(h_hi, h_lo, pos2, w_in, w_out, gn_g, batch, seq):
    m, d = h_hi.shape
    gw = d // 2
    tm = min(m, 1024)
    tn = 512
    c128, s128 = _rope_tables(pos2, HEAD_DIM)
    c64, s64 = _rope_tables(pos2, IDX_DIM)
    w_t = jnp.swapaxes(w_in, 0, 1)
    rope = functools.partial(_proj_rope, h_hi, w_t, c_tab=c128, s_tab=s128, n=gw, tm=tm, tn=tn)
    plain = functools.partial(_mm, [h_hi], w_t, n=gw, tm=tm, tn=tn, transposed=True)
    aq = rope(col_off=0, scale=HEAD_DIM ** -0.5)
    ak = rope(col_off=gw, scale=1.0)
    av = plain(col_off=2 * gw, out_dtype=bf16)
    idx_raw = _mm3(h_hi, h_lo, w_t, col_off=3 * gw, n=IDX_RAW, tm=min(m, 512), tn=IDX_RAW // 3, transposed=True)
    iq, ik, iw = _idx_rope(idx_raw, c64, s64)
    mask = _dsa_index(iq, ik, iw, batch, seq)
    a_out = _dsa_attn(aq, ak, av, mask, batch, seq)
    b_off = 3 * gw + IDX_W + IDX_DIM + IDX_HEADS
    bq = rope(col_off=b_off, scale=1.0)
    bk = rope(col_off=b_off + gw, scale=HEAD_DIM ** -0.5)
    bv = plain(col_off=b_off + 2 * gw, out_dtype=bf16)
    bg = plain(col_off=b_off + 3 * gw, out_dtype=f32)
    b_out = _retention(bq, bk, bv, bg, gn_g, batch, seq)
    return _mm([a_out, b_out], w_out, col_off=0, n=d, tm=tm, tn=tn, out_dtype=f32)


MXU_N = 256


def _seg_sum64(x):
    r = lax.broadcasted_iota(i32, (MXU_N, MXU_N), 0) // C_HEAD_DIM
    c = lax.broadcasted_iota(i32, (MXU_N, MXU_N), 1) // C_HEAD_DIM
    ones = (r == c).astype(bf16)
    hi = x.astype(bf16)
    r1 = x - hi.astype(f32)
    mid = r1.astype(bf16)
    lo = (r1 - mid.astype(f32)).astype(bf16)
    cols = []
    for k in range(x.shape[1] // MXU_N):
        sl = slice(k * MXU_N, (k + 1) * MXU_N)
        s = jnp.dot(hi[:, sl], ones, preferred_element_type=f32)
        s += jnp.dot(mid[:, sl], ones, preferred_element_type=f32)
        s += jnp.dot(lo[:, sl], ones, preferred_element_type=f32)
        cols.append(s)
    return jnp.concatenate(cols, axis=1) if len(cols) > 1 else cols[0]


def _neg_softplus_neg(z):
    return jnp.minimum(z, 0.0) - jnp.log(1.0 + jnp.exp(-jnp.abs(z)))


def _rwkv_prep_kernel(p_ref, pp_ref, mu_ref, w0_ref, wup_ref, a0_ref, aup_ref, gup_ref, ka_ref, rk_ref,
                      r_o, w_o, k_o, a_o, v_o, g_o, bon_o, *, seq, gw):
    i = pl.program_id(0)
    tm = p_ref.shape[0]
    p = p_ref[...]
    prev_row = jnp.where((i * tm) % seq == 0, 0.0, pp_ref[SUBLANES - 1:SUBLANES, :])
    row = lax.broadcasted_iota(i32, (tm, 1), 0)
    shifted = jnp.where(row == 0, prev_row, pltpu.roll(p, 1, 0))
    pm = p + (shifted - p) * mu_ref[...]
    r = pm[:, 0:gw]
    k = pm[:, gw:2 * gw]
    v = pm[:, 2 * gw:3 * gw]
    o = 3 * gw
    dw = pm[:, o:o + C_DECAY_RANK]
    da = pm[:, o + C_DECAY_RANK:o + C_DECAY_RANK + C_ICLR_RANK]
    dg = pm[:, o + C_DECAY_RANK + C_ICLR_RANK:]
    w_log = _neg_softplus_neg(w0_ref[...] + _dot3(jnp.tanh(dw), wup_ref[...])) - 0.5
    decay = jnp.exp(-jnp.exp(w_log))
    a = _sigmoid(a0_ref[...] + _dot3(da, aup_ref[...]))
    g = _dot3(_sigmoid(dg), gup_ref[...])
    k2 = k * (1.0 + (a - 1.0) * ka_ref[...])
    r_o[...] = r
    w_o[...] = decay
    k_o[...] = k
    a_o[...] = a
    v_o[...] = v
    g_o[...] = g
    bon_o[...] = _seg_sum64(r * k2 * rk_ref[...]) * v


def _rwkv_prep(pc, mu, w0, w_up, a0, a_up, g_up, k_a, r_k, seq):
    m, cc = pc.shape
    gw = w0.shape[0]
    tm = 128
    nsub = tm // SUBLANES
    vec = pl.BlockSpec((1, gw), lambda i: (0, 0))
    out = pl.BlockSpec((tm, gw), lambda i: (i, 0))
    full = lambda a: pl.BlockSpec(a.shape, lambda i: (0, 0))
    return pl.pallas_call(
        functools.partial(_rwkv_prep_kernel, seq=seq, gw=gw),
        grid=(m // tm,),
        in_specs=[pl.BlockSpec((tm, cc), lambda i: (i, 0)),
                  pl.BlockSpec((SUBLANES, cc), lambda i: (jnp.maximum(i * nsub - 1, 0), 0)),
                  pl.BlockSpec((1, cc), lambda i: (0, 0)),
                  vec, full(w_up), vec, full(a_up), full(g_up), vec, vec],
        out_specs=[out] * 7,
        out_shape=[jax.ShapeDtypeStruct((m, gw), f32)] * 7,
        compiler_params=_cparams(("parallel",), 2 * tm * cc * 4 + 2 * 7 * tm * gw * 4 + 16 * tm * gw * 4),
        name="rwkv_prep",
    )(pc, pc, mu[None, :], w0[None, :], w_up, a0[None, :], a_up, g_up, k_a[None, :], r_k.reshape(1, -1))


SCAN_T = 64
SCAN_G = 16


def _rwkv_scan_kernel(w_ref, a_ref, k_ref, r_ref, v_ref, kkw_ref, kaw_ref, y_ref,
                      z_ref, w_s, kk_s, ka_s, k_s, r_s, v_s, y_s):
    c = pl.program_id(0)
    batch, steps, width = w_ref.shape
    nslab = z_ref.shape[0]
    heads = width // C_HEAD_DIM
    half = batch * heads

    @pl.when(c == 0)
    def _init():
        z_ref[...] = jnp.zeros_like(z_ref)

    kkw = kkw_ref[...]
    kaw = kaw_ref[...]

    def by_head(ref, rows, copies):
        per_batch = [ref[b, rows, :].reshape(SCAN_G, heads, C_HEAD_DIM) for b in range(batch)]
        return jnp.concatenate(per_batch * copies, axis=1)

    def key_tiles(ref, rows):
        return jnp.swapaxes(by_head(ref, rows, 2), 1, 2)

    def derive(g, carry):
        rows = pl.ds(pl.multiple_of(g * SCAN_G, SCAN_G), SCAN_G)
        a = key_tiles(a_ref, rows)
        kraw = key_tiles(k_ref, rows)
        kk = kraw * kkw
        kk = kk * lax.rsqrt(jnp.maximum(jnp.sum(kk * kk, axis=1, keepdims=True), 1e-24))
        w_s[rows] = key_tiles(w_ref, rows)
        r_s[rows] = key_tiles(r_ref, rows)
        kk_s[rows] = kk
        ka_s[rows] = kk * a
        k_s[rows] = kraw * (1.0 + (a - 1.0) * kaw)
        vt = jnp.swapaxes(by_head(v_ref, rows, 1), 1, 2)
        v_s[rows] = jnp.concatenate([vt[:, :nslab, :], vt[:, nslab:, :]], axis=2)
        return carry

    lax.fori_loop(0, steps // SCAN_G, derive, 0)

    def step(t, carry):
        w = w_s[t]
        kk = kk_s[t]
        ka = ka_s[t]
        k = k_s[t]
        r = r_s[t]
        vrows = v_s[t]
        for s in range(nslab):
            z = z_ref[s]
            sk = jnp.sum(z * kk, axis=0, keepdims=True)
            zn = z * w - ka * sk + k * vrows[s:s + 1, :]
            z_ref[s] = zn
            y_s[t, s:s + 1, :] = jnp.sum(zn * r, axis=0, keepdims=True)
        return carry

    lax.fori_loop(0, steps, step, 0)

    def restore(g, carry):
        rows = pl.ds(pl.multiple_of(g * SCAN_G, SCAN_G), SCAN_G)
        y = y_s[rows]
        yt = jnp.concatenate([y[:, :, :half], y[:, :, half:]], axis=1)
        yh = jnp.swapaxes(yt, 1, 2)
        for b in range(batch):
            y_ref[b, rows, :] = yh[:, b * heads:(b + 1) * heads, :].reshape(SCAN_G, width)
        return carry

    lax.fori_loop(0, steps // SCAN_G, restore, 0)


def _rwkv_scan(w, a, k, r, v, k_k, k_a, batch, seq):
    width = w.shape[1]
    heads = width // C_HEAD_DIM
    assert 2 * batch * heads == LANES
    nslab = C_HEAD_DIM // 2
    steps = min(seq, SCAN_T)
    consts = [jnp.tile(p.reshape(heads, C_HEAD_DIM).T, (1, 2 * batch)) for p in (k_k, k_a)]
    nat = pl.BlockSpec((batch, steps, width), lambda c: (0, c, 0))
    const_spec = pl.BlockSpec((C_HEAD_DIM, LANES), lambda c: (0, 0))
    tile_bytes = steps * C_HEAD_DIM * LANES * 4
    y = pl.pallas_call(
        _rwkv_scan_kernel,
        grid=(seq // steps,),
        in_specs=[nat] * 5 + [const_spec, const_spec],
        out_specs=nat,
        out_shape=jax.ShapeDtypeStruct((batch, seq, width), f32),
        scratch_shapes=[pltpu.VMEM((nslab, C_HEAD_DIM, LANES), f32)]
        + [pltpu.VMEM((steps, C_HEAD_DIM, LANES), f32)] * 5
        + [pltpu.VMEM((steps, nslab, LANES), f32)] * 2,
        compiler_params=_cparams(("arbitrary",), 6 * tile_bytes + 2 * 6 * batch * steps * width * 4
                                 + 8 * tile_bytes),
        name="rwkv_scan",
    )(*(x.reshape(batch, seq, width) for x in (w, a, k, r, v)), *consts)
    return y.reshape(batch * seq, width)


def _rwkv_post_kernel(y_ref, g_ref, bon_ref, lng_ref, lnb_ref, o_ref):
    y = y_ref[...]
    mu = _seg_sum64(y) * (1.0 / C_HEAD_DIM)
    yc = y - mu
    var = _seg_sum64(yc * yc) * (1.0 / C_HEAD_DIM)
    yn = yc * lax.rsqrt(var + C_EPS) * lng_ref[...] + lnb_ref[...]
    o_ref[...] = ((yn + bon_ref[...]) * g_ref[...]).astype(o_ref.dtype)


def _rwkv_post(y, g, bonus, ln_g, ln_b):
    m, gw = y.shape
    tm = 256
    blk = pl.BlockSpec((tm, gw), lambda i: (i, 0))
    vec = pl.BlockSpec((1, gw), lambda i: (0, 0))
    return pl.pallas_call(
        _rwkv_post_kernel,
        grid=(m // tm,),
        in_specs=[blk, blk, blk, vec, vec],
        out_specs=blk,
        out_shape=jax.ShapeDtypeStruct((m, gw), bf16),
        compiler_params=_cparams(("parallel",), 16 * tm * gw * 4),
        name="rwkv_post",
    )(y, g, bonus, ln_g[None, :], ln_b[None, :])


GELU_C = 0.7978845608028654


def _lru_kernel(px_ref, pg_ref, cw_ref, cb_ref, wa_ref, ba_ref, wx_ref, bx_ref, lam_ref, o_ref,
                tail, hcar, a_s, b_s):
    c = pl.program_id(1)
    t = px_ref.shape[0]

    @pl.when(c == 0)
    def _init():
        tail[...] = jnp.zeros_like(tail)
        hcar[...] = jnp.zeros_like(hcar)

    x = px_ref[...]
    ext = jnp.concatenate([tail[...], x], axis=0)
    xc = cb_ref[...]
    for j in range(D_CONV):
        off = SUBLANES - (D_CONV - 1) + j
        xc = xc + cw_ref[j:j + 1, :] * ext[off:off + t, :]
    tail[...] = x[t - SUBLANES:, :]
    lam = lam_ref[...]
    sp = jnp.maximum(-lam, 0.0) + jnp.log(1.0 + jnp.exp(-jnp.abs(lam)))
    bw = wa_ref.shape[1]
    for n in range(wa_ref.shape[0]):
        sl = slice(n * bw, (n + 1) * bw)
        xb = xc[:, sl]
        rg = _sigmoid(_dot3(xb, wa_ref[n]) + ba_ref[:, sl])
        ig = _sigmoid(_dot3(xb, wx_ref[n]) + bx_ref[:, sl])
        log_a = -LRU_C * rg * sp[:, sl]
        a_s[:, sl] = jnp.exp(log_a)
        th = jnp.tanh(log_a)
        one_minus_a2 = -2.0 * th / (1.0 - th)
        b_s[:, sl] = jnp.sqrt(one_minus_a2) * (ig * xb)

    sub = lax.broadcasted_iota(i32, (SUBLANES, a_s.shape[1]), 0)

    def rows8(g, h):
        r0 = pl.multiple_of(g * SUBLANES, SUBLANES)
        a = a_s[pl.ds(r0, SUBLANES), :]
        b = b_s[pl.ds(r0, SUBLANES), :]
        for s in (1, 2, 4):
            b = a * jnp.where(sub >= s, pltpu.roll(b, s, 0), 0.0) + b
            a = a * jnp.where(sub >= s, pltpu.roll(a, s, 0), 1.0)
        hs = a * h + b
        b_s[pl.ds(r0, SUBLANES), :] = hs
        return hs[SUBLANES - 1:SUBLANES, :]

    hcar[...] = lax.fori_loop(0, t // SUBLANES, rows8, hcar[...])
    gate = pg_ref[...]
    gelu = 0.5 * gate * (1.0 + jnp.tanh(GELU_C * (gate + 0.044715 * (gate * gate * gate))))
    o_ref[...] = (b_s[...] * gelu).astype(o_ref.dtype)


def _lru(px, pg, conv_w, conv_b, w_a, b_a, w_x, b_x, lam, batch, seq):
    m, w = px.shape
    t = min(seq, 256)
    nc = seq // t
    blk = pl.BlockSpec((t, w), lambda b, c: (b * nc + c, 0))
    vec = pl.BlockSpec((1, w), lambda b, c: (0, 0))
    wblk = pl.BlockSpec(w_a.shape, lambda b, c: (0, 0, 0))
    return pl.pallas_call(
        _lru_kernel,
        grid=(batch, nc),
        in_specs=[blk, blk, pl.BlockSpec((D_CONV, w), lambda b, c: (0, 0)), vec, wblk, vec, wblk, vec, vec],
        out_specs=blk,
        out_shape=jax.ShapeDtypeStruct((m, w), bf16),
        scratch_shapes=[pltpu.VMEM((SUBLANES, w), f32), pltpu.VMEM((1, w), f32),
                        pltpu.VMEM((t, w), f32), pltpu.VMEM((t, w), f32)],
        compiler_params=_cparams(("parallel", "arbitrary"), 16 * t * w * 4),
        name="lru",
    )(px, pg, conv_w, conv_b[None, :], w_a, b_a[None, :], w_x, b_x[None, :], lam[None, :])


def _mixer_rwkv_lru(h_hi, w_in, w_out, mu, w0, w_up, a0, a_up, g_up, k_k, k_a, r_k, ln_g, ln_b,
                    conv_w, conv_b, w_a, b_a, w_x, b_x, lam, batch, seq):
    m, d = h_hi.shape
    gw = d // 2
    c_cols = 3 * gw + C_DECAY_RANK + C_ICLR_RANK + C_GATE_RANK
    tm = min(m, 1024)
    tn = 512
    proj = functools.partial(_mm, [h_hi], w_in, tm=tm, tn=tn, out_dtype=f32)
    pc = proj(col_off=0, n=c_cols)
    pg = proj(col_off=c_cols, n=gw)
    px = proj(col_off=c_cols + gw, n=gw)
    r, w, k, a, v, g, bonus = _rwkv_prep(pc, mu, w0, w_up, a0, a_up, g_up, k_a, r_k, seq)
    y = _rwkv_scan(w, a, k, r, v, k_k, k_a, batch, seq)
    c_out = _rwkv_post(y, g, bonus, ln_g, ln_b)
    d_out = _lru(px, pg, conv_w, conv_b, w_a, b_a, w_x, b_x, lam, batch, seq)
    return _mm([c_out, d_out], w_out, col_off=0, n=d, tm=tm, tn=tn, out_dtype=f32)


ADA_ROWS = 16


def _ada_kernel(c_ref, w_ref, b_ref, o_ref):
    c = c_ref[...]
    s = (c * _sigmoid(c)).astype(bf16)
    o_ref[...] = jnp.dot(s, w_ref[...].astype(bf16), preferred_element_type=f32) + b_ref[...]


def _ada(c, ada_w, ada_b):
    batch, d = c.shape
    n = ada_w.shape[1]
    tn = 512
    cp = jnp.zeros((ADA_ROWS, d), f32).at[:batch].set(c)
    out = pl.pallas_call(
        _ada_kernel,
        grid=(n // tn,),
        in_specs=[pl.BlockSpec((ADA_ROWS, d), lambda j: (0, 0)),
                  pl.BlockSpec((d, tn), lambda j: (0, j)),
                  pl.BlockSpec((1, tn), lambda j: (0, j))],
        out_specs=pl.BlockSpec((ADA_ROWS, tn), lambda j: (0, j)),
        out_shape=jax.ShapeDtypeStruct((ADA_ROWS, n), f32),
        compiler_params=_cparams(("parallel",), 3 * d * tn * 4),
        name="ada",
    )(cp, ada_w, ada_b[None, :])
    return out[:batch]


def kernel(x, c, positions, ada_w, ada_b, ada_table, ln_g, ln_b, ab_w_in, ab_w_out, ret_gn_g, cd_w_in, cd_w_out, rwkv_mu, rwkv_w0, rwkv_w_up, rwkv_a0, rwkv_a_up, rwkv_g_up, rwkv_k_k, rwkv_k_a, rwkv_r_k, rwkv_ln_g, rwkv_ln_b, lru_conv_w, lru_conv_b, lru_w_a, lru_b_a, lru_w_x, lru_b_x, lru_lambda, moe_w_grp, moe_b_grp, moe_w_exp, moe_b_exp, moe_w_gate, moe_w_up, moe_w_down):
    batch, seq, d = x.shape
    x2 = x.reshape(batch * seq, d)
    pos2 = positions.reshape(batch * seq, 1)
    ada = _ada(c, ada_w, ada_b).reshape(batch, 6, 1, d)
    mods = [ada + ada_table[layer][None, :, None, :] for layer in range(DEPTH)]
    h_hi, h_lo = _modcast(x2, mods[0][:, 1], mods[0][:, 0], seq)
    for layer in range(DEPTH):
        shift_m, scale_m, gate_m, shift_f, scale_f, gate_f = (mods[layer][:, i] for i in range(6))
        j = layer // 2
        if layer % 2 == 0:
            y = _mixer_sparse_retention(h_hi, h_lo, pos2, ab_w_in[j], ab_w_out[j], ret_gn_g[j], batch, seq)
        else:
            y = _mixer_rwkv_lru(h_hi, cd_w_in[j], cd_w_out[j], rwkv_mu[j], rwkv_w0[j], rwkv_w_up[j],
                                rwkv_a0[j], rwkv_a_up[j], rwkv_g_up[j], rwkv_k_k[j], rwkv_k_a[j],
                                rwkv_r_k[j], rwkv_ln_g[j], rwkv_ln_b[j], lru_conv_w[j], lru_conv_b[j],
                                lru_w_a[j], lru_b_a[j], lru_w_x[j], lru_b_x[j], lru_lambda[j], batch, seq)
        last = layer + 1 == DEPTH
        next_mod = None if last else (mods[layer + 1][:, 1], mods[layer + 1][:, 0])
        out = _moe_layer(x2, y, gate_m, ln_g[layer, 0], ln_b[layer, 0], scale_f, shift_f, gate_f,
                         ln_g[layer, 1], ln_b[layer, 1], moe_w_grp[layer], moe_b_grp[layer],
                         moe_w_exp[layer], moe_b_exp[layer], moe_w_gate, moe_w_up, moe_w_down, layer, seq,
                         next_mod)
        x2, h_hi, h_lo = (out, None, None) if last else out
    return x2.reshape(batch, seq, d)
```

```python
import functools

import jax
import jax.numpy as jnp
from jax import lax
from jax.experimental import pallas as pl
from jax.experimental.pallas import tpu as pltpu

f32 = jnp.float32
bf16 = jnp.bfloat16
i32 = jnp.int32

DEPTH = 2
CHUNK = 64
ROPE_THETA = 10000.0
LN_EPS = 1e-5
ALPHA = (2 * DEPTH) ** 0.25
HEAD_DIM = 128
IDX_HEADS = 16
IDX_DIM = 64
TOPK_MAX = 256
C_HEAD_DIM = 64
C_DECAY_RANK = 128
C_ICLR_RANK = 128
C_GATE_RANK = 256
C_EPS = 64e-5
D_BLOCKS = 16
D_CONV = 4
LRU_C = 8.0
N_GROUPS = 4
EXPERTS_PER_GROUP = 8
N_EXPERTS = N_GROUPS * EXPERTS_PER_GROUP
D_EXPERT = 512

LANES = 128
SUBLANES = 8
VMEM_BYTES_V7X = 64 * 1024 * 1024
VMEM_HEADROOM = 8 * 1024 * 1024


def _cparams(semantics, vmem_bytes):
    limit = min(int(vmem_bytes) + VMEM_HEADROOM, VMEM_BYTES_V7X - VMEM_HEADROOM)
    return pltpu.CompilerParams(dimension_semantics=semantics, vmem_limit_bytes=limit)


def _split_bf16(a):
    hi = a.astype(bf16)
    lo = (a - hi.astype(f32)).astype(bf16)
    return hi, lo


def _dot3(a, b, dims=(((1,), (0,)), ((), ()))):
    ah, al = _split_bf16(a)
    bh, bl = _split_bf16(b)
    dg = functools.partial(lax.dot_general, dimension_numbers=dims, preferred_element_type=f32)
    return dg(ah, bh) + dg(ah, bl) + dg(al, bh)


def _sigmoid(x):
    return 1.0 / (1.0 + jnp.exp(-x))


def _modcast_kernel(x_ref, sc_ref, sh_ref, hi_ref, lo_ref):
    h = x_ref[...] * (1.0 + sc_ref[0]) + sh_ref[0]
    hi, lo = _split_bf16(h)
    hi_ref[...] = hi
    lo_ref[...] = lo


def _modcast(x2, sc, sh, seq):
    m, k = x2.shape
    tm = 256
    row = pl.BlockSpec((tm, k), lambda i: (i, 0))
    per_batch = pl.BlockSpec((1, 1, k), lambda i: (i * tm // seq, 0, 0))
    return pl.pallas_call(
        _modcast_kernel,
        grid=(m // tm,),
        in_specs=[row, per_batch, per_batch],
        out_specs=[row, row],
        out_shape=[jax.ShapeDtypeStruct((m, k), bf16)] * 2,
        compiler_params=_cparams(("parallel",), 2 * tm * k * (4 + 2 + 2)),
        name="modcast",
    )(x2, sc, sh)


_NN = (((1,), (0,)), ((), ()))
_NT = (((1,), (1,)), ((), ()))


def _mm_kernel(*refs, nparts, epilogue, dims):
    acc = None
    for p in range(nparts):
        d = lax.dot_general(refs[p][...], refs[nparts + p][...].astype(bf16), dims, preferred_element_type=f32)
        acc = d if acc is None else acc + d
    epilogue(acc, *refs[2 * nparts:])


def _mm3_kernel(xh_ref, xl_ref, w_ref, o_ref, *, dims):
    wh, wl = _split_bf16(w_ref[...])
    xh = xh_ref[...]
    dg = functools.partial(lax.dot_general, dimension_numbers=dims, preferred_element_type=f32)
    o_ref[...] = dg(xh, wh) + dg(xh, wl) + dg(xl_ref[...], wh)


def _w_spec(kp, tn, k_off, col_off, transposed):
    if transposed:
        assert col_off % SUBLANES == 0 and tn % SUBLANES == 0
        return pl.BlockSpec((pl.Element(tn), pl.Element(kp)),
                            lambda i, j: (pl.multiple_of(col_off + j * tn, SUBLANES), k_off))
    assert col_off % tn == 0 and k_off % kp == 0
    return pl.BlockSpec((kp, tn), lambda i, j: (k_off // kp, col_off // tn + j))


def _store_epilogue(acc, o_ref):
    o_ref[...] = acc.astype(o_ref.dtype)


def _rope_epilogue(acc, c_ref, s_ref, o_ref, *, scale):
    c = c_ref[...]
    s = s_ref[...]
    for k in range(acc.shape[1] // LANES):
        blk = acc[:, k * LANES:(k + 1) * LANES]
        rot = blk * c + pltpu.roll(blk, LANES // 2, 1) * s
        if scale != 1.0:
            rot = rot * scale
        o_ref[:, k * LANES:(k + 1) * LANES] = rot.astype(o_ref.dtype)


def _mm(xs, w, *, col_off, n, tm, tn, out_dtype, epilogue=_store_epilogue, extra=(), extra_specs=(),
        transposed=False):
    m = xs[0].shape[0]
    kp = xs[0].shape[1]
    nparts = len(xs)
    assert all(x.shape == (m, kp) for x in xs) and w.shape[1 if transposed else 0] == nparts * kp
    assert m % tm == 0 and n % tn == 0
    x_specs = [pl.BlockSpec((tm, kp), lambda i, j: (i, 0)) for _ in xs]
    w_specs = [_w_spec(kp, tn, p * kp, col_off, transposed) for p in range(nparts)]
    vmem = 2 * nparts * (tm * kp * 2 + kp * tn * 4) + nparts * kp * tn * 2 + 4 * tm * tn * 4
    return pl.pallas_call(
        functools.partial(_mm_kernel, nparts=nparts, epilogue=epilogue, dims=_NT if transposed else _NN),
        grid=(m // tm, n // tn),
        in_specs=x_specs + w_specs + list(extra_specs),
        out_specs=pl.BlockSpec((tm, tn), lambda i, j: (i, j)),
        out_shape=jax.ShapeDtypeStruct((m, n), out_dtype),
        compiler_params=_cparams(("parallel", "arbitrary"), vmem),
        name="mm",
    )(*xs, *([w] * nparts), *extra)


def _mm3(xh, xl, w, *, col_off, n, tm, tn, transposed=False):
    m, k = xh.shape
    assert m % tm == 0 and n % tn == 0
    xspec = pl.BlockSpec((tm, k), lambda i, j: (i, 0))
    vmem = 2 * (2 * tm * k * 2 + k * tn * 4) + 2 * k * tn * 2 + 4 * tm * tn * 4
    return pl.pallas_call(
        functools.partial(_mm3_kernel, dims=_NT if transposed else _NN),
        grid=(m // tm, n // tn),
        in_specs=[xspec, xspec, _w_spec(k, tn, 0, col_off, transposed)],
        out_specs=pl.BlockSpec((tm, tn), lambda i, j: (i, j)),
        out_shape=jax.ShapeDtypeStruct((m, n), f32),
        compiler_params=_cparams(("parallel", "arbitrary"), vmem),
        name="mm3",
    )(xh, xl, w)


def _res_ln(x, y, gate, g, b):
    z = ALPHA * x + (1.0 + gate) * y
    mu = jnp.mean(z, axis=-1, keepdims=True)
    zc = z - mu
    var = jnp.mean(zc * zc, axis=-1, keepdims=True)
    return zc * lax.rsqrt(var + LN_EPS) * g + b


def _ln_router_kernel(x_ref, y_ref, gate_ref, g_ref, b_ref, sc_ref, sh_ref, wr_ref, br_ref,
                      xo_ref, h_ref, rw_ref, rid_ref):
    xn = _res_ln(x_ref[...], y_ref[...], gate_ref[0], g_ref[...], b_ref[...])
    xo_ref[...] = xn
    h = xn * (1.0 + sc_ref[0]) + sh_ref[0]
    h_ref[...] = h
    logits = _dot3(h, wr_ref[...]) + br_ref[...]
    lane = lax.broadcasted_iota(i32, logits.shape, 1)
    neg = jnp.float32(-jnp.inf)
    is_grp = lane < N_GROUPS
    gl = jnp.where(is_grp, logits, neg)
    gmax = jnp.max(gl, axis=1, keepdims=True)
    gidx = jnp.min(jnp.where(gl == gmax, lane, LANES), axis=1, keepdims=True)
    gsum = jnp.sum(jnp.where(is_grp, jnp.exp(gl - gmax), 0.0), axis=1, keepdims=True)
    grp_p = 1.0 / gsum
    lo = N_GROUPS + EXPERTS_PER_GROUP * gidx
    in_grp = jnp.logical_and(lane >= lo, lane < lo + EXPERTS_PER_GROUP)
    el = jnp.where(in_grp, logits, neg)
    m1 = jnp.max(el, axis=1, keepdims=True)
    i1 = jnp.min(jnp.where(el == m1, lane, LANES), axis=1, keepdims=True)
    el2 = jnp.where(lane == i1, neg, el)
    m2 = jnp.max(el2, axis=1, keepdims=True)
    i2 = jnp.min(jnp.where(el2 == m2, lane, LANES), axis=1, keepdims=True)
    e2 = jnp.exp(m2 - m1)
    w1 = grp_p / (1.0 + e2)
    w2 = grp_p * e2 / (1.0 + e2)
    rw_ref[...] = jnp.where(lane == 0, w1, jnp.where(lane == 1, w2, 0.0))
    rid_ref[...] = jnp.where(lane == 0, i1 - N_GROUPS, jnp.where(lane == 1, i2 - N_GROUPS, 0))


def _ln_router(x2, y2, gate, g, b, sc, sh, wr, br, seq):
    m, d = x2.shape
    tm = 256
    row = pl.BlockSpec((tm, d), lambda i: (i, 0))
    per_batch = pl.BlockSpec((1, 1, d), lambda i: (i * tm // seq, 0, 0))
    vec = pl.BlockSpec((1, d), lambda i: (0, 0))
    small = pl.BlockSpec((tm, LANES), lambda i: (i, 0))
    return pl.pallas_call(
        _ln_router_kernel,
        grid=(m // tm,),
        in_specs=[row, row, per_batch, vec, vec, per_batch, per_batch,
                  pl.BlockSpec((d, LANES), lambda i: (0, 0)), pl.BlockSpec((1, LANES), lambda i: (0, 0))],
        out_specs=[row, row, small, small],
        out_shape=[jax.ShapeDtypeStruct((m, d), f32), jax.ShapeDtypeStruct((m, d), f32),
                   jax.ShapeDtypeStruct((m, LANES), f32), jax.ShapeDtypeStruct((m, LANES), i32)],
        compiler_params=_cparams(("parallel",), 12 * tm * d * 4 + 2 * d * LANES * 4),
        name="ln_router",
    )(x2, y2, gate, g, b, sc, sh, wr, br)


MOE_TM = 256
MOE_HALF = D_EXPERT // 2
DMA_UNROLL = 8
DMA_PRIORITIES = 2


def _row_copy(src_hbm, row, dst_vmem, r, sem):
    return pltpu.make_async_copy(src_hbm.at[pl.ds(row, 1), :], dst_vmem.at[pl.ds(r, 1), :], sem)


def _gather_start(idx_ref, base, src_hbm, dst_vmem, sem, nrows, stride=1):
    def issue(p, c):
        for prio in range(DMA_PRIORITIES):
            r = p * DMA_PRIORITIES + prio
            _row_copy(src_hbm, idx_ref[base + r * stride], dst_vmem, r, sem).start(priority=prio)
        return c

    lax.fori_loop(0, nrows // DMA_PRIORITIES, issue, 0, unroll=DMA_UNROLL // DMA_PRIORITIES)


def _gather_wait(src_hbm, dst_vmem, sem, nrows):
    def wait(r, c):
        _row_copy(src_hbm, 0, dst_vmem, r, sem).wait()
        return c

    lax.fori_loop(0, nrows, wait, 0, unroll=DMA_UNROLL)


MOE_NHALF = D_EXPERT // MOE_HALF


def _moe_ffn_kernel(te_ref, tv_ref, tok_ref, h_hbm, wg_hbm, wu_hbm, wd_hbm, o_ref,
                    xbuf, wgb, wub, wdb, xsems, wsems, *, layer):
    i = pl.program_id(0)
    n = pl.num_programs(0)
    slot = i % 2
    valid = tv_ref[i] > 0
    nxt = jnp.minimum(i + 1, n - 1)
    has_next = jnp.logical_and(i + 1 < n, tv_ref[nxt] > 0)
    e = te_ref[i]
    next_e = te_ref[nxt]
    first = jnp.logical_or(i == 0, e != te_ref[jnp.maximum(i - 1, 0)])
    last = jnp.logical_and(has_next, next_e != e)

    def gather(tile, s):
        _gather_start(tok_ref, tile * MOE_TM, h_hbm, xbuf.at[s], xsems.at[s], MOE_TM)

    def weight_copies(expert, hf):
        cs = pl.ds(hf * MOE_HALF, MOE_HALF)
        return (pltpu.make_async_copy(wg_hbm.at[layer, expert, :, cs], wgb.at[hf], wsems.at[hf, 0]),
                pltpu.make_async_copy(wu_hbm.at[layer, expert, :, cs], wub.at[hf], wsems.at[hf, 1]),
                pltpu.make_async_copy(wd_hbm.at[layer, expert, cs, :], wdb.at[hf], wsems.at[hf, 2]))

    @pl.when(jnp.logical_and(i == 0, valid))
    def _prime():
        gather(0, 0)
        for hf in range(MOE_NHALF):
            for cp in weight_copies(e, hf):
                cp.start()

    def gather_next_part(part, nparts):
        rows = MOE_TM // nparts
        for r in range(part * rows, (part + 1) * rows):
            _row_copy(h_hbm, tok_ref[nxt * MOE_TM + r], xbuf.at[1 - slot], r,
                      xsems.at[1 - slot]).start(priority=r % DMA_PRIORITIES)

    @pl.when(valid)
    def _compute():
        _gather_wait(h_hbm, xbuf.at[slot], xsems.at[slot], MOE_TM)
        xb = xbuf[slot].astype(bf16)
        y = None
        for hf in range(MOE_NHALF):
            @pl.when(first)
            def _arrived(hf=hf):
                for cp in weight_copies(e, hf):
                    cp.wait()

            gather_next_part(2 * hf, 2 * MOE_NHALF)
            gate = jnp.dot(xb, wgb[hf].astype(bf16), preferred_element_type=f32)
            up = jnp.dot(xb, wub[hf].astype(bf16), preferred_element_type=f32)
            hid = gate * _sigmoid(gate) * up
            gather_next_part(2 * hf + 1, 2 * MOE_NHALF)
            part = jnp.dot(hid.astype(bf16), wdb[hf].astype(bf16), preferred_element_type=f32)
            y = part if y is None else y + part

            @pl.when(last)
            def _reload(hf=hf):
                for cp in weight_copies(next_e, hf):
                    cp.start()

        o_ref[...] = y

        @pl.when(jnp.logical_not(has_next))
        def _drain():
            _gather_wait(h_hbm, xbuf.at[1 - slot], xsems.at[1 - slot], MOE_TM)

    @pl.when(jnp.logical_not(valid))
    def _empty():
        o_ref[...] = jnp.zeros_like(o_ref)


def _moe_ffn(h2, w_gate, w_up, w_down, layer, tile_e, tile_valid, row_tok, n_tiles):
    n, d = h2.shape
    tm = MOE_TM
    anyspace = pl.BlockSpec(memory_space=pl.ANY)
    grid_spec = pltpu.PrefetchScalarGridSpec(
        num_scalar_prefetch=3,
        grid=(n_tiles,),
        in_specs=[anyspace, anyspace, anyspace, anyspace],
        out_specs=pl.BlockSpec((tm, d), lambda i, te, tv, tok: (i, 0)),
        scratch_shapes=[pltpu.VMEM((2, tm, d), f32),
                        pltpu.VMEM((MOE_NHALF, d, MOE_HALF), f32),
                        pltpu.VMEM((MOE_NHALF, d, MOE_HALF), f32),
                        pltpu.VMEM((MOE_NHALF, MOE_HALF, d), f32),
                        pltpu.SemaphoreType.DMA((2,)),
                        pltpu.SemaphoreType.DMA((MOE_NHALF, 3))],
    )
    vmem = 3 * d * D_EXPERT * 4 + 3 * d * MOE_HALF * 2 + 2 * tm * d * 4 + tm * d * 2 + 4 * tm * d * 4
    return pl.pallas_call(
        functools.partial(_moe_ffn_kernel, layer=layer),
        grid_spec=grid_spec,
        out_shape=jax.ShapeDtypeStruct((n_tiles * tm, d), f32),
        compiler_params=_cparams(("arbitrary",), vmem),
        name="moe_ffn",
    )(tile_e, tile_valid, row_tok, h2, w_gate, w_up, w_down)


def _moe_combine_kernel(pos_ref, ys_hbm, rw_ref, x_ref, gate_ref, g_ref, b_ref, *rest, next_mod):
    if next_mod:
        sc_ref, sh_ref, o_ref, hi_ref, lo_ref, buf, sems = rest
    else:
        o_ref, buf, sems = rest
    i = pl.program_id(0)
    n = pl.num_programs(0)
    tm = x_ref.shape[0]
    slot = i % 2

    def start(tile, s):
        for k in range(2):
            _gather_start(pos_ref, 2 * tile * tm + k, ys_hbm, buf.at[s, k], sems.at[s, k], tm, stride=2)

    @pl.when(i == 0)
    def _first_gather():
        start(0, 0)

    @pl.when(i + 1 < n)
    def _next_gather():
        start(i + 1, 1 - slot)

    for k in range(2):
        _gather_wait(ys_hbm, buf.at[slot, k], sems.at[slot, k], tm)
    rw = rw_ref[...]
    y = rw[:, 0:1] * buf[slot, 0] + rw[:, 1:2] * buf[slot, 1]
    xn = _res_ln(x_ref[...], y, gate_ref[0], g_ref[...], b_ref[...])
    o_ref[...] = xn
    if next_mod:
        hi, lo = _split_bf16(xn * (1.0 + sc_ref[0]) + sh_ref[0])
        hi_ref[...] = hi
        lo_ref[...] = lo


def _moe_combine(pos, ys, rw, x2, gate, g, b, seq, next_mod=None):
    n, d = x2.shape
    tm = 256
    row = pl.BlockSpec((tm, d), lambda i, pos: (i, 0))
    per_batch = pl.BlockSpec((1, 1, d), lambda i, pos: (i * tm // seq, 0, 0))
    vec = pl.BlockSpec((1, d), lambda i, pos: (0, 0))
    extra = () if next_mod is None else tuple(next_mod)
    grid_spec = pltpu.PrefetchScalarGridSpec(
        num_scalar_prefetch=1,
        grid=(n // tm,),
        in_specs=[pl.BlockSpec(memory_space=pl.ANY), pl.BlockSpec((tm, LANES), lambda i, pos: (i, 0)),
                  row, per_batch, vec, vec] + [per_batch] * len(extra),
        out_specs=[row] * (1 + len(extra)) if extra else row,
        scratch_shapes=[pltpu.VMEM((2, 2, tm, d), f32), pltpu.SemaphoreType.DMA((2, 2))],
    )
    x_shape = jax.ShapeDtypeStruct((n, d), f32)
    mod_shape = jax.ShapeDtypeStruct((n, d), bf16)
    return pl.pallas_call(
        functools.partial(_moe_combine_kernel, next_mod=bool(extra)),
        grid_spec=grid_spec,
        out_shape=[x_shape, mod_shape, mod_shape] if extra else x_shape,
        compiler_params=_cparams(("arbitrary",), 14 * tm * d * 4),
        name="moe_combine",
    )(pos, ys, rw, x2, gate, g, b, *extra)


def _moe_plan(eid, n_tiles):
    tm = MOE_TM
    flat_e = eid.reshape(-1)
    onehot = (flat_e[:, None] == jnp.arange(N_EXPERTS, dtype=i32)[None, :]).astype(i32)
    csum = jnp.cumsum(onehot, axis=0)
    rank = jnp.sum((csum - onehot) * onehot, axis=1)
    counts = csum[-1]
    padded = ((counts + tm - 1) // tm) * tm
    ends = jnp.cumsum(padded)
    pos = ((ends - padded)[flat_e] + rank).astype(i32)
    tile_start = jnp.arange(n_tiles, dtype=i32) * tm
    tile_valid = (tile_start < ends[-1]).astype(i32)
    tile_e = jnp.searchsorted(ends, tile_start, side="right").astype(i32)
    last_valid = jnp.maximum(ends[-1] // tm - 1, 0)
    tile_e = jnp.where(tile_valid > 0, tile_e, tile_e[last_valid])
    row_tok = jnp.zeros((n_tiles * tm,), i32).at[pos].set(jnp.arange(flat_e.shape[0], dtype=i32) // 2)
    return pos, tile_e, tile_valid, row_tok


def _moe_layer(x2, y2, gate_m, ln_g, ln_b, scale_f, shift_f, gate_f, ln_g2, ln_b2,
               w_grp, b_grp, w_exp, b_exp, w_gate, w_up, w_down, layer, seq, next_mod=None):
    n, d = x2.shape
    pad = LANES - N_GROUPS - N_EXPERTS
    wr = jnp.concatenate([w_grp, w_exp, jnp.zeros((d, pad), f32)], axis=1)
    br = jnp.concatenate([b_grp, b_exp, jnp.zeros((pad,), f32)])[None, :]
    x1, h, rw, rid = _ln_router(x2, y2, gate_m, ln_g[None, :], ln_b[None, :], scale_f, shift_f, wr, br, seq)
    n_tiles = (2 * n) // MOE_TM + N_EXPERTS
    pos, tile_e, tile_valid, row_tok = _moe_plan(rid[:, :2], n_tiles)
    ys = _moe_ffn(h, w_gate, w_up, w_down, layer, tile_e, tile_valid, row_tok, n_tiles)
    return _moe_combine(pos, ys, rw, x1, gate_f, ln_g2[None, :], ln_b2[None, :], seq, next_mod)


def _rope_table_kernel(pos_ref, fr_ref, sg_ref, c_ref, s_ref):
    ang = pos_ref[...].astype(f32) * fr_ref[...]
    c_ref[...] = jnp.cos(ang)
    s_ref[...] = jnp.sin(ang) * sg_ref[...]


def _rope_tables(pos2, head_dim):
    m = pos2.shape[0]
    half = head_dim // 2
    freqs = ROPE_THETA ** (-jnp.arange(half, dtype=f32) / half)
    reps = LANES // head_dim
    fr = jnp.tile(jnp.concatenate([freqs, freqs]), reps)[None, :]
    sg = jnp.tile(jnp.concatenate([-jnp.ones((half,), f32), jnp.ones((half,), f32)]), reps)[None, :]
    tm = min(m, 1024)
    vec = pl.BlockSpec((1, LANES), lambda i: (0, 0))
    out = pl.BlockSpec((tm, LANES), lambda i: (i, 0))
    return pl.pallas_call(
        _rope_table_kernel,
        grid=(m // tm,),
        in_specs=[pl.BlockSpec((tm, 1), lambda i: (i, 0)), vec, vec],
        out_specs=[out, out],
        out_shape=[jax.ShapeDtypeStruct((m, LANES), f32)] * 2,
        compiler_params=_cparams(("parallel",), 8 * tm * LANES * 4),
        name="rope_tables",
    )(pos2, fr, sg)


def _proj_rope(h_hi, w_t, c_tab, s_tab, *, col_off, n, scale, tm, tn):
    tab = pl.BlockSpec((tm, LANES), lambda i, j: (i, 0))
    return _mm([h_hi], w_t, col_off=col_off, n=n, tm=tm, tn=tn, out_dtype=bf16, transposed=True,
               epilogue=functools.partial(_rope_epilogue, scale=scale),
               extra=(c_tab, s_tab), extra_specs=(tab, tab))


IDX_W = IDX_HEADS * IDX_DIM
IDX_RAW = IDX_W + LANES


IDX_K = 4 * IDX_DIM


def _idx_rope_kernel(x_ref, c_ref, s_ref, q_ref, k_ref, w_ref):
    c = c_ref[...]
    s = s_ref[...]
    lane = lax.broadcasted_iota(i32, c.shape, 1)
    first = (lane & (IDX_DIM // 2)) == 0
    low = lane < IDX_DIM

    def rope(blk):
        swapped = jnp.where(first, pltpu.roll(blk, LANES - IDX_DIM // 2, 1), pltpu.roll(blk, IDX_DIM // 2, 1))
        return blk * c + swapped * s

    def hi_lo(x):
        hi = x.astype(bf16).astype(f32)
        return hi, x - hi

    for k in range(IDX_W // LANES):
        hi, lo = hi_lo(rope(x_ref[:, k * LANES:(k + 1) * LANES]))
        hi_sw = pltpu.roll(hi, IDX_DIM, 1)
        lo_sw = pltpu.roll(lo, IDX_DIM, 1)
        base = 2 * k * IDX_K
        q_ref[:, base:base + LANES] = jnp.where(low, hi, hi_sw).astype(bf16)
        q_ref[:, base + LANES:base + IDX_K] = jnp.where(low, lo, 0.0).astype(bf16)
        q_ref[:, base + IDX_K:base + IDX_K + LANES] = jnp.where(low, hi_sw, hi).astype(bf16)
        q_ref[:, base + IDX_K + LANES:base + 2 * IDX_K] = jnp.where(low, lo_sw, 0.0).astype(bf16)
    kw = x_ref[:, IDX_W:IDX_RAW]
    hi, lo = hi_lo(rope(kw))
    k_ref[:, 0:LANES] = jnp.where(low, hi, pltpu.roll(lo, IDX_DIM, 1)).astype(bf16)
    k_ref[:, LANES:IDX_K] = jnp.where(low, hi, 0.0).astype(bf16)
    w_ref[...] = kw


def _idx_rope(raw, c_tab, s_tab):
    m = raw.shape[0]
    tm = min(m, 512)
    tab = pl.BlockSpec((tm, LANES), lambda i: (i, 0))
    return pl.pallas_call(
        _idx_rope_kernel,
        grid=(m // tm,),
        in_specs=[pl.BlockSpec((tm, IDX_RAW), lambda i: (i, 0)), tab, tab],
        out_specs=[pl.BlockSpec((tm, IDX_HEADS * IDX_K), lambda i: (i, 0)),
                   pl.BlockSpec((tm, IDX_K), lambda i: (i, 0)), tab],
        out_shape=[jax.ShapeDtypeStruct((m, IDX_HEADS * IDX_K), bf16), jax.ShapeDtypeStruct((m, IDX_K), bf16),
                   jax.ShapeDtypeStruct((m, LANES), f32)],
        compiler_params=_cparams(("parallel",), 8 * tm * IDX_RAW * 4 + 4 * tm * IDX_HEADS * IDX_K * 2),
        name="idx_rope",
    )(raw, c_tab, s_tab)


INT_MIN = -2 ** 31


DSA_T = 256
DSA_KG = 4
MASKED = -1e30


def _key_group(nkb):
    return DSA_KG if nkb % DSA_KG == 0 else 1


def _dsa_index_kernel(q_ref, k_ref, qw_ref, m_ref, key_scr, cut_s, *, topk, scale):
    i = pl.program_id(1)
    nkb, kb, tq = key_scr.shape
    qw_t = qw_ref[...].T
    q_chunk = (i * tq + lax.broadcasted_iota(i32, (1, tq), 1)) // CHUNK
    krow = lax.broadcasted_iota(i32, (kb, tq), 0)

    def score_block(j, c):
        kblk = k_ref[pl.ds(pl.multiple_of(j * kb, kb), kb), :]
        score = jnp.zeros((kb, tq), f32)
        for hd in range(IDX_HEADS):
            rel = lax.dot_general(kblk, q_ref[:, hd * IDX_K:(hd + 1) * IDX_K], _NT, preferred_element_type=f32)
            score = score + jnp.maximum(rel, 0.0) * qw_t[IDX_DIM + hd:IDX_DIM + hd + 1, :]
        score = score * scale
        adm = j * kb + krow < (q_chunk + 1) * CHUNK
        bits = pltpu.bitcast(score, i32)
        key = jnp.where(bits < 0, bits ^ 0x7FFFFFFF, bits)
        key_scr[j] = jnp.where(adm, key, INT_MIN)
        return c

    lax.fori_loop(0, i + 1, score_block, 0)

    kg = _key_group(nkb)
    ngroups = (i + kg) // kg

    def fill(j, c):
        key_scr[j] = jnp.full((kb, tq), INT_MIN, i32)
        return c

    lax.fori_loop(i + 1, ngroups * kg, fill, 0)

    def count_ge(cand):
        def grp(g, acc):
            for u in range(kg):
                ge = (key_scr[g * kg + u] >= cand).astype(f32)
                for r in range(kb // SUBLANES):
                    acc = acc + ge[r * SUBLANES:(r + 1) * SUBLANES, :]
            return acc

        acc = lax.fori_loop(0, ngroups, grp, jnp.zeros((SUBLANES, tq), f32))
        return jnp.sum(acc, axis=0, keepdims=True)

    kf = jnp.float32(topk)
    cur = jnp.where(count_ge(jnp.zeros((1, tq), i32)) >= kf, 0, INT_MIN).astype(i32)

    def bisect(it, cur):
        cand = cur + jnp.left_shift(jnp.int32(1), 30 - it)
        return jnp.where(count_ge(cand) >= kf, cand, cur)

    thr = jnp.maximum(lax.fori_loop(0, 31, bisect, cur), INT_MIN + 1)

    cut_s[...] = jnp.full((1, tq), nkb * kb, i32)
    surplus = count_ge(thr) - kf

    @pl.when(jnp.max(surplus) > 0.0)
    def _break_ties():
        def count_eq_below(cand):
            def grp(g, acc):
                for u in range(kg):
                    j = g * kg + u
                    hit = jnp.logical_and(key_scr[j] == thr, j * kb + krow < cand).astype(f32)
                    for r in range(kb // SUBLANES):
                        acc = acc + hit[r * SUBLANES:(r + 1) * SUBLANES, :]
                return acc

            acc = lax.fori_loop(0, ngroups, grp, jnp.zeros((SUBLANES, tq), f32))
            return jnp.sum(acc, axis=0, keepdims=True)

        need = count_eq_below(jnp.full((1, tq), nkb * kb, i32)) - surplus
        nbits = (nkb * kb).bit_length()

        def narrow(it, cur):
            cand = cur + jnp.left_shift(jnp.int32(1), nbits - 1 - it)
            return jnp.where(count_eq_below(cand) < need, cand, cur)

        cut_s[...] = lax.fori_loop(0, nbits, narrow, jnp.zeros((1, tq), i32)) + 1

    cut = cut_s[...]

    def write(j, c):
        key = key_scr[j]
        keep = jnp.logical_or(key > thr, jnp.logical_and(key == thr, j * kb + krow < cut))
        m_ref[0, j] = jnp.where(keep, 0.0, MASKED).T
        return c

    def clear(j, c):
        m_ref[0, j] = jnp.full((tq, kb), MASKED, f32)
        return c

    lax.fori_loop(0, i + 1, write, 0)
    lax.fori_loop(i + 1, nkb, clear, 0)


def _dsa_index(iq, ik, iw, batch, seq):
    tq = min(seq, DSA_T)
    nq = seq // tq
    topk = min(TOPK_MAX, seq // 4)
    scale = IDX_DIM ** -0.5 * IDX_HEADS ** -0.5
    return pl.pallas_call(
        functools.partial(_dsa_index_kernel, topk=topk, scale=scale),
        grid=(batch, nq),
        in_specs=[pl.BlockSpec((tq, IDX_HEADS * IDX_K), lambda b, i: (b * nq + i, 0)),
                  pl.BlockSpec((seq, IDX_K), lambda b, i: (b, 0)),
                  pl.BlockSpec((tq, LANES), lambda b, i: (b * nq + i, 0))],
        out_specs=pl.BlockSpec((1, nq, tq, tq), lambda b, i: (b * nq + i, 0, 0, 0)),
        out_shape=jax.ShapeDtypeStruct((batch * nq, nq, tq, tq), f32),
        scratch_shapes=[pltpu.VMEM((nq, tq, tq), i32), pltpu.VMEM((1, tq), i32)],
        compiler_params=_cparams(("parallel", "parallel"),
                                 8 * tq * seq * 4 + 4 * seq * IDX_K * 2 + 4 * tq * IDX_HEADS * IDX_K * 2),
        name="dsa_index",
    )(iq, ik, iw)


ATT_HG = 4


def _dsa_attn_kernel(q_ref, k_ref, v_ref, m_ref, o_ref):
    i = pl.program_id(2)
    _, nkb, tq, kb = m_ref.shape
    kg = _key_group(nkb)

    def group(g, carry):
        rows = pl.ds(pl.multiple_of(g * (kg * kb), kg * kb), kg * kb)
        parts = [m_ref[0, g * kg + u] for u in range(kg)]
        bias = jnp.concatenate(parts, axis=1) if kg > 1 else parts[0]
        out = []
        for h in range(ATT_HG):
            mx, den, acc = carry[h]
            sl = slice(h * HEAD_DIM, (h + 1) * HEAD_DIM)
            logits = lax.dot_general(q_ref[:, sl], k_ref[rows, sl], _NT, preferred_element_type=f32) + bias
            mx_new = jnp.maximum(mx, jnp.max(logits, axis=1, keepdims=True))
            alpha = jnp.exp(mx - mx_new)
            p = jnp.exp(logits - mx_new)
            den = alpha * den + jnp.sum(p, axis=1, keepdims=True)
            acc = alpha * acc + jnp.dot(p.astype(bf16), v_ref[rows, sl], preferred_element_type=f32)
            out.append((mx_new, den, acc))
        return tuple(out)

    init = tuple((jnp.full((tq, 1), MASKED, f32), jnp.zeros((tq, 1), f32), jnp.zeros((tq, HEAD_DIM), f32))
                 for _ in range(ATT_HG))
    final = lax.fori_loop(0, (i + kg) // kg, group, init)
    for h in range(ATT_HG):
        _, den, acc = final[h]
        o_ref[:, h * HEAD_DIM:(h + 1) * HEAD_DIM] = (acc / den).astype(o_ref.dtype)


def _dsa_attn(q, k, v, mask, batch, seq):
    m, w = q.shape
    _, nq, tq, _ = mask.shape
    gw = ATT_HG * HEAD_DIM
    qspec = pl.BlockSpec((tq, gw), lambda b, g, i: (b * nq + i, g))
    kvspec = pl.BlockSpec((seq, gw), lambda b, g, i: (b, g))
    vmem = 2 * (2 * tq * gw * 2 + 2 * seq * gw * 2 + tq * seq * 4) + 8 * tq * DSA_KG * tq * 4
    return pl.pallas_call(
        _dsa_attn_kernel,
        grid=(batch, w // gw, nq),
        in_specs=[qspec, kvspec, kvspec, pl.BlockSpec((1, nq, tq, tq), lambda b, g, i: (b * nq + i, 0, 0, 0))],
        out_specs=qspec,
        out_shape=jax.ShapeDtypeStruct((m, w), bf16),
        compiler_params=_cparams(("parallel", "parallel", "arbitrary"), vmem),
        name="dsa_attn",
    )(q, k, v, mask)


RET_HG = 4


def _retention_kernel(q_ref, k_ref, v_ref, g_ref, gn_ref, lg_ref, o_ref, state):
    c = pl.program_id(2)
    t = q_ref.shape[0]

    @pl.when(c == 0)
    def _init():
        state[...] = jnp.zeros_like(state)

    ri = lax.broadcasted_iota(i32, (t, t), 0)
    ci = lax.broadcasted_iota(i32, (t, t), 1)
    diff = (ri - ci).astype(f32)
    pos = lax.broadcasted_iota(i32, (t, 1), 0).astype(f32)
    for h in range(RET_HG):
        sl = slice(h * HEAD_DIM, (h + 1) * HEAD_DIM)
        lg = lg_ref[:, h * HEAD_DIM:h * HEAD_DIM + 1]
        q = q_ref[:, sl]
        k = k_ref[:, sl]
        v = v_ref[:, sl]
        decay = jnp.where(diff >= 0, jnp.exp(lg * jnp.maximum(diff, 0.0)), 0.0)
        scores = lax.dot_general(q, k, _NT, preferred_element_type=f32) * decay
        inner = jnp.dot(scores.astype(bf16), v, preferred_element_type=f32)
        st = state[h]
        cross = jnp.dot(q, st.astype(bf16), preferred_element_type=f32) * jnp.exp(lg * (pos + 1.0))
        kz = k.astype(f32) * jnp.exp(lg * (t - 1.0 - pos))
        kv = jnp.dot(kz.T.astype(bf16), v, preferred_element_type=f32)
        state[h] = jnp.exp(lg * t) * st + kv
        ret = inner + cross
        mu = jnp.mean(ret, axis=1, keepdims=True)
        rc = ret - mu
        var = jnp.mean(rc * rc, axis=1, keepdims=True)
        gate = g_ref[:, sl]
        out = rc * lax.rsqrt(var + LN_EPS) * gn_ref[:, sl] * (gate * _sigmoid(gate))
        o_ref[:, sl] = out.astype(o_ref.dtype)


def _retention(q, k, v, g, gn_g, batch, seq):
    m, w = q.shape
    heads = w // HEAD_DIM
    t = min(seq, 256)
    nc = seq // t
    gw = RET_HG * HEAD_DIM
    log_g = jnp.log1p(-jnp.exp2(-5.0 - jnp.arange(heads, dtype=f32)))
    lg = jnp.repeat(log_g, HEAD_DIM)[None, :]
    blk = pl.BlockSpec((t, gw), lambda b, gi, c: (b * nc + c, gi))
    vec = pl.BlockSpec((1, gw), lambda b, gi, c: (0, gi))
    return pl.pallas_call(
        _retention_kernel,
        grid=(batch, w // gw, nc),
        in_specs=[blk, blk, blk, blk, vec, vec],
        out_specs=blk,
        out_shape=jax.ShapeDtypeStruct((m, w), bf16),
        scratch_shapes=[pltpu.VMEM((RET_HG, HEAD_DIM, HEAD_DIM), f32)],
        compiler_params=_cparams(("parallel", "parallel", "arbitrary"), 16 * t * gw * 4 + 8 * t * t * 4),
        name="retention",
    )(q, k, v, g, gn_g[None, :], lg)


def _mixer_sparse_retention(h_hi, h_lo, pos2, w_in, w_out, gn_g, batch, seq):
    m, d = h_hi.shape
    gw = d // 2
    tm = min(m, 1024)
    tn = 512
    c128, s128 = _rope_tables(pos2, HEAD_DIM)
    c64, s64 = _rope_tables(pos2, IDX_DIM)
    w_t = jnp.swapaxes(w_in, 0, 1)
    rope = functools.partial(_proj_rope, h_hi, w_t, c_tab=c128, s_tab=s128, n=gw, tm=tm, tn=tn)
    plain = functools.partial(_mm, [h_hi], w_t, n=gw, tm=tm, tn=tn, transposed=True)
    aq = rope(col_off=0, scale=HEAD_DIM ** -0.5)
    ak = rope(col_off=gw, scale=1.0)
    av = plain(col_off=2 * gw, out_dtype=bf16)
    idx_raw = _mm3(h_hi, h_lo, w_t, col_off=3 * gw, n=IDX_RAW, tm=min(m, 512), tn=IDX_RAW // 3, transposed=True)
    iq, ik, iw = _idx_rope(idx_raw, c64, s64)
    mask = _dsa_index(iq, ik, iw, batch, seq)
    a_out = _dsa_attn(aq, ak, av, mask, batch, seq)
    b_off = 3 * gw + IDX_W + IDX_DIM + IDX_HEADS
    bq = rope(col_off=b_off, scale=1.0)
    bk = rope(col_off=b_off + gw, scale=HEAD_DIM ** -0.5)
    bv = plain(col_off=b_off + 2 * gw, out_dtype=bf16)
    bg = plain(col_off=b_off + 3 * gw, out_dtype=f32)
    b_out = _retention(bq, bk, bv, bg, gn_g, batch, seq)
    return _mm([a_out, b_out], w_out, col_off=0, n=d, tm=tm, tn=tn, out_dtype=f32)


MXU_N = 256


def _seg_sum64(x):
    r = lax.broadcasted_iota(i32, (MXU_N, MXU_N), 0) // C_HEAD_DIM
    c = lax.broadcasted_iota(i32, (MXU_N, MXU_N), 1) // C_HEAD_DIM
    ones = (r == c).astype(bf16)
    hi = x.astype(bf16)
    r1 = x - hi.astype(f32)
    mid = r1.astype(bf16)
    lo = (r1 - mid.astype(f32)).astype(bf16)
    cols = []
    for k in range(x.shape[1] // MXU_N):
        sl = slice(k * MXU_N, (k + 1) * MXU_N)
        s = jnp.dot(hi[:, sl], ones, preferred_element_type=f32)
        s += jnp.dot(mid[:, sl], ones, preferred_element_type=f32)
        s += jnp.dot(lo[:, sl], ones, preferred_element_type=f32)
        cols.append(s)
    return jnp.concatenate(cols, axis=1) if len(cols) > 1 else cols[0]


def _neg_softplus_neg(z):
    return jnp.minimum(z, 0.0) - jnp.log(1.0 + jnp.exp(-jnp.abs(z)))


def _rwkv_prep_kernel(p_ref, pp_ref, mu_ref, w0_ref, wup_ref, a0_ref, aup_ref, gup_ref, ka_ref, rk_ref,
                      r_o, w_o, k_o, a_o, v_o, g_o, bon_o, *, seq, gw):
    i = pl.program_id(0)
    tm = p_ref.shape[0]
    p = p_ref[...]
    prev_row = jnp.where((i * tm) % seq == 0, 0.0, pp_ref[SUBLANES - 1:SUBLANES, :])
    row = lax.broadcasted_iota(i32, (tm, 1), 0)
    shifted = jnp.where(row == 0, prev_row, pltpu.roll(p, 1, 0))
    pm = p + (shifted - p) * mu_ref[...]
    r = pm[:, 0:gw]
    k = pm[:, gw:2 * gw]
    v = pm[:, 2 * gw:3 * gw]
    o = 3 * gw
    dw = pm[:, o:o + C_DECAY_RANK]
    da = pm[:, o + C_DECAY_RANK:o + C_DECAY_RANK + C_ICLR_RANK]
    dg = pm[:, o + C_DECAY_RANK + C_ICLR_RANK:]
    w_log = _neg_softplus_neg(w0_ref[...] + _dot3(jnp.tanh(dw), wup_ref[...])) - 0.5
    decay = jnp.exp(-jnp.exp(w_log))
    a = _sigmoid(a0_ref[...] + _dot3(da, aup_ref[...]))
    g = _dot3(_sigmoid(dg), gup_ref[...])
    k2 = k * (1.0 + (a - 1.0) * ka_ref[...])
    r_o[...] = r
    w_o[...] = decay
    k_o[...] = k
    a_o[...] = a
    v_o[...] = v
    g_o[...] = g
    bon_o[...] = _seg_sum64(r * k2 * rk_ref[...]) * v


def _rwkv_prep(pc, mu, w0, w_up, a0, a_up, g_up, k_a, r_k, seq):
    m, cc = pc.shape
    gw = w0.shape[0]
    tm = 128
    nsub = tm // SUBLANES
    vec = pl.BlockSpec((1, gw), lambda i: (0, 0))
    out = pl.BlockSpec((tm, gw), lambda i: (i, 0))
    full = lambda a: pl.BlockSpec(a.shape, lambda i: (0, 0))
    return pl.pallas_call(
        functools.partial(_rwkv_prep_kernel, seq=seq, gw=gw),
        grid=(m // tm,),
        in_specs=[pl.BlockSpec((tm, cc), lambda i: (i, 0)),
                  pl.BlockSpec((SUBLANES, cc), lambda i: (jnp.maximum(i * nsub - 1, 0), 0)),
                  pl.BlockSpec((1, cc), lambda i: (0, 0)),
                  vec, full(w_up), vec, full(a_up), full(g_up), vec, vec],
        out_specs=[out] * 7,
        out_shape=[jax.ShapeDtypeStruct((m, gw), f32)] * 7,
        compiler_params=_cparams(("parallel",), 2 * tm * cc * 4 + 2 * 7 * tm * gw * 4 + 16 * tm * gw * 4),
        name="rwkv_prep",
    )(pc, pc, mu[None, :], w0[None, :], w_up, a0[None, :], a_up, g_up, k_a[None, :], r_k.reshape(1, -1))


SCAN_T = 64
SCAN_G = 16


def _rwkv_scan_kernel(w_ref, a_ref, k_ref, r_ref, v_ref, kkw_ref, kaw_ref, y_ref,
                      z_ref, w_s, kk_s, ka_s, k_s, r_s, v_s, y_s):
    c = pl.program_id(0)
    batch, steps, width = w_ref.shape
    nslab = z_ref.shape[0]
    heads = width // C_HEAD_DIM
    half = batch * heads

    @pl.when(c == 0)
    def _init():
        z_ref[...] = jnp.zeros_like(z_ref)

    kkw = kkw_ref[...]
    kaw = kaw_ref[...]

    def by_head(ref, rows, copies):
        per_batch = [ref[b, rows, :].reshape(SCAN_G, heads, C_HEAD_DIM) for b in range(batch)]
        return jnp.concatenate(per_batch * copies, axis=1)

    def key_tiles(ref, rows):
        return jnp.swapaxes(by_head(ref, rows, 2), 1, 2)

    def derive(g, carry):
        rows = pl.ds(pl.multiple_of(g * SCAN_G, SCAN_G), SCAN_G)
        a = key_tiles(a_ref, rows)
        kraw = key_tiles(k_ref, rows)
        kk = kraw * kkw
        kk = kk * lax.rsqrt(jnp.maximum(jnp.sum(kk * kk, axis=1, keepdims=True), 1e-24))
        w_s[rows] = key_tiles(w_ref, rows)
        r_s[rows] = key_tiles(r_ref, rows)
        kk_s[rows] = kk
        ka_s[rows] = kk * a
        k_s[rows] = kraw * (1.0 + (a - 1.0) * kaw)
        vt = jnp.swapaxes(by_head(v_ref, rows, 1), 1, 2)
        v_s[rows] = jnp.concatenate([vt[:, :nslab, :], vt[:, nslab:, :]], axis=2)
        return carry

    lax.fori_loop(0, steps // SCAN_G, derive, 0)

    def step(t, carry):
        w = w_s[t]
        kk = kk_s[t]
        ka = ka_s[t]
        k = k_s[t]
        r = r_s[t]
        vrows = v_s[t]
        for s in range(nslab):
            z = z_ref[s]
            sk = jnp.sum(z * kk, axis=0, keepdims=True)
            zn = z * w - ka * sk + k * vrows[s:s + 1, :]
            z_ref[s] = zn
            y_s[t, s:s + 1, :] = jnp.sum(zn * r, axis=0, keepdims=True)
        return carry

    lax.fori_loop(0, steps, step, 0)

    def restore(g, carry):
        rows = pl.ds(pl.multiple_of(g * SCAN_G, SCAN_G), SCAN_G)
        y = y_s[rows]
        yt = jnp.concatenate([y[:, :, :half], y[:, :, half:]], axis=1)
        yh = jnp.swapaxes(yt, 1, 2)
        for b in range(batch):
            y_ref[b, rows, :] = yh[:, b * heads:(b + 1) * heads, :].reshape(SCAN_G, width)
        return carry

    lax.fori_loop(0, steps // SCAN_G, restore, 0)


def _rwkv_scan(w, a, k, r, v, k_k, k_a, batch, seq):
    width = w.shape[1]
    heads = width // C_HEAD_DIM
    assert 2 * batch * heads == LANES
    nslab = C_HEAD_DIM // 2
    steps = min(seq, SCAN_T)
    consts = [jnp.tile(p.reshape(heads, C_HEAD_DIM).T, (1, 2 * batch)) for p in (k_k, k_a)]
    nat = pl.BlockSpec((batch, steps, width), lambda c: (0, c, 0))
    const_spec = pl.BlockSpec((C_HEAD_DIM, LANES), lambda c: (0, 0))
    tile_bytes = steps * C_HEAD_DIM * LANES * 4
    y = pl.pallas_call(
        _rwkv_scan_kernel,
        grid=(seq // steps,),
        in_specs=[nat] * 5 + [const_spec, const_spec],
        out_specs=nat,
        out_shape=jax.ShapeDtypeStruct((batch, seq, width), f32),
        scratch_shapes=[pltpu.VMEM((nslab, C_HEAD_DIM, LANES), f32)]
        + [pltpu.VMEM((steps, C_HEAD_DIM, LANES), f32)] * 5
        + [pltpu.VMEM((steps, nslab, LANES), f32)] * 2,
        compiler_params=_cparams(("arbitrary",), 6 * tile_bytes + 2 * 6 * batch * steps * width * 4
                                 + 8 * tile_bytes),
        name="rwkv_scan",
    )(*(x.reshape(batch, seq, width) for x in (w, a, k, r, v)), *consts)
    return y.reshape(batch * seq, width)


def _rwkv_post_kernel(y_ref, g_ref, bon_ref, lng_ref, lnb_ref, o_ref):
    y = y_ref[...]
    mu = _seg_sum64(y) * (1.0 / C_HEAD_DIM)
    yc = y - mu
    var = _seg_sum64(yc * yc) * (1.0 / C_HEAD_DIM)
    yn = yc * lax.rsqrt(var + C_EPS) * lng_ref[...] + lnb_ref[...]
    o_ref[...] = ((yn + bon_ref[...]) * g_ref[...]).astype(o_ref.dtype)


def _rwkv_post(y, g, bonus, ln_g, ln_b):
    m, gw = y.shape
    tm = 256
    blk = pl.BlockSpec((tm, gw), lambda i: (i, 0))
    vec = pl.BlockSpec((1, gw), lambda i: (0, 0))
    return pl.pallas_call(
        _rwkv_post_kernel,
        grid=(m // tm,),
        in_specs=[blk, blk, blk, vec, vec],
        out_specs=blk,
        out_shape=jax.ShapeDtypeStruct((m, gw), bf16),
        compiler_params=_cparams(("parallel",), 16 * tm * gw * 4),
        name="rwkv_post",
    )(y, g, bonus, ln_g[None, :], ln_b[None, :])


GELU_C = 0.7978845608028654


def _lru_kernel(px_ref, pg_ref, cw_ref, cb_ref, wa_ref, ba_ref, wx_ref, bx_ref, lam_ref, o_ref,
                tail, hcar, a_s, b_s):
    c = pl.program_id(1)
    t = px_ref.shape[0]

    @pl.when(c == 0)
    def _init():
        tail[...] = jnp.zeros_like(tail)
        hcar[...] = jnp.zeros_like(hcar)

    x = px_ref[...]
    ext = jnp.concatenate([tail[...], x], axis=0)
    xc = cb_ref[...]
    for j in range(D_CONV):
        off = SUBLANES - (D_CONV - 1) + j
        xc = xc + cw_ref[j:j + 1, :] * ext[off:off + t, :]
    tail[...] = x[t - SUBLANES:, :]
    lam = lam_ref[...]
    sp = jnp.maximum(-lam, 0.0) + jnp.log(1.0 + jnp.exp(-jnp.abs(lam)))
    bw = wa_ref.shape[1]
    for n in range(wa_ref.shape[0]):
        sl = slice(n * bw, (n + 1) * bw)
        xb = xc[:, sl]
        rg = _sigmoid(_dot3(xb, wa_ref[n]) + ba_ref[:, sl])
        ig = _sigmoid(_dot3(xb, wx_ref[n]) + bx_ref[:, sl])
        log_a = -LRU_C * rg * sp[:, sl]
        a_s[:, sl] = jnp.exp(log_a)
        th = jnp.tanh(log_a)
        one_minus_a2 = -2.0 * th / (1.0 - th)
        b_s[:, sl] = jnp.sqrt(one_minus_a2) * (ig * xb)

    sub = lax.broadcasted_iota(i32, (SUBLANES, a_s.shape[1]), 0)

    def rows8(g, h):
        r0 = pl.multiple_of(g * SUBLANES, SUBLANES)
        a = a_s[pl.ds(r0, SUBLANES), :]
        b = b_s[pl.ds(r0, SUBLANES), :]
        for s in (1, 2, 4):
            b = a * jnp.where(sub >= s, pltpu.roll(b, s, 0), 0.0) + b
            a = a * jnp.where(sub >= s, pltpu.roll(a, s, 0), 1.0)
        hs = a * h + b
        b_s[pl.ds(r0, SUBLANES), :] = hs
        return hs[SUBLANES - 1:SUBLANES, :]

    hcar[...] = lax.fori_loop(0, t // SUBLANES, rows8, hcar[...])
    gate = pg_ref[...]
    gelu = 0.5 * gate * (1.0 + jnp.tanh(GELU_C * (gate + 0.044715 * (gate * gate * gate))))
    o_ref[...] = (b_s[...] * gelu).astype(o_ref.dtype)


def _lru(px, pg, conv_w, conv_b, w_a, b_a, w_x, b_x, lam, batch, seq):
    m, w = px.shape
    t = min(seq, 256)
    nc = seq // t
    blk = pl.BlockSpec((t, w), lambda b, c: (b * nc + c, 0))
    vec = pl.BlockSpec((1, w), lambda b, c: (0, 0))
    wblk = pl.BlockSpec(w_a.shape, lambda b, c: (0, 0, 0))
    return pl.pallas_call(
        _lru_kernel,
        grid=(batch, nc),
        in_specs=[blk, blk, pl.BlockSpec((D_CONV, w), lambda b, c: (0, 0)), vec, wblk, vec, wblk, vec, vec],
        out_specs=blk,
        out_shape=jax.ShapeDtypeStruct((m, w), bf16),
        scratch_shapes=[pltpu.VMEM((SUBLANES, w), f32), pltpu.VMEM((1, w), f32),
                        pltpu.VMEM((t, w), f32), pltpu.VMEM((t, w), f32)],
        compiler_params=_cparams(("parallel", "arbitrary"), 16 * t * w * 4),
        name="lru",
    )(px, pg, conv_w, conv_b[None, :], w_a, b_a[None, :], w_x, b_x[None, :], lam[None, :])


def _mixer_rwkv_lru(h_hi, w_in, w_out, mu, w0, w_up, a0, a_up, g_up, k_k, k_a, r_k, ln_g, ln_b,
                    conv_w, conv_b, w_a, b_a, w_x, b_x, lam, batch, seq):
    m, d = h_hi.shape
    gw = d // 2
    c_cols = 3 * gw + C_DECAY_RANK + C_ICLR_RANK + C_GATE_RANK
    tm = min(m, 1024)
    tn = 512
    proj = functools.partial(_mm, [h_hi], w_in, tm=tm, tn=tn, out_dtype=f32)
    pc = proj(col_off=0, n=c_cols)
    pg = proj(col_off=c_cols, n=gw)
    px = proj(col_off=c_cols + gw, n=gw)
    r, w, k, a, v, g, bonus = _rwkv_prep(pc, mu, w0, w_up, a0, a_up, g_up, k_a, r_k, seq)
    y = _rwkv_scan(w, a, k, r, v, k_k, k_a, batch, seq)
    c_out = _rwkv_post(y, g, bonus, ln_g, ln_b)
    d_out = _lru(px, pg, conv_w, conv_b, w_a, b_a, w_x, b_x, lam, batch, seq)
    return _mm([c_out, d_out], w_out, col_off=0, n=d, tm=tm, tn=tn, out_dtype=f32)


ADA_ROWS = 16


def _ada_kernel(c_ref, w_ref, b_ref, o_ref):
    c = c_ref[...]
    s = (c * _sigmoid(c)).astype(bf16)
    o_ref[...] = jnp.dot(s, w_ref[...].astype(bf16), preferred_element_type=f32) + b_ref[...]


def _ada(c, ada_w, ada_b):
    batch, d = c.shape
    n = ada_w.shape[1]
    tn = 512
    cp = jnp.zeros((ADA_ROWS, d), f32).at[:batch].set(c)
    out = pl.pallas_call(
        _ada_kernel,
        grid=(n // tn,),
        in_specs=[pl.BlockSpec((ADA_ROWS, d), lambda j: (0, 0)),
                  pl.BlockSpec((d, tn), lambda j: (0, j)),
                  pl.BlockSpec((1, tn), lambda j: (0, j))],
        out_specs=pl.BlockSpec((ADA_ROWS, tn), lambda j: (0, j)),
        out_shape=jax.ShapeDtypeStruct((ADA_ROWS, n), f32),
        compiler_params=_cparams(("parallel",), 3 * d * tn * 4),
        name="ada",
    )(cp, ada_w, ada_b[None, :])
    return out[:batch]


def kernel(x, c, positions, ada_w, ada_b, ada_table, ln_g, ln_b, ab_w_in, ab_w_out, ret_gn_g, cd_w_in, cd_w_out, rwkv_mu, rwkv_w0, rwkv_w_up, rwkv_a0, rwkv_a_up, rwkv_g_up, rwkv_k_k, rwkv_k_a, rwkv_r_k, rwkv_ln_g, rwkv_ln_b, lru_conv_w, lru_conv_b, lru_w_a, lru_b_a, lru_w_x, lru_b_x, lru_lambda, moe_w_grp, moe_b_grp, moe_w_exp, moe_b_exp, moe_w_gate, moe_w_up, moe_w_down):
    batch, seq, d = x.shape
    x2 = x.reshape(batch * seq, d)
    pos2 = positions.reshape(batch * seq, 1)
    ada = _ada(c, ada_w, ada_b).reshape(batch, 6, 1, d)
    mods = [ada + ada_table[layer][None, :, None, :] for layer in range(DEPTH)]
    h_hi, h_lo = _modcast(x2, mods[0][:, 1], mods[0][:, 0], seq)
    for layer in range(DEPTH):
        shift_m, scale_m, gate_m, shift_f, scale_f, gate_f = (mods[layer][:, i] for i in range(6))
        j = layer // 2
        if layer % 2 == 0:
            y = _mixer_sparse_retention(h_hi, h_lo, pos2, ab_w_in[j], ab_w_out[j], ret_gn_g[j], batch, seq)
        else:
            y = _mixer_rwkv_lru(h_hi, cd_w_in[j], cd_w_out[j], rwkv_mu[j], rwkv_w0[j], rwkv_w_up[j],
                                rwkv_a0[j], rwkv_a_up[j], rwkv_g_up[j], rwkv_k_k[j], rwkv_k_a[j],
                                rwkv_r_k[j], rwkv_ln_g[j], rwkv_ln_b[j], lru_conv_w[j], lru_conv_b[j],
                                lru_w_a[j], lru_b_a[j], lru_w_x[j], lru_b_x[j], lru_lambda[j], batch, seq)
        last = layer + 1 == DEPTH
        next_mod = None if last else (mods[layer + 1][:, 1], mods[layer + 1][:, 0])
        out = _moe_layer(x2, y, gate_m, ln_g[layer, 0], ln_b[layer, 0], scale_f, shift_f, gate_f,
                         ln_g[layer, 1], ln_b[layer, 1], moe_w_grp[layer], moe_b_grp[layer],
                         moe_w_exp[layer], moe_b_exp[layer], moe_w_gate, moe_w_up, moe_w_down, layer, seq,
                         next_mod)
        x2, h_hi, h_lo = (out, None, None) if last else out
    return x2.reshape(batch, seq, d)
```
